```python
import jax
import jax.numpy as jnp
from jax import lax
import numpy as np

D_MODEL = 1024
BATCH = 4
SEQ = 8192
DEPTH = 2

GRID_W = 64
CTX_LEN = 256
HEAD_DIM = 64
ROPE_BASE = 10000.0
EPS = 1e-6
NEG_INF = -1e30
Q_BLOCK = 128

MLA_HEADS = 8
MLA_NOPE = 64
MLA_ROPE = 32
MLA_V = 64
MLA_Q_RANK = 256
MLA_KV_RANK = 128
MLA_IN = MLA_Q_RANK + MLA_KV_RANK + MLA_ROPE

NA_HEADS = 8
NA_KR_MAX = 8
NA_KC = 16
NA_CB = 16
NA_KB = 2 * NA_KC

SWA_HEADS = 16
SWA_KV_HEADS = 2
SWA_WINDOW = 128
SWA_BLOCK = 128

D_FF = (-(-8 * D_MODEL // 3) + 255) // 256 * 256

EVEN_IN = MLA_IN + 3 * NA_HEADS * HEAD_DIM
EVEN_OUT = MLA_HEADS * MLA_V + NA_HEADS * HEAD_DIM
ODD_IN = (SWA_HEADS + 2 * SWA_KV_HEADS) * HEAD_DIM
ODD_OUT = SWA_HEADS * HEAD_DIM
N_EVEN = (DEPTH + 1) // 2
N_ODD = DEPTH // 2

kernel_name = 'hybrid_mla_natten_swa_dit'


def rms_norm(x, gain=None):
    xf = x.astype(jnp.float32)
    y = xf * lax.rsqrt(jnp.mean(xf * xf, axis=-1, keepdims=True) + EPS)
    if gain is not None:
        y = y * gain.astype(jnp.float32)
    return y.astype(x.dtype)


def modulate(x, shift, scale):
    return rms_norm(x) * (1 + scale) + shift


def axial_rope(x, rows, cols):
    half = x.shape[-1] // 2
    inv = ROPE_BASE ** (-jnp.arange(0, half, 2, dtype=jnp.float32) / half)

    def rot(xa, pos):
        ang = pos.astype(jnp.float32)[:, None] * inv
        cos = jnp.cos(ang)[None, :, None, :]
        sin = jnp.sin(ang)[None, :, None, :]
        x1, x2 = jnp.split(xa.astype(jnp.float32), 2, axis=-1)
        return jnp.concatenate([x1 * cos - x2 * sin, x1 * sin + x2 * cos], axis=-1)

    xr, xc = jnp.split(x, 2, axis=-1)
    return jnp.concatenate([rot(xr, rows), rot(xc, cols)], axis=-1).astype(x.dtype)


def swiglu(h, w_gate_up, w_down):
    g, u = jnp.split(h @ w_gate_up, 2, axis=-1)
    return (jax.nn.silu(g) * u) @ w_down


def mla_heads(p_lat, p_ctx, rows, cols, q_norm, kv_norm, w_q_up, w_uk, w_uv, ctx_out):
    b, s = p_lat.shape[:2]
    n_ctx = p_ctx.shape[1]
    scale = (MLA_NOPE + MLA_ROPE) ** -0.5

    def queries(p, rotate):
        q = rms_norm(p[..., :MLA_Q_RANK], q_norm) @ w_q_up
        q = q.reshape(p.shape[0], p.shape[1], MLA_HEADS, MLA_NOPE + MLA_ROPE)
        q_nope, q_rope = q[..., :MLA_NOPE], q[..., MLA_NOPE:]
        if rotate:
            q_rope = axial_rope(q_rope, rows, cols)
        return jnp.einsum('bshn,hcn->bshc', q_nope, w_uk), q_rope

    def keys(p, rotate):
        c_kv = rms_norm(p[..., MLA_Q_RANK:MLA_Q_RANK + MLA_KV_RANK], kv_norm)
        k_rope = p[..., MLA_Q_RANK + MLA_KV_RANK:][:, :, None, :]
        if rotate:
            k_rope = axial_rope(k_rope, rows, cols)
        return c_kv, k_rope[:, :, 0]

    def attend(q_lat, q_rope, c_kv, k_rope):
        sc = jnp.einsum('bqhc,bkc->bhqk', q_lat, c_kv) + jnp.einsum('bqhr,bkr->bhqk', q_rope, k_rope)
        p = jax.nn.softmax(sc.astype(jnp.float32) * scale, axis=-1).astype(c_kv.dtype)
        return jnp.einsum('bhqk,bkc->bqhc', p, c_kv)

    ckv_c, kr_c = keys(p_ctx, False)
    ckv_l, kr_l = keys(p_lat, True)
    ckv_all = jnp.concatenate([ckv_c, ckv_l], axis=1)
    kr_all = jnp.concatenate([kr_c, kr_l], axis=1)
    ql, qr = queries(p_lat, True)
    nb = s // Q_BLOCK

    def to_blocks(t):
        return t.reshape(b, nb, Q_BLOCK, *t.shape[2:]).swapaxes(0, 1)

    o = lax.map(lambda a: attend(a[0], a[1], ckv_all, kr_all), (to_blocks(ql), to_blocks(qr)))
    o = o.swapaxes(0, 1).reshape(b, s, MLA_HEADS, MLA_KV_RANK)
    o_lat = jnp.einsum('bshc,hcv->bshv', o, w_uv).reshape(b, s, MLA_HEADS * MLA_V)
    o_ctx = None
    if ctx_out:
        qlc, qrc = queries(p_ctx, False)
        oc = attend(qlc, qrc, ckv_c, kr_c)
        o_ctx = jnp.einsum('bshc,hcv->bshv', oc, w_uv).reshape(b, n_ctx, MLA_HEADS * MLA_V)
    return o_lat, o_ctx


def na_heads(q, k, v, qc, kc, vc, rel_bias, ctx_out):
    b, s, h, d = q.shape
    n_ctx = kc.shape[1]
    n_rows = s // GRID_W
    kr = min(NA_KR_MAX, n_rows)
    n_cb = GRID_W // NA_CB
    scale = d ** -0.5
    qcol = np.arange(GRID_W).reshape(n_cb, NA_CB)
    q_start = np.clip(qcol - NA_KC // 2, 0, GRID_W - NA_KC)
    kb_start = np.clip(np.arange(n_cb) * NA_CB - NA_KC // 2, 0, GRID_W - NA_KB)
    kcol = kb_start[:, None] + np.arange(NA_KB)
    col_ok = (kcol[:, None, :] >= q_start[..., None]) & (kcol[:, None, :] < q_start[..., None] + NA_KC)
    dc_idx = np.clip(kcol[:, None, :] - qcol[..., None] + NA_KC - 1, 0, 2 * NA_KC - 2)
    col_bias = rel_bias[:, :, dc_idx].astype(jnp.float32)
    kg = k.reshape(b, n_rows, GRID_W, h, d)
    vg = v.reshape(b, n_rows, GRID_W, h, d)
    q_rows = q.reshape(b, n_rows, n_cb, NA_CB, h, d).swapaxes(0, 1)

    def row(args):
        r, qr = args
        rs = jnp.clip(r - kr // 2, 0, n_rows - kr)
        k_blk = lax.dynamic_slice_in_dim(kg, rs, kr, axis=1)[:, :, kcol]
        v_blk = lax.dynamic_slice_in_dim(vg, rs, kr, axis=1)[:, :, kcol]
        s_lat = jnp.einsum('bnqhd,bknmhd->bhnqkm', qr, k_blk).astype(jnp.float32) * scale
        dr_idx = rs + jnp.arange(kr) - r + NA_KR_MAX - 1
        s_lat = s_lat + col_bias[:, dr_idx].transpose(0, 2, 3, 1, 4)[None]
        s_lat = jnp.where(col_ok[:, :, None, :], s_lat, NEG_INF).reshape(b, h, n_cb, NA_CB, kr * NA_KB)
        s_ctx = jnp.einsum('bnqhd,bchd->bhnqc', qr, kc).astype(jnp.float32) * scale
        p = jax.nn.softmax(jnp.concatenate([s_ctx, s_lat], axis=-1), axis=-1).astype(v.dtype)
        p_lat = p[..., n_ctx:].reshape(b, h, n_cb, NA_CB, kr, NA_KB)
        return (jnp.einsum('bhnqc,bchd->bnqhd', p[..., :n_ctx], vc)
                + jnp.einsum('bhnqkm,bknmhd->bnqhd', p_lat, v_blk))

    o = lax.map(row, (jnp.arange(n_rows), q_rows))
    o_lat = o.swapaxes(0, 1).reshape(b, s, h * d)
    o_ctx = None
    if ctx_out:
        sc = jnp.einsum('bqhd,bkhd->bhqk', qc, kc).astype(jnp.float32) * scale
        pc = jax.nn.softmax(sc, axis=-1).astype(vc.dtype)
        o_ctx = jnp.einsum('bhqk,bkhd->bqhd', pc, vc).reshape(b, n_ctx, h * d)
    return o_lat, o_ctx


def swa_heads(q, k, v, qc, kc, vc, sinks, ctx_out):
    b, s, h, d = q.shape
    kvh = k.shape[2]
    g = h // kvh
    n_ctx = kc.shape[1]
    scale = d ** -0.5
    nb = s // SWA_BLOCK
    n_side = -(-SWA_WINDOW // SWA_BLOCK)
    pad = n_side * SWA_BLOCK
    kw = (2 * n_side + 1) * SWA_BLOCK

    def windows(t):
        tp = jnp.pad(t, ((0, 0), (pad, pad), (0, 0), (0, 0))).reshape(b, nb + 2 * n_side, SWA_BLOCK, kvh, d)
        return jnp.concatenate([tp[:, j:j + nb] for j in range(2 * n_side + 1)], axis=2).swapaxes(0, 1)

    q_pos = jnp.arange(s).reshape(nb, SWA_BLOCK)
    k_pos = (jnp.arange(nb)[:, None] - n_side) * SWA_BLOCK + jnp.arange(kw)[None, :]
    sink_col = sinks.astype(jnp.float32).reshape(kvh, g, 1, 1)

    def attend(qb, kb, vb, band_ok):
        tq = qb.shape[1]
        qg = qb.reshape(b, tq, kvh, g, d)
        s_ctx = jnp.einsum('bqkgd,bckd->bkgqc', qg, kc).astype(jnp.float32) * scale
        parts = [jnp.broadcast_to(sink_col, (b, kvh, g, tq, 1)), s_ctx]
        if kb is not None:
            s_lat = jnp.einsum('bqkgd,bmkd->bkgqm', qg, kb).astype(jnp.float32) * scale
            parts.append(jnp.where(band_ok, s_lat, NEG_INF))
        p = jax.nn.softmax(jnp.concatenate(parts, axis=-1), axis=-1).astype(vc.dtype)
        o = jnp.einsum('bkgqc,bckd->bqkgd', p[..., 1:1 + n_ctx], vc)
        if kb is not None:
            o = o + jnp.einsum('bkgqm,bmkd->bqkgd', p[..., 1 + n_ctx:], vb)
        return o.reshape(b, tq, h * d)

    def block(args):
        qb, kb, vb, qp, kp = args
        ok = (jnp.abs(kp[None, :] - qp[:, None]) <= SWA_WINDOW) & (kp >= 0)[None, :] & (kp < s)[None, :]
        return attend(qb, kb, vb, ok)

    q_blocks = q.reshape(b, nb, SWA_BLOCK, h, d).swapaxes(0, 1)
    o = lax.map(block, (q_blocks, windows(k), windows(v), q_pos, k_pos))
    o_lat = o.swapaxes(0, 1).reshape(b, s, h * d)
    o_ctx = attend(qc, None, None, None) if ctx_out else None
    return o_lat, o_ctx


def even_mixer(a_lat, a_ctx, rows, cols, w_in, q_norm, kv_norm, w_q_up, w_uk, w_uv, rel_bias, w_out, ctx_out):
    b, s, _ = a_lat.shape
    n_ctx = a_ctx.shape[1]
    p_lat = a_lat @ w_in
    p_ctx = a_ctx @ w_in
    o_mla_lat, o_mla_ctx = mla_heads(p_lat[..., :MLA_IN], p_ctx[..., :MLA_IN], rows, cols,
                                     q_norm, kv_norm, w_q_up, w_uk, w_uv, ctx_out)
    na_l = p_lat[..., MLA_IN:].reshape(b, s, 3, NA_HEADS, HEAD_DIM)
    na_c = p_ctx[..., MLA_IN:].reshape(b, n_ctx, 3, NA_HEADS, HEAD_DIM)
    o_na_lat, o_na_ctx = na_heads(na_l[:, :, 0], na_l[:, :, 1], na_l[:, :, 2],
                                  na_c[:, :, 0], na_c[:, :, 1], na_c[:, :, 2], rel_bias, ctx_out)
    o_lat = jnp.concatenate([o_mla_lat, o_na_lat], axis=-1) @ w_out
    o_ctx = jnp.concatenate([o_mla_ctx, o_na_ctx], axis=-1) @ w_out if ctx_out else None
    return o_lat, o_ctx


def odd_mixer(a_lat, a_ctx, rows, cols, w_in, sinks, w_out, ctx_out):
    b, s, _ = a_lat.shape
    n_ctx = a_ctx.shape[1]

    def split(p, t):
        q = p[..., :SWA_HEADS * HEAD_DIM].reshape(b, t, SWA_HEADS, HEAD_DIM)
        kv = p[..., SWA_HEADS * HEAD_DIM:].reshape(b, t, 2, SWA_KV_HEADS, HEAD_DIM)
        return q, kv[:, :, 0], kv[:, :, 1]

    q, k, v = split(a_lat @ w_in, s)
    q = axial_rope(q, rows, cols)
    k = axial_rope(k, rows, cols)
    qc, kc, vc = split(a_ctx @ w_in, n_ctx)
    o_lat, o_ctx = swa_heads(q, k, v, qc, kc, vc, sinks, ctx_out)
    return o_lat @ w_out, (o_ctx @ w_out if ctx_out else None)


def setup_inputs(seed: int = 0) -> dict:
    key = jax.random.key(seed)
    ks = jax.random.split(key, 20)
    nrm = jax.random.normal
    f32 = jnp.float32
    return {
        'x': nrm(ks[0], (BATCH, SEQ, D_MODEL), f32),
        'c': nrm(ks[1], (BATCH, D_MODEL), f32),
        'ctx': nrm(ks[2], (BATCH, CTX_LEN, D_MODEL), f32),
        'c_ctx': nrm(ks[3], (D_MODEL,), f32),
        'mod_w': nrm(ks[4], (DEPTH, D_MODEL, 6 * D_MODEL), f32) * (0.5 * D_MODEL ** -0.5),
        'mod_b': nrm(ks[5], (DEPTH, 6 * D_MODEL), f32) * 0.02,
        'even_w_in': nrm(ks[6], (N_EVEN, D_MODEL, EVEN_IN), f32) * D_MODEL ** -0.5,
        'mla_q_norm': 1.0 + 0.02 * nrm(ks[7], (N_EVEN, MLA_Q_RANK), f32),
        'mla_kv_norm': 1.0 + 0.02 * nrm(ks[8], (N_EVEN, MLA_KV_RANK), f32),
        'mla_w_q_up': nrm(ks[9], (N_EVEN, MLA_Q_RANK, MLA_HEADS * (MLA_NOPE + MLA_ROPE)), f32) * MLA_Q_RANK ** -0.5,
        'mla_w_uk': nrm(ks[10], (N_EVEN, MLA_HEADS, MLA_KV_RANK, MLA_NOPE), f32) * MLA_NOPE ** -0.5,
        'mla_w_uv': nrm(ks[11], (N_EVEN, MLA_HEADS, MLA_KV_RANK, MLA_V), f32) * MLA_KV_RANK ** -0.5,
        'na_rel_bias': 0.2 * nrm(ks[12], (N_EVEN, NA_HEADS, 2 * NA_KR_MAX - 1, 2 * NA_KC - 1), f32),
        'even_w_out': nrm(ks[13], (N_EVEN, EVEN_OUT, D_MODEL), f32) * EVEN_OUT ** -0.5,
        'odd_w_in': nrm(ks[14], (N_ODD, D_MODEL, ODD_IN), f32) * D_MODEL ** -0.5,
        'swa_sinks': nrm(ks[15], (N_ODD, SWA_HEADS), f32),
        'odd_w_out': nrm(ks[16], (N_ODD, ODD_OUT, D_MODEL), f32) * ODD_OUT ** -0.5,
        'ffn_w_gate_up': nrm(ks[17], (DEPTH, D_MODEL, 2 * D_FF), f32) * D_MODEL ** -0.5,
        'ffn_w_down': nrm(ks[18], (DEPTH, D_FF, D_MODEL), f32) * D_FF ** -0.5,
        'final_norm': 1.0 + 0.02 * nrm(ks[19], (D_MODEL,), f32),
    }


def reference(x, c, ctx, c_ctx, mod_w, mod_b, even_w_in, mla_q_norm, mla_kv_norm, mla_w_q_up, mla_w_uk,
              mla_w_uv, na_rel_bias, even_w_out, odd_w_in, swa_sinks, odd_w_out, ffn_w_gate_up, ffn_w_down,
              final_norm):
    s = x.shape[1]
    t = jnp.arange(s, dtype=jnp.int32)
    rows, cols = t // GRID_W, t % GRID_W
    h_lat, h_ctx = x, ctx
    silu_c = jax.nn.silu(c)[:, None, :]
    silu_cc = jax.nn.silu(c_ctx)[None, None, :]
    for l in range(DEPTH):
        ctx_out = l < DEPTH - 1
        sh_a, sc_a, g_a, sh_f, sc_f, g_f = jnp.split(silu_c @ mod_w[l] + mod_b[l], 6, axis=-1)
        csh_a, csc_a, cg_a, csh_f, csc_f, cg_f = jnp.split(silu_cc @ mod_w[l] + mod_b[l], 6, axis=-1)
        a_lat = modulate(h_lat, sh_a, sc_a)
        a_ctx = modulate(h_ctx, csh_a, csc_a)
        if l % 2 == 0:
            e = l // 2
            o_lat, o_ctx = even_mixer(a_lat, a_ctx, rows, cols, even_w_in[e], mla_q_norm[e], mla_kv_norm[e],
                                      mla_w_q_up[e], mla_w_uk[e], mla_w_uv[e], na_rel_bias[e], even_w_out[e],
                                      ctx_out)
        else:
            o = l // 2
            o_lat, o_ctx = odd_mixer(a_lat, a_ctx, rows, cols, odd_w_in[o], swa_sinks[o], odd_w_out[o], ctx_out)
        h_lat = h_lat + g_a * o_lat
        h_lat = h_lat + g_f * swiglu(modulate(h_lat, sh_f, sc_f), ffn_w_gate_up[l], ffn_w_down[l])
        if ctx_out:
            h_ctx = h_ctx + cg_a * o_ctx
            h_ctx = h_ctx + cg_f * swiglu(modulate(h_ctx, csh_f, csc_f), ffn_w_gate_up[l], ffn_w_down[l])
    return rms_norm(h_lat, final_norm)
```

```python
import functools

import numpy as np
import jax
import jax.numpy as jnp
from jax import lax
from jax.experimental import pallas as pl
from jax.experimental.pallas import tpu as pltpu

F32 = jnp.float32
BF16 = jnp.bfloat16

D_MODEL = 1024
GRID_W = 64
HEAD_DIM = 64
ROPE_BASE = 10000.0
EPS = 1e-6
NEG_INF = -1e30

MLA_HEADS = 8
MLA_NOPE = 64
MLA_ROPE = 32
MLA_V = 64
MLA_Q_RANK = 256
MLA_KV_RANK = 128
MLA_IN = MLA_Q_RANK + MLA_KV_RANK + MLA_ROPE

NA_HEADS = 8
NA_KR = 8
NA_KC = 16
NA_RB = 8
NA_SPAN = 16

SWA_HEADS = 16
SWA_KV_HEADS = 2
SWA_WINDOW = 128
SWA_BLOCK = 128

LANES = 128
MLA_QK_PAD = 256
ONES_LANE = MLA_ROPE
VMEM_LIMIT = 56 * 1024 * 1024


def _cparams(sem):
    return pltpu.CompilerParams(dimension_semantics=sem, vmem_limit_bytes=VMEM_LIMIT)


def _dot(a, b):
    return jnp.dot(a, b, preferred_element_type=F32)


def _dot_nt(a, b):
    return lax.dot_general(a, b, (((1,), (1,)), ((), ())), preferred_element_type=F32)


def _rms(x):
    return x * lax.rsqrt(jnp.mean(x * x, axis=-1, keepdims=True) + EPS)


def _silu(x):
    return x * (1.0 / (1.0 + jnp.exp(-x)))


def _rope_tile(x, cos, sin, quarter):
    lane = lax.broadcasted_iota(jnp.int32, x.shape, 1)
    first = (lane % (2 * quarter)) < quarter
    swapped = jnp.where(first, pltpu.roll(x, LANES - quarter, 1), pltpu.roll(x, quarter, 1))
    return x * cos + swapped * sin


def _rowmax(s):
    n = s.shape[1] // LANES
    mx = s[:, :LANES]
    for j in range(1, n):
        mx = jnp.maximum(mx, s[:, j * LANES:(j + 1) * LANES])
    return jnp.max(mx, axis=1, keepdims=True)


def _rowsum(p):
    n = p.shape[1] // LANES
    sm = p[:, :LANES]
    for j in range(1, n):
        sm = sm + p[:, j * LANES:(j + 1) * LANES]
    return jnp.sum(sm, axis=1, keepdims=True)


def _mod_kernel(c_ref, w_ref, b_ref, o_ref):
    a = _silu(c_ref[...])
    o_ref[0] = jnp.dot(a, w_ref[0], precision=lax.Precision.HIGHEST, preferred_element_type=F32) + b_ref[0]


def _modulation(c8, mod_w, mod_b):
    depth, d, n = mod_w.shape
    tn = 1536
    return pl.pallas_call(
        _mod_kernel,
        out_shape=jax.ShapeDtypeStruct((depth, 8, n), F32),
        grid=(depth, n // tn),
        in_specs=[pl.BlockSpec((8, d), lambda l, j: (0, 0)),
                  pl.BlockSpec((1, d, tn), lambda l, j: (l, 0, j)),
                  pl.BlockSpec((1, 1, tn), lambda l, j: (l, 0, j))],
        out_specs=pl.BlockSpec((1, 8, tn), lambda l, j: (l, 0, j)),
        compiler_params=_cparams(("parallel", "parallel")),
        name="modulation",
    )(c8, mod_w, mod_b.reshape(depth, 1, n))


def _inproj0_kernel(x_ref, mod_ref, win_ref, qn_ref, wq_ref, wuk_ref, kvn_ref, cos_ref, sin_ref,
                    qmla_ref, kmla_ref, naq_ref, nak_ref, nav_ref):
    d = D_MODEL
    x = x_ref[0]
    shift = mod_ref[0, :, 0:d]
    scale = mod_ref[0, :, d:2 * d]
    a = (_rms(x) * (1.0 + scale) + shift).astype(BF16)
    p = _dot(a, win_ref[...])
    cos = cos_ref[...]
    sin = sin_ref[...]
    mla_scale = (MLA_NOPE + MLA_ROPE) ** -0.5

    cq = (_rms(p[:, 0:MLA_Q_RANK]) * qn_ref[...]).astype(BF16)
    q = _dot(cq, wq_ref[...])
    n_nope = MLA_HEADS * MLA_NOPE
    q_lat = _dot(q[:, 0:n_nope].astype(BF16), wuk_ref[...])
    for h in range(MLA_HEADS):
        qr = _rope_tile(q[:, n_nope + h * LANES:n_nope + (h + 1) * LANES], cos, sin, MLA_ROPE // 4)
        qmla_ref[0, h, :, 0:LANES] = (q_lat[:, h * LANES:(h + 1) * LANES] * mla_scale).astype(BF16)
        qmla_ref[0, h, :, LANES:2 * LANES] = (qr * mla_scale).astype(BF16)

    ckv = _rms(p[:, MLA_Q_RANK:MLA_Q_RANK + MLA_KV_RANK]) * kvn_ref[...]
    kr = _rope_tile(p[:, 3 * LANES:4 * LANES], cos, sin, MLA_ROPE // 4)
    lane = lax.broadcasted_iota(jnp.int32, kr.shape, 1)
    kr = jnp.where(lane == ONES_LANE, 1.0, kr)
    kmla_ref[0, :, 0:LANES] = ckv.astype(BF16)
    kmla_ref[0, :, LANES:2 * LANES] = kr.astype(BF16)

    w = NA_HEADS * HEAD_DIM
    naq_ref[0] = (p[:, 4 * LANES:4 * LANES + w] * (HEAD_DIM ** -0.5)).astype(BF16)
    nak_ref[0] = p[:, 4 * LANES + w:4 * LANES + 2 * w].astype(BF16)
    nav_ref[0] = p[:, 4 * LANES + 2 * w:4 * LANES + 3 * w].astype(BF16)


def _inproj0(x, mods, grp, win, qn, wq, wuk, kvn, cos, sin, tm):
    b, s, d = x.shape
    nt = s // tm
    w = NA_HEADS * HEAD_DIM
    const = lambda bi, i: (0, 0)
    return pl.pallas_call(
        _inproj0_kernel,
        out_shape=(jax.ShapeDtypeStruct((b, MLA_HEADS, s, MLA_QK_PAD), BF16),
                   jax.ShapeDtypeStruct((b, s, MLA_QK_PAD), BF16),
                   jax.ShapeDtypeStruct((b, s, w), BF16),
                   jax.ShapeDtypeStruct((b, s, w), BF16),
                   jax.ShapeDtypeStruct((b, s, w), BF16)),
        grid=(b, nt),
        in_specs=[pl.BlockSpec((1, tm, d), lambda bi, i: (bi, i, 0)),
                  pl.BlockSpec((1, 1, mods.shape[2]), lambda bi, i: (grp(bi), 0, 0)),
                  pl.BlockSpec(win.shape, const),
                  pl.BlockSpec(qn.shape, const),
                  pl.BlockSpec(wq.shape, const),
                  pl.BlockSpec(wuk.shape, const),
                  pl.BlockSpec(kvn.shape, const),
                  pl.BlockSpec((tm, LANES), lambda bi, i: (i, 0)),
                  pl.BlockSpec((tm, LANES), lambda bi, i: (i, 0))],
        out_specs=(pl.BlockSpec((1, MLA_HEADS, tm, MLA_QK_PAD), lambda bi, i: (bi, 0, i, 0)),
                   pl.BlockSpec((1, tm, MLA_QK_PAD), lambda bi, i: (bi, i, 0)),
                   pl.BlockSpec((1, tm, w), lambda bi, i: (bi, i, 0)),
                   pl.BlockSpec((1, tm, w), lambda bi, i: (bi, i, 0)),
                   pl.BlockSpec((1, tm, w), lambda bi, i: (bi, i, 0))),
        compiler_params=_cparams(("parallel", "parallel")),
        name="inproj_even",
    )(x, mods, win, qn, wq, wuk, kvn, cos, sin)


def _mla_kernel(q_ref, kt_ref, k_ref, wuv_ref, o_ref, acc_ref, m_ref, *, n_k):
    ks = pl.program_id(2)

    @pl.when(ks == 0)
    def _():
        acc_ref[...] = jnp.zeros(acc_ref.shape, F32)
        m_ref[...] = jnp.full(m_ref.shape, -jnp.inf, F32)

    kt = kt_ref[0]
    kv = k_ref[0]
    tk = kv.shape[0]
    for h in range(MLA_HEADS):
        s = _dot(q_ref[0, h], kt)
        m_prev = m_ref[h]
        m_new = jnp.maximum(m_prev, _rowmax(s))
        p = jnp.concatenate([jnp.exp(s[:, j * LANES:(j + 1) * LANES] - m_new) for j in range(tk // LANES)],
                            axis=1).astype(BF16)
        alpha = jnp.exp(m_prev - m_new)
        pv = _dot(p, kv)
        acc_ref[h] = acc_ref[h] * jnp.concatenate([alpha, alpha], axis=1) + pv
        m_ref[h] = m_new

    @pl.when(ks == n_k - 1)
    def _():
        outs = []
        for h in range(MLA_HEADS):
            acc = acc_ref[h]
            tail = acc[:, LANES:2 * LANES]
            lane = lax.broadcasted_iota(jnp.int32, tail.shape, 1)
            l = jnp.sum(jnp.where(lane == ONES_LANE, tail, 0.0), axis=1, keepdims=True)
            outs.append((acc[:, 0:LANES] / l).astype(BF16))
        o = jnp.concatenate(outs, axis=1)
        o_ref[0] = _dot(o, wuv_ref[...]).astype(o_ref.dtype)


def _mla_attention(q, kt, k, wuv, tq, tk):
    b, h, sq, _ = q.shape
    nk = k.shape[1]
    n_k = nk // tk
    wo = wuv.shape[1]
    return pl.pallas_call(
        functools.partial(_mla_kernel, n_k=n_k),
        out_shape=jax.ShapeDtypeStruct((b, sq, wo), BF16),
        grid=(b, sq // tq, n_k),
        in_specs=[pl.BlockSpec((1, h, tq, MLA_QK_PAD), lambda bi, i, j: (bi, 0, i, 0)),
                  pl.BlockSpec((1, MLA_QK_PAD, tk), lambda bi, i, j: (bi, 0, j)),
                  pl.BlockSpec((1, tk, MLA_QK_PAD), lambda bi, i, j: (bi, j, 0)),
                  pl.BlockSpec(wuv.shape, lambda bi, i, j: (0, 0))],
        out_specs=pl.BlockSpec((1, tq, wo), lambda bi, i, j: (bi, i, 0)),
        scratch_shapes=[pltpu.VMEM((h, tq, MLA_QK_PAD), F32), pltpu.VMEM((h, tq, LANES), F32)],
        compiler_params=_cparams(("parallel", "parallel", "arbitrary")),
        name="mla_attention",
    )(q, kt, k, wuv)


def _split_heads(t):
    lane = lax.broadcasted_iota(jnp.int32, t.shape, 1)
    lo = lane < HEAD_DIM
    zero = jnp.zeros_like(t)
    return jnp.where(lo, t, zero), jnp.where(lo, zero, t)


def _na_kernel(q_ref, k_ref, v_ref, kc_ref, vc_ref, bias_ref, o_ref, *, n_rows):
    rb = pl.program_id(2)
    base = jnp.clip(rb * NA_RB - NA_KR // 2, 0, n_rows - NA_SPAN)
    start = pl.multiple_of(base * GRID_W, 256)
    n_keys = NA_SPAN * GRID_W
    q = q_ref[0]
    ks = _split_heads(k_ref[0, pl.ds(start, n_keys), :])
    vs = _split_heads(v_ref[0, pl.ds(start, n_keys), :])
    kcs = _split_heads(kc_ref[0])
    vcs = _split_heads(vc_ref[0])
    out = None
    for par in range(2):
        s_lat = _dot_nt(q, ks[par]) + bias_ref[0, par]
        s_ctx = _dot_nt(q, kcs[par])
        m = jnp.maximum(_rowmax(s_lat), _rowmax(s_ctx))
        p_lat = jnp.exp(s_lat - m)
        p_ctx = jnp.exp(s_ctx - m)
        l = _rowsum(p_lat) + _rowsum(p_ctx)
        o = (_dot(p_ctx.astype(BF16), vcs[par]) + _dot(p_lat.astype(BF16), vs[par])) / l
        out = o if out is None else out + o
    o_ref[0] = out.astype(o_ref.dtype)


def _na_attention(q, k, v, kc, vc, bias):
    b, s, w = q.shape
    n_rows = s // GRID_W
    nrb = n_rows // NA_RB
    tq = NA_RB * GRID_W
    n_ctx = kc.shape[1]
    case = lambda r: jnp.minimum(r, 1) + jnp.maximum(r - (nrb - 2), 0)
    return pl.pallas_call(
        functools.partial(_na_kernel, n_rows=n_rows),
        out_shape=jax.ShapeDtypeStruct((b, s, w), BF16),
        grid=(b, w // LANES, nrb),
        in_specs=[pl.BlockSpec((1, tq, LANES), lambda bi, j, r: (bi, r, j)),
                  pl.BlockSpec((1, s, LANES), lambda bi, j, r: (bi, 0, j)),
                  pl.BlockSpec((1, s, LANES), lambda bi, j, r: (bi, 0, j)),
                  pl.BlockSpec((1, n_ctx, LANES), lambda bi, j, r: (bi, 0, j)),
                  pl.BlockSpec((1, n_ctx, LANES), lambda bi, j, r: (bi, 0, j)),
                  pl.BlockSpec((1, 2, tq, NA_SPAN * GRID_W), lambda bi, j, r: (case(r), j, 0, 0))],
        out_specs=pl.BlockSpec((1, tq, LANES), lambda bi, j, r: (bi, r, j)),
        compiler_params=_cparams(("parallel", "parallel", "arbitrary")),
        name="na_attention",
    )(q, k, v, kc, vc, bias)


def _pair_ctx_kernel(q_ref, kc_ref, vc_ref, o_ref):
    q = q_ref[0]
    kcs = _split_heads(kc_ref[0])
    vcs = _split_heads(vc_ref[0])
    out = None
    for par in range(2):
        s = _dot_nt(q, kcs[par])
        p = jnp.exp(s - _rowmax(s))
        o = _dot(p.astype(BF16), vcs[par]) / _rowsum(p)
        out = o if out is None else out + o
    o_ref[0] = out.astype(o_ref.dtype)


def _pair_ctx_attention(q, kc, vc):
    b, n, w = q.shape
    spec = pl.BlockSpec((1, n, LANES), lambda bi, j: (bi, 0, j))
    return pl.pallas_call(
        _pair_ctx_kernel,
        out_shape=jax.ShapeDtypeStruct((b, n, w), BF16),
        grid=(b, w // LANES),
        in_specs=[spec, spec, spec],
        out_specs=spec,
        compiler_params=_cparams(("parallel", "parallel")),
        name="na_ctx_attention",
    )(q, kc, vc)


def _na_bias_table(rel_bias, n_rows):
    qc = np.arange(GRID_W)
    cstart = np.clip(qc - NA_KC // 2, 0, GRID_W - NA_KC)
    kc = np.arange(GRID_W)
    col_ok = (kc[None, :] >= cstart[:, None]) & (kc[None, :] < cstart[:, None] + NA_KC)
    n_dc = 2 * NA_KC - 1
    padw = GRID_W - NA_KC
    rbp = jnp.pad(rel_bias.astype(F32), ((0, 0), (0, 0), (padw, padw)))
    toe = jnp.stack([rbp[:, :, GRID_W - 1 - c:2 * GRID_W - 1 - c] for c in range(GRID_W)], axis=2)
    toe = jnp.where(jnp.asarray(col_ok)[None, None], toe, NEG_INF)
    masked = jnp.full((rel_bias.shape[0], 1, GRID_W, GRID_W), NEG_INF, F32)
    toe = jnp.concatenate([toe, masked], axis=1)
    n_dr = 2 * NA_KR - 1
    idx = np.full((3, NA_RB, NA_SPAN), n_dr, np.int32)
    nrb = n_rows // NA_RB
    for case, rb in enumerate((0, min(1, nrb - 1), nrb - 1)):
        r0 = rb * NA_RB
        base = int(np.clip(r0 - NA_KR // 2, 0, n_rows - NA_SPAN))
        for a in range(NA_RB):
            r = r0 + a
            rs = int(np.clip(r - NA_KR // 2, 0, n_rows - NA_KR))
            for t in range(NA_SPAN):
                kr = base + t
                if rs <= kr < rs + NA_KR:
                    idx[case, a, t] = kr - r + NA_KR - 1
    tab = jnp.take(toe, jnp.asarray(idx.reshape(-1)), axis=1)
    nh = rel_bias.shape[0]
    tab = tab.reshape(nh, 3, NA_RB, NA_SPAN, GRID_W, GRID_W).transpose(1, 0, 2, 4, 3, 5)
    del n_dc
    return tab.reshape(3, nh, NA_RB * GRID_W, NA_SPAN * GRID_W)


def _ffn_kernel(*refs, n_attn, n_ff, final):
    h_ref, mod_ref = refs[0], refs[1]
    attn_refs = refs[2:2 + n_attn]
    wout_refs = refs[2 + n_attn:2 + 2 * n_attn]
    wg_ref, wu_ref, wd_ref = refs[2 + 2 * n_attn:5 + 2 * n_attn]
    pos = 5 + 2 * n_attn
    fn_ref = refs[pos] if final else None
    pos += 1 if final else 0
    o_ref, h1_ref, a2_ref, acc_ref = refs[pos:pos + 4]
    d = D_MODEL
    j = pl.program_id(2)

    @pl.when(j == 0)
    def _():
        proj = _dot(attn_refs[0][0], wout_refs[0][...])
        for t in range(1, n_attn):
            proj = proj + _dot(attn_refs[t][0], wout_refs[t][...])
        h1 = h_ref[0] + mod_ref[0, :, 2 * d:3 * d] * proj
        h1_ref[...] = h1
        a2_ref[...] = (_rms(h1) * (1.0 + mod_ref[0, :, 4 * d:5 * d]) + mod_ref[0, :, 3 * d:4 * d]).astype(BF16)
        acc_ref[...] = jnp.zeros(acc_ref.shape, F32)

    a2 = a2_ref[...]
    g = _dot(a2, wg_ref[...])
    u = _dot(a2, wu_ref[...])
    acc_ref[...] += _dot((_silu(g) * u).astype(BF16), wd_ref[...])

    @pl.when(j == n_ff - 1)
    def _():
        out = h1_ref[...] + mod_ref[0, :, 5 * d:6 * d] * acc_ref[...]
        if final:
            out = _rms(out) * fn_ref[...]
        o_ref[0] = out


def _ffn_block(h, mods, grp, attns, wouts, wgu, wd, final_norm, tm, tf):
    b, s, d = h.shape
    dff = wd.shape[0]
    n_ff = dff // tf
    n_attn = len(attns)
    final = final_norm is not None
    in_specs = [pl.BlockSpec((1, tm, d), lambda bi, i, j: (bi, i, 0)),
                pl.BlockSpec((1, 1, mods.shape[2]), lambda bi, i, j: (grp(bi), 0, 0))]
    in_specs += [pl.BlockSpec((1, tm, a.shape[2]), lambda bi, i, j: (bi, i, 0)) for a in attns]
    in_specs += [pl.BlockSpec(w.shape, lambda bi, i, j: (0, 0)) for w in wouts]
    in_specs += [pl.BlockSpec((d, tf), lambda bi, i, j: (0, j)),
                 pl.BlockSpec((d, tf), lambda bi, i, j: (0, n_ff + j)),
                 pl.BlockSpec((tf, d), lambda bi, i, j: (j, 0))]
    args = [h, mods, *attns, *wouts, wgu, wgu, wd]
    if final:
        in_specs.append(pl.BlockSpec((1, d), lambda bi, i, j: (0, 0)))
        args.append(final_norm)
    return pl.pallas_call(
        functools.partial(_ffn_kernel, n_attn=n_attn, n_ff=n_ff, final=final),
        out_shape=jax.ShapeDtypeStruct((b, s, d), F32),
        grid=(b, s // tm, n_ff),
        in_specs=in_specs,
        out_specs=pl.BlockSpec((1, tm, d), lambda bi, i, j: (bi, i, 0)),
        scratch_shapes=[pltpu.VMEM((tm, d), F32), pltpu.VMEM((tm, d), BF16), pltpu.VMEM((tm, d), F32)],
        compiler_params=_cparams(("parallel", "parallel", "arbitrary")),
        name="outproj_ffn",
    )(*args)


def _inproj1_kernel(x_ref, mod_ref, win_ref, cos_ref, sin_ref, q_ref, k_ref, v_ref):
    d = D_MODEL
    x = x_ref[0]
    a = (_rms(x) * (1.0 + mod_ref[0, :, d:2 * d]) + mod_ref[0, :, 0:d]).astype(BF16)
    p = _dot(a, win_ref[...])
    cos = cos_ref[...]
    sin = sin_ref[...]
    nq = SWA_HEADS * HEAD_DIM
    nkv = 2 * SWA_KV_HEADS * LANES
    quarter = HEAD_DIM // 4
    for t in range(nq // LANES):
        q = _rope_tile(p[:, t * LANES:(t + 1) * LANES], cos, sin, quarter)
        q_ref[0, :, t * LANES:(t + 1) * LANES] = (q * (HEAD_DIM ** -0.5)).astype(BF16)
    for t in range(nkv // LANES):
        k = _rope_tile(p[:, nq + t * LANES:nq + (t + 1) * LANES], cos, sin, quarter)
        k_ref[0, :, t * LANES:(t + 1) * LANES] = k.astype(BF16)
    v_ref[0] = p[:, nq + nkv:nq + 2 * nkv].astype(BF16)


def _inproj1(x, mods, grp, win, cos, sin, tm):
    b, s, d = x.shape
    nq = SWA_HEADS * HEAD_DIM
    nkv = 2 * SWA_KV_HEADS * LANES
    return pl.pallas_call(
        _inproj1_kernel,
        out_shape=(jax.ShapeDtypeStruct((b, s, nq), BF16),
                   jax.ShapeDtypeStruct((b, s, nkv), BF16),
                   jax.ShapeDtypeStruct((b, s, nkv), BF16)),
        grid=(b, s // tm),
        in_specs=[pl.BlockSpec((1, tm, d), lambda bi, i: (bi, i, 0)),
                  pl.BlockSpec((1, 1, mods.shape[2]), lambda bi, i: (grp(bi), 0, 0)),
                  pl.BlockSpec(win.shape, lambda bi, i: (0, 0)),
                  pl.BlockSpec((tm, LANES), lambda bi, i: (i, 0)),
                  pl.BlockSpec((tm, LANES), lambda bi, i: (i, 0))],
        out_specs=(pl.BlockSpec((1, tm, nq), lambda bi, i: (bi, i, 0)),
                   pl.BlockSpec((1, tm, nkv), lambda bi, i: (bi, i, 0)),
                   pl.BlockSpec((1, tm, nkv), lambda bi, i: (bi, i, 0))),
        compiler_params=_cparams(("parallel", "parallel")),
        name="inproj_odd",
    )(x, mods, win, cos, sin)


def _swa_kernel(q_ref, km_ref, k0_ref, kp_ref, vm_ref, v0_ref, vp_ref, kc_ref, vc_ref, sink_ref, o_ref, *, nb):
    j = pl.program_id(2)
    blk = SWA_BLOCK
    npair = q_ref.shape[2] // LANES
    qb = q_ref[0]
    qst = jnp.concatenate([qb[:, t * LANES:(t + 1) * LANES] for t in range(npair)], axis=0)
    k3 = jnp.concatenate([km_ref[0], k0_ref[0], kp_ref[0]], axis=0)
    v3 = jnp.concatenate([vm_ref[0], v0_ref[0], vp_ref[0]], axis=0)
    kc = kc_ref[0]
    vc = vc_ref[0]
    r = lax.broadcasted_iota(jnp.int32, (blk, 3 * blk), 0)
    c = lax.broadcasted_iota(jnp.int32, (blk, 3 * blk), 1)
    cc = c % blk
    left_off = jnp.where(j >= 1, 0, blk)
    right_off = jnp.where(j <= nb - 2, 0, blk)
    ok = ((c >= blk) & (c < 2 * blk)) | ((c < blk) & (cc >= r + left_off)) | ((c >= 2 * blk) & (cc <= r - right_off))
    neg = jnp.where(ok, 0.0, NEG_INF).astype(F32)
    neg = jnp.concatenate([neg] * npair, axis=0)
    out = None
    for par in range(2):
        sl = slice(par * LANES, (par + 1) * LANES)
        sink = sink_ref[0, par][:, 0:1]
        s_lat = _dot_nt(qst, k3[:, sl]) + neg
        s_ctx = _dot_nt(qst, kc[:, sl])
        m = jnp.maximum(jnp.maximum(_rowmax(s_lat), _rowmax(s_ctx)), sink)
        p_lat = jnp.exp(s_lat - m)
        p_ctx = jnp.exp(s_ctx - m)
        l = _rowsum(p_lat) + _rowsum(p_ctx) + jnp.exp(sink - m)
        o = (_dot(p_ctx.astype(BF16), vc[:, sl]) + _dot(p_lat.astype(BF16), v3[:, sl])) / l
        out = o if out is None else out + o
    o_ref[0] = jnp.concatenate([out[t * blk:(t + 1) * blk] for t in range(npair)], axis=1).astype(o_ref.dtype)


def _swa_attention(q, kvar, vvar, kc, vc, sink_tab):
    b, s, nq = q.shape
    blk = SWA_BLOCK
    nb = s // blk
    gw = nq // SWA_KV_HEADS
    n_ctx = kc.shape[1]
    prev = lambda bi, g, j: (bi, jnp.maximum(j - 1, 0), g)
    cur = lambda bi, g, j: (bi, j, g)
    nxt = lambda bi, g, j: (bi, jnp.minimum(j + 1, nb - 1), g)
    kv_spec = lambda f: pl.BlockSpec((1, blk, 2 * LANES), f)
    ctx_spec = pl.BlockSpec((1, n_ctx, 2 * LANES), lambda bi, g, j: (bi, 0, g))
    return pl.pallas_call(
        functools.partial(_swa_kernel, nb=nb),
        out_shape=jax.ShapeDtypeStruct((b, s, nq), BF16),
        grid=(b, SWA_KV_HEADS, nb),
        in_specs=[pl.BlockSpec((1, blk, gw), cur),
                  kv_spec(prev), kv_spec(cur), kv_spec(nxt),
                  kv_spec(prev), kv_spec(cur), kv_spec(nxt),
                  ctx_spec, ctx_spec,
                  pl.BlockSpec((1, 2, sink_tab.shape[2], LANES), lambda bi, g, j: (g, 0, 0, 0))],
        out_specs=pl.BlockSpec((1, blk, gw), cur),
        compiler_params=_cparams(("parallel", "parallel", "arbitrary")),
        name="swa_attention",
    )(q, kvar, kvar, kvar, vvar, vvar, vvar, kc, vc, sink_tab)


def _rope_tables(s, dim, pad_to):
    t = jnp.arange(s, dtype=jnp.int32)
    rows, cols = t // GRID_W, t % GRID_W
    half = dim // 2
    inv = ROPE_BASE ** (-jnp.arange(0, half, 2, dtype=F32) / half)
    ar = rows.astype(F32)[:, None] * inv
    ac = cols.astype(F32)[:, None] * inv
    cos = jnp.concatenate([jnp.cos(ar), jnp.cos(ar), jnp.cos(ac), jnp.cos(ac)], axis=1)
    sin = jnp.concatenate([-jnp.sin(ar), jnp.sin(ar), -jnp.sin(ac), jnp.sin(ac)], axis=1)
    if pad_to > dim:
        cos = jnp.concatenate([cos, jnp.ones((s, pad_to - dim), F32)], axis=1)
        sin = jnp.concatenate([sin, jnp.zeros((s, pad_to - dim), F32)], axis=1)
    reps = LANES // cos.shape[1]
    return jnp.tile(cos, (1, reps)), jnp.tile(sin, (1, reps))


def _even_weights(w_in, w_q_up, w_uk, w_uv, w_out):
    d = w_in.shape[0]
    win = jnp.concatenate([w_in[:, :MLA_IN], jnp.zeros((d, 4 * LANES - MLA_IN), w_in.dtype), w_in[:, MLA_IN:]], axis=1)
    wq3 = w_q_up.reshape(MLA_Q_RANK, MLA_HEADS, MLA_NOPE + MLA_ROPE)
    nope = wq3[:, :, :MLA_NOPE].reshape(MLA_Q_RANK, MLA_HEADS * MLA_NOPE)
    rope = jnp.pad(wq3[:, :, MLA_NOPE:], ((0, 0), (0, 0), (0, LANES - MLA_ROPE))).reshape(MLA_Q_RANK, MLA_HEADS * LANES)
    wq = jnp.concatenate([nope, rope], axis=1)
    eye = jnp.eye(MLA_HEADS, dtype=w_uk.dtype)
    wuk = jnp.einsum('hcn,hg->hngc', w_uk, eye).reshape(MLA_HEADS * MLA_NOPE, MLA_HEADS * MLA_KV_RANK)
    wuv = jnp.einsum('hcv,hg->hcgv', w_uv, eye).reshape(MLA_HEADS * MLA_KV_RANK, MLA_HEADS * MLA_V)
    n_mla = MLA_HEADS * MLA_V
    return (win.astype(BF16), wq.astype(BF16), wuk.astype(BF16), wuv.astype(BF16),
            w_out[:n_mla].astype(BF16), w_out[n_mla:].astype(BF16))


def _odd_weights(w_in):
    d = w_in.shape[0]
    nq = SWA_HEADS * HEAD_DIM
    z = jnp.zeros((d, HEAD_DIM), w_in.dtype)

    def variants(off):
        cols = []
        for g in range(SWA_KV_HEADS):
            w = w_in[:, off + g * HEAD_DIM:off + (g + 1) * HEAD_DIM]
            cols += [w, z, z, w]
        return jnp.concatenate(cols, axis=1)

    kcols = variants(nq)
    vcols = variants(nq + SWA_KV_HEADS * HEAD_DIM)
    return jnp.concatenate([w_in[:, :nq], kcols, vcols], axis=1).astype(BF16)


def _sink_table(sinks):
    g = SWA_HEADS // SWA_KV_HEADS
    t = sinks.astype(F32).reshape(SWA_KV_HEADS, g // 2, 2).transpose(0, 2, 1)
    t = jnp.broadcast_to(t[:, :, :, None, None], (SWA_KV_HEADS, 2, g // 2, SWA_BLOCK, LANES))
    return t.reshape(SWA_KV_HEADS, 2, (g // 2) * SWA_BLOCK, LANES)


def kernel(x, c, ctx, c_ctx, mod_w, mod_b, even_w_in, mla_q_norm, mla_kv_norm, mla_w_q_up, mla_w_uk, mla_w_uv,
           na_rel_bias, even_w_out, odd_w_in, swa_sinks, odd_w_out, ffn_w_gate_up, ffn_w_down, final_norm):
    b, s, d = x.shape
    n_ctx = ctx.shape[1]
    n_rows = s // GRID_W
    assert d == D_MODEL and s % (NA_RB * GRID_W) == 0 and n_rows >= NA_SPAN and s % SWA_BLOCK == 0
    assert mod_w.shape[0] == 2 and b <= 4

    ctx_grp = 4
    c8 = jnp.zeros((8, d), F32).at[:b].set(c.astype(F32)).at[ctx_grp].set(c_ctx.astype(F32))
    mods = _modulation(c8, mod_w.astype(F32), mod_b.astype(F32))
    mods0 = mods[0].reshape(8, 1, 6 * d)
    mods1 = mods[1].reshape(8, 1, 6 * d)
    lat_grp = lambda bi: bi
    ctx_g = lambda bi: ctx_grp

    tm = min(512, s)
    tf = 256
    wgu = ffn_w_gate_up.astype(BF16)
    wdn = ffn_w_down.astype(BF16)

    win, wq, wuk, wuv, wo_mla, wo_na = _even_weights(even_w_in[0], mla_w_q_up[0], mla_w_uk[0], mla_w_uv[0],
                                                      even_w_out[0])
    qn = mla_q_norm[0].astype(F32).reshape(1, -1)
    kvn = mla_kv_norm[0].astype(F32).reshape(1, -1)
    cos_m, sin_m = _rope_tables(s, MLA_ROPE, LANES)
    one_c = jnp.ones((n_ctx, LANES), F32)
    zero_c = jnp.zeros((n_ctx, LANES), F32)

    q_l, k_l, naq_l, nak_l, nav_l = _inproj0(x, mods0, lat_grp, win, qn, wq, wuk, kvn, cos_m, sin_m, tm)
    q_c, k_c, naq_c, nak_c, nav_c = _inproj0(ctx, mods0, ctx_g, win, qn, wq, wuk, kvn, one_c, zero_c, n_ctx)

    k_all = jnp.concatenate([k_c, k_l], axis=1)
    kt_all = jnp.swapaxes(k_all, 1, 2)
    tk = 768 if (n_ctx + s) % 768 == 0 else 256
    o_mla_l = _mla_attention(q_l, kt_all, k_all, wuv, min(256, s), tk)
    o_mla_c = _mla_attention(q_c, jnp.swapaxes(k_c, 1, 2), k_c, wuv, n_ctx, n_ctx)

    bias = _na_bias_table(na_rel_bias[0], n_rows)
    o_na_l = _na_attention(naq_l, nak_l, nav_l, nak_c, nav_c, bias)
    o_na_c = _pair_ctx_attention(naq_c, nak_c, nav_c)

    h_lat = _ffn_block(x, mods0, lat_grp, [o_mla_l, o_na_l], [wo_mla, wo_na], wgu[0], wdn[0], None, tm, tf)
    h_ctx = _ffn_block(ctx, mods0, ctx_g, [o_mla_c, o_na_c], [wo_mla, wo_na], wgu[0], wdn[0], None, n_ctx, tf)

    win1 = _odd_weights(odd_w_in[0])
    cos_s, sin_s = _rope_tables(s, HEAD_DIM, HEAD_DIM)
    q1, k1, v1 = _inproj1(h_lat, mods1, lat_grp, win1, cos_s, sin_s, tm)
    _, k1c, v1c = _inproj1(h_ctx, mods1, ctx_g, win1, one_c, zero_c, n_ctx)
    o_swa = _swa_attention(q1, k1, v1, k1c, v1c, _sink_table(swa_sinks[0]))
    fn = final_norm.astype(F32).reshape(1, d)
    return _ffn_block(h_lat, mods1, lat_grp, [o_swa], [odd_w_out[0].astype(BF16)], wgu[1], wdn[1], fn, tm, tf)
```

```python
import functools

import numpy as np
import jax
import jax.numpy as jnp
from jax import lax
from jax.experimental import pallas as pl
from jax.experimental.pallas import tpu as pltpu

F32 = jnp.float32
BF16 = jnp.bfloat16

D_MODEL = 1024
GRID_W = 64
HEAD_DIM = 64
ROPE_BASE = 10000.0
EPS = 1e-6
NEG_INF = -1e30

MLA_HEADS = 8
MLA_NOPE = 64
MLA_ROPE = 32
MLA_V = 64
MLA_Q_RANK = 256
MLA_KV_RANK = 128
MLA_IN = MLA_Q_RANK + MLA_KV_RANK + MLA_ROPE

NA_HEADS = 8
NA_KR = 8
NA_KC = 16
NA_RB = 8
NA_SPAN = 16

SWA_HEADS = 16
SWA_KV_HEADS = 2
SWA_WINDOW = 128
SWA_BLOCK = 128
SWA_GROUP = 4

LANES = 128
MLA_QK_PAD = 256
ONES_LANE = MLA_ROPE
VMEM_LIMIT = 56 * 1024 * 1024


def _cparams(sem):
    return pltpu.CompilerParams(dimension_semantics=sem, vmem_limit_bytes=VMEM_LIMIT)


def _dot(a, b):
    return jnp.dot(a, b, preferred_element_type=F32)


def _dot_nt(a, b):
    return lax.dot_general(a, b, (((1,), (1,)), ((), ())), preferred_element_type=F32)


def _rms(x):
    return x * lax.rsqrt(jnp.mean(x * x, axis=-1, keepdims=True) + EPS)


def _silu(x):
    return x * (1.0 / (1.0 + jnp.exp(-x)))


def _rope_tile(x, cos, sin, quarter):
    lane = lax.broadcasted_iota(jnp.int32, x.shape, 1)
    first = (lane % (2 * quarter)) < quarter
    swapped = jnp.where(first, pltpu.roll(x, LANES - quarter, 1), pltpu.roll(x, quarter, 1))
    return x * cos + swapped * sin


def _tiles(s):
    return [s[:, j * LANES:(j + 1) * LANES] for j in range(s.shape[1] // LANES)]


def _rowmax128(tiles):
    mx = tiles[0]
    for t in tiles[1:]:
        mx = jnp.maximum(mx, t)
    return jnp.broadcast_to(jnp.max(mx, axis=1, keepdims=True), mx.shape)


def _rowmax(s):
    return _rowmax128(_tiles(s))


def _exp_tiles(tiles, m):
    return [jnp.exp(t - m).astype(BF16) for t in tiles]


def _pair_indicator(n, hi):
    lane = lax.broadcasted_iota(jnp.int32, (n, LANES), 1)
    return jnp.where((lane >= HEAD_DIM) == hi, 1.0, 0.0).astype(BF16)


def _mod_kernel(c_ref, w_ref, b_ref, o_ref):
    a = _silu(c_ref[...])
    o_ref[0] = jnp.dot(a, w_ref[0], precision=lax.Precision.HIGHEST, preferred_element_type=F32) + b_ref[0]


def _modulation(c8, mod_w, mod_b):
    depth, d, n = mod_w.shape
    tn = 1536
    return pl.pallas_call(
        _mod_kernel,
        out_shape=jax.ShapeDtypeStruct((depth, 8, n), F32),
        grid=(depth, n // tn),
        in_specs=[pl.BlockSpec((8, d), lambda l, j: (0, 0)),
                  pl.BlockSpec((1, d, tn), lambda l, j: (l, 0, j)),
                  pl.BlockSpec((1, 1, tn), lambda l, j: (l, 0, j))],
        out_specs=pl.BlockSpec((1, 8, tn), lambda l, j: (l, 0, j)),
        compiler_params=_cparams(("parallel", "parallel")),
        name="modulation",
    )(c8, mod_w, mod_b.reshape(depth, 1, n))


def _inproj0_kernel(x_ref, mod_ref, win_ref, qn_ref, wq_ref, wuk_ref, kvn_ref, cos_ref, sin_ref,
                    qmla_ref, kmla_ref, naq_ref, nak_ref, nav_ref):
    d = D_MODEL
    x = x_ref[0]
    shift = mod_ref[0, :, 0:d]
    scale = mod_ref[0, :, d:2 * d]
    a = (_rms(x) * (1.0 + scale) + shift).astype(BF16)
    p = _dot(a, win_ref[...])
    cos = cos_ref[...]
    sin = sin_ref[...]
    mla_scale = (MLA_NOPE + MLA_ROPE) ** -0.5

    cq = (_rms(p[:, 0:MLA_Q_RANK]) * qn_ref[...]).astype(BF16)
    q = _dot(cq, wq_ref[...])
    n_nope = MLA_HEADS * MLA_NOPE
    q_lat = _dot(q[:, 0:n_nope].astype(BF16), wuk_ref[...])
    for h in range(MLA_HEADS):
        qr = _rope_tile(q[:, n_nope + h * LANES:n_nope + (h + 1) * LANES], cos, sin, MLA_ROPE // 4)
        qmla_ref[0, h, :, 0:LANES] = (q_lat[:, h * LANES:(h + 1) * LANES] * mla_scale).astype(BF16)
        qmla_ref[0, h, :, LANES:2 * LANES] = (qr * mla_scale).astype(BF16)

    ckv = _rms(p[:, MLA_Q_RANK:MLA_Q_RANK + MLA_KV_RANK]) * kvn_ref[...]
    kr = _rope_tile(p[:, 3 * LANES:4 * LANES], cos, sin, MLA_ROPE // 4)
    lane = lax.broadcasted_iota(jnp.int32, kr.shape, 1)
    kr = jnp.where(lane == ONES_LANE, 1.0, kr)
    kmla_ref[0, :, 0:LANES] = ckv.astype(BF16)
    kmla_ref[0, :, LANES:2 * LANES] = kr.astype(BF16)

    w = NA_HEADS * HEAD_DIM
    naq_ref[0] = (p[:, 4 * LANES:4 * LANES + w] * (HEAD_DIM ** -0.5)).astype(BF16)
    nak_ref[0] = p[:, 4 * LANES + w:4 * LANES + 2 * w].astype(BF16)
    nav_ref[0] = p[:, 4 * LANES + 2 * w:4 * LANES + 3 * w].astype(BF16)


def _inproj0(x, mods, grp, win, qn, wq, wuk, kvn, cos, sin, tm):
    b, s, d = x.shape
    nt = s // tm
    w = NA_HEADS * HEAD_DIM
    const = lambda bi, i: (0, 0)
    return pl.pallas_call(
        _inproj0_kernel,
        out_shape=(jax.ShapeDtypeStruct((b, MLA_HEADS, s, MLA_QK_PAD), BF16),
                   jax.ShapeDtypeStruct((b, s, MLA_QK_PAD), BF16),
                   jax.ShapeDtypeStruct((b, s, w), BF16),
                   jax.ShapeDtypeStruct((b, s, w), BF16),
                   jax.ShapeDtypeStruct((b, s, w), BF16)),
        grid=(b, nt),
        in_specs=[pl.BlockSpec((1, tm, d), lambda bi, i: (bi, i, 0)),
                  pl.BlockSpec((1, 1, mods.shape[2]), lambda bi, i: (grp(bi), 0, 0)),
                  pl.BlockSpec(win.shape, const),
                  pl.BlockSpec(qn.shape, const),
                  pl.BlockSpec(wq.shape, const),
                  pl.BlockSpec(wuk.shape, const),
                  pl.BlockSpec(kvn.shape, const),
                  pl.BlockSpec((tm, LANES), lambda bi, i: (i, 0)),
                  pl.BlockSpec((tm, LANES), lambda bi, i: (i, 0))],
        out_specs=(pl.BlockSpec((1, MLA_HEADS, tm, MLA_QK_PAD), lambda bi, i: (bi, 0, i, 0)),
                   pl.BlockSpec((1, tm, MLA_QK_PAD), lambda bi, i: (bi, i, 0)),
                   pl.BlockSpec((1, tm, w), lambda bi, i: (bi, i, 0)),
                   pl.BlockSpec((1, tm, w), lambda bi, i: (bi, i, 0)),
                   pl.BlockSpec((1, tm, w), lambda bi, i: (bi, i, 0))),
        compiler_params=_cparams(("parallel", "parallel")),
        name="inproj_even",
    )(x, mods, win, qn, wq, wuk, kvn, cos, sin)


def _mla_kernel(q_ref, kt_ref, k_ref, wuv_ref, o_ref, acc_ref, m_ref, *, n_k):
    ks = pl.program_id(2)

    @pl.when(ks == 0)
    def _():
        acc_ref[...] = jnp.zeros(acc_ref.shape, F32)
        m_ref[...] = jnp.full(m_ref.shape, -jnp.inf, F32)

    kt = kt_ref[0]
    kv = k_ref[0]
    tk = kv.shape[0]
    for h in range(MLA_HEADS):
        s = _dot(q_ref[0, h], kt)
        m_prev = m_ref[h]
        m_new = jnp.maximum(m_prev, _rowmax(s))
        p = jnp.concatenate([jnp.exp(s[:, j * LANES:(j + 1) * LANES] - m_new) for j in range(tk // LANES)],
                            axis=1).astype(BF16)
        alpha = jnp.exp(m_prev - m_new)
        pv = _dot(p, kv)
        acc_ref[h] = acc_ref[h] * jnp.concatenate([alpha, alpha], axis=1) + pv
        m_ref[h] = m_new

    @pl.when(ks == n_k - 1)
    def _():
        outs = []
        for h in range(MLA_HEADS):
            acc = acc_ref[h]
            tail = acc[:, LANES:2 * LANES]
            lane = lax.broadcasted_iota(jnp.int32, tail.shape, 1)
            l = jnp.sum(jnp.where(lane == ONES_LANE, tail, 0.0), axis=1, keepdims=True)
            outs.append((acc[:, 0:LANES] / l).astype(BF16))
        o = jnp.concatenate(outs, axis=1)
        o_ref[0] = _dot(o, wuv_ref[...]).astype(o_ref.dtype)


def _mla_attention(q, kt, k, wuv, tq, tk):
    b, h, sq, _ = q.shape
    nk = k.shape[1]
    n_k = nk // tk
    wo = wuv.shape[1]
    return pl.pallas_call(
        functools.partial(_mla_kernel, n_k=n_k),
        out_shape=jax.ShapeDtypeStruct((b, sq, wo), BF16),
        grid=(b, sq // tq, n_k),
        in_specs=[pl.BlockSpec((1, h, tq, MLA_QK_PAD), lambda bi, i, j: (bi, 0, i, 0)),
                  pl.BlockSpec((1, MLA_QK_PAD, tk), lambda bi, i, j: (bi, 0, j)),
                  pl.BlockSpec((1, tk, MLA_QK_PAD), lambda bi, i, j: (bi, j, 0)),
                  pl.BlockSpec(wuv.shape, lambda bi, i, j: (0, 0))],
        out_specs=pl.BlockSpec((1, tq, wo), lambda bi, i, j: (bi, i, 0)),
        scratch_shapes=[pltpu.VMEM((h, tq, MLA_QK_PAD), F32), pltpu.VMEM((h, tq, LANES), F32)],
        compiler_params=_cparams(("parallel", "parallel", "arbitrary")),
        name="mla_attention",
    )(q, kt, k, wuv)


def _split_heads(t):
    lane = lax.broadcasted_iota(jnp.int32, t.shape, 1)
    lo = lane < HEAD_DIM
    zero = jnp.zeros_like(t)
    return jnp.where(lo, t, zero), jnp.where(lo, zero, t)


def _stack_pair(lo_hi_ctx, lo_hi_lat):
    return jnp.concatenate([lo_hi_ctx[0], lo_hi_ctx[1], lo_hi_lat[0], lo_hi_lat[1]], axis=0)


def _na_kernel(q_ref, k_ref, v_ref, kc_ref, vc_ref, bias_ref, o_ref, *, n_rows):
    rb = pl.program_id(2)
    base = jnp.clip(rb * NA_RB - NA_KR // 2, 0, n_rows - NA_SPAN)
    start = pl.multiple_of(base * GRID_W, 256)
    n_keys = NA_SPAN * GRID_W
    n_ctx = kc_ref.shape[1]
    q = q_ref[0]
    k_st = _stack_pair(_split_heads(kc_ref[0]), _split_heads(k_ref[0, pl.ds(start, n_keys), :]))
    v_st = _stack_pair(_split_heads(vc_ref[0]), _split_heads(v_ref[0, pl.ds(start, n_keys), :]))
    ind = _stack_pair((_pair_indicator(n_ctx, False), _pair_indicator(n_ctx, True)),
                      (_pair_indicator(n_keys, False), _pair_indicator(n_keys, True)))
    s = _dot_nt(q, k_st)
    p_ctx, p_lat = [], []
    for par in range(2):
        t_c = _tiles(s[:, par * n_ctx:(par + 1) * n_ctx])
        off = 2 * n_ctx + par * n_keys
        t_l = _tiles(s[:, off:off + n_keys] + bias_ref[0, par])
        m = _rowmax128(t_c + t_l)
        p_ctx += _exp_tiles(t_c, m)
        p_lat += _exp_tiles(t_l, m)
    o = _dot(jnp.concatenate(p_ctx + p_lat, axis=1), jnp.concatenate([v_st, ind], axis=1))
    o_ref[0] = (o[:, :LANES] / o[:, LANES:]).astype(o_ref.dtype)


def _na_attention(q, k, v, kc, vc, bias):
    b, s, w = q.shape
    n_rows = s // GRID_W
    nrb = n_rows // NA_RB
    tq = NA_RB * GRID_W
    n_ctx = kc.shape[1]
    case = lambda r: jnp.minimum(r, 1) + jnp.maximum(r - (nrb - 2), 0)
    return pl.pallas_call(
        functools.partial(_na_kernel, n_rows=n_rows),
        out_shape=jax.ShapeDtypeStruct((b, s, w), BF16),
        grid=(b, w // LANES, nrb),
        in_specs=[pl.BlockSpec((1, tq, LANES), lambda bi, j, r: (bi, r, j)),
                  pl.BlockSpec((1, s, LANES), lambda bi, j, r: (bi, 0, j)),
                  pl.BlockSpec((1, s, LANES), lambda bi, j, r: (bi, 0, j)),
                  pl.BlockSpec((1, n_ctx, LANES), lambda bi, j, r: (bi, 0, j)),
                  pl.BlockSpec((1, n_ctx, LANES), lambda bi, j, r: (bi, 0, j)),
                  pl.BlockSpec((1, 2, tq, NA_SPAN * GRID_W), lambda bi, j, r: (case(r), j, 0, 0))],
        out_specs=pl.BlockSpec((1, tq, LANES), lambda bi, j, r: (bi, r, j)),
        compiler_params=_cparams(("parallel", "parallel", "arbitrary")),
        name="na_attention",
    )(q, k, v, kc, vc, bias)


def _pair_ctx_kernel(q_ref, kc_ref, vc_ref, o_ref):
    q = q_ref[0]
    n_ctx = kc_ref.shape[1]
    kcs = _split_heads(kc_ref[0])
    vcs = _split_heads(vc_ref[0])
    s = _dot_nt(q, jnp.concatenate(kcs, axis=0))
    p = []
    for par in range(2):
        t_c = _tiles(s[:, par * n_ctx:(par + 1) * n_ctx])
        p += _exp_tiles(t_c, _rowmax128(t_c))
    ind = jnp.concatenate([_pair_indicator(n_ctx, False), _pair_indicator(n_ctx, True)], axis=0)
    o = _dot(jnp.concatenate(p, axis=1), jnp.concatenate([jnp.concatenate(vcs, axis=0), ind], axis=1))
    o_ref[0] = (o[:, :LANES] / o[:, LANES:]).astype(o_ref.dtype)


def _pair_ctx_attention(q, kc, vc):
    b, n, w = q.shape
    spec = pl.BlockSpec((1, n, LANES), lambda bi, j: (bi, 0, j))
    return pl.pallas_call(
        _pair_ctx_kernel,
        out_shape=jax.ShapeDtypeStruct((b, n, w), BF16),
        grid=(b, w // LANES),
        in_specs=[spec, spec, spec],
        out_specs=spec,
        compiler_params=_cparams(("parallel", "parallel")),
        name="na_ctx_attention",
    )(q, kc, vc)


def _na_bias_table(rel_bias, n_rows):
    qc = np.arange(GRID_W)
    cstart = np.clip(qc - NA_KC // 2, 0, GRID_W - NA_KC)
    kc = np.arange(GRID_W)
    col_ok = (kc[None, :] >= cstart[:, None]) & (kc[None, :] < cstart[:, None] + NA_KC)
    padw = GRID_W - NA_KC
    rbp = jnp.pad(rel_bias.astype(F32), ((0, 0), (0, 0), (padw, padw)))
    toe = jnp.stack([rbp[:, :, GRID_W - 1 - c:2 * GRID_W - 1 - c] for c in range(GRID_W)], axis=2)
    toe = jnp.where(jnp.asarray(col_ok)[None, None], toe, NEG_INF)
    masked = jnp.full((rel_bias.shape[0], 1, GRID_W, GRID_W), NEG_INF, F32)
    toe = jnp.concatenate([toe, masked], axis=1)
    n_dr = 2 * NA_KR - 1
    idx = np.full((3, NA_RB, NA_SPAN), n_dr, np.int32)
    nrb = n_rows // NA_RB
    for case, rb in enumerate((0, min(1, nrb - 1), nrb - 1)):
        r0 = rb * NA_RB
        base = int(np.clip(r0 - NA_KR // 2, 0, n_rows - NA_SPAN))
        for a in range(NA_RB):
            r = r0 + a
            rs = int(np.clip(r - NA_KR // 2, 0, n_rows - NA_KR))
            for t in range(NA_SPAN):
                kr = base + t
                if rs <= kr < rs + NA_KR:
                    idx[case, a, t] = kr - r + NA_KR - 1
    nh = rel_bias.shape[0]
    cases = []
    for case in range(3):
        rows = [jnp.concatenate([toe[:, int(idx[case, a, t])] for t in range(NA_SPAN)], axis=-1)
                for a in range(NA_RB)]
        cases.append(jnp.stack(rows, axis=1).reshape(nh, NA_RB * GRID_W, NA_SPAN * GRID_W))
    return jnp.stack(cases, axis=0)


def _ffn_kernel(*refs, n_attn, tf, final):
    h_ref, mod_ref = refs[0], refs[1]
    attn_refs = refs[2:2 + n_attn]
    wout_refs = refs[2 + n_attn:2 + 2 * n_attn]
    wgu_ref, wd_ref = refs[2 + 2 * n_attn:4 + 2 * n_attn]
    pos = 4 + 2 * n_attn
    fn_ref = refs[pos] if final else None
    pos += 1 if final else 0
    o_ref, a2_ref, f_ref = refs[pos:pos + 3]
    d = D_MODEL
    dff = wd_ref.shape[0]

    proj = _dot(attn_refs[0][0], wout_refs[0][...])
    for t in range(1, n_attn):
        proj = proj + _dot(attn_refs[t][0], wout_refs[t][...])
    h1 = h_ref[0] + mod_ref[0, :, 2 * d:3 * d] * proj
    o_ref[0] = h1
    a2_ref[...] = (_rms(h1) * (1.0 + mod_ref[0, :, 4 * d:5 * d]) + mod_ref[0, :, 3 * d:4 * d]).astype(BF16)

    for j in range(dff // tf):
        a2 = a2_ref[...]
        g = _dot(a2, wgu_ref[:, j * tf:(j + 1) * tf])
        u = _dot(a2, wgu_ref[:, dff + j * tf:dff + (j + 1) * tf])
        f_ref[:, j * tf:(j + 1) * tf] = (_silu(g) * u).astype(BF16)

    out = o_ref[0] + mod_ref[0, :, 5 * d:6 * d] * _dot(f_ref[...], wd_ref[...])
    if final:
        out = _rms(out) * fn_ref[...]
    o_ref[0] = out


def _resident(shape):
    return pl.BlockSpec(shape, lambda *_: (0,) * len(shape), pipeline_mode=pl.Buffered(1))


def _ffn_block(h, mods, grp, attns, wouts, wgu, wd, final_norm, tm, tf):
    b, s, d = h.shape
    dff = wd.shape[0]
    n_attn = len(attns)
    final = final_norm is not None
    in_specs = [pl.BlockSpec((1, tm, d), lambda bi, i: (bi, i, 0)),
                pl.BlockSpec((1, 1, mods.shape[2]), lambda bi, i: (grp(bi), 0, 0))]
    in_specs += [pl.BlockSpec((1, tm, a.shape[2]), lambda bi, i: (bi, i, 0)) for a in attns]
    in_specs += [_resident(w.shape) for w in wouts]
    in_specs += [_resident(wgu.shape), _resident(wd.shape)]
    args = [h, mods, *attns, *wouts, wgu, wd]
    if final:
        in_specs.append(_resident((1, d)))
        args.append(final_norm)
    return pl.pallas_call(
        functools.partial(_ffn_kernel, n_attn=n_attn, tf=tf, final=final),
        out_shape=jax.ShapeDtypeStruct((b, s, d), F32),
        grid=(b, s // tm),
        in_specs=in_specs,
        out_specs=pl.BlockSpec((1, tm, d), lambda bi, i: (bi, i, 0)),
        scratch_shapes=[pltpu.VMEM((tm, d), BF16), pltpu.VMEM((tm, dff), BF16)],
        compiler_params=_cparams(("parallel", "parallel")),
        name="outproj_ffn",
    )(*args)


def _inproj1_kernel(x_ref, mod_ref, win_ref, cos_ref, sin_ref, q_ref, k_ref, v_ref):
    d = D_MODEL
    x = x_ref[0]
    a = (_rms(x) * (1.0 + mod_ref[0, :, d:2 * d]) + mod_ref[0, :, 0:d]).astype(BF16)
    p = _dot(a, win_ref[...])
    cos = cos_ref[...]
    sin = sin_ref[...]
    nq = SWA_HEADS * HEAD_DIM
    nkv = 2 * SWA_KV_HEADS * LANES
    quarter = HEAD_DIM // 4
    for t in range(nq // LANES):
        q = _rope_tile(p[:, t * LANES:(t + 1) * LANES], cos, sin, quarter)
        q_ref[0, :, t * LANES:(t + 1) * LANES] = (q * (HEAD_DIM ** -0.5)).astype(BF16)
    for t in range(nkv // LANES):
        k = _rope_tile(p[:, nq + t * LANES:nq + (t + 1) * LANES], cos, sin, quarter)
        k_ref[0, :, t * LANES:(t + 1) * LANES] = k.astype(BF16)
    v_ref[0] = p[:, nq + nkv:nq + 2 * nkv].astype(BF16)


def _inproj1(x, mods, grp, win, cos, sin, tm):
    b, s, d = x.shape
    nq = SWA_HEADS * HEAD_DIM
    nkv = 2 * SWA_KV_HEADS * LANES
    return pl.pallas_call(
        _inproj1_kernel,
        out_shape=(jax.ShapeDtypeStruct((b, s, nq), BF16),
                   jax.ShapeDtypeStruct((b, s, nkv), BF16),
                   jax.ShapeDtypeStruct((b, s, nkv), BF16)),
        grid=(b, s // tm),
        in_specs=[pl.BlockSpec((1, tm, d), lambda bi, i: (bi, i, 0)),
                  pl.BlockSpec((1, 1, mods.shape[2]), lambda bi, i: (grp(bi), 0, 0)),
                  pl.BlockSpec(win.shape, lambda bi, i: (0, 0)),
                  pl.BlockSpec((tm, LANES), lambda bi, i: (i, 0)),
                  pl.BlockSpec((tm, LANES), lambda bi, i: (i, 0))],
        out_specs=(pl.BlockSpec((1, tm, nq), lambda bi, i: (bi, i, 0)),
                   pl.BlockSpec((1, tm, nkv), lambda bi, i: (bi, i, 0)),
                   pl.BlockSpec((1, tm, nkv), lambda bi, i: (bi, i, 0))),
        compiler_params=_cparams(("parallel", "parallel")),
        name="inproj_odd",
    )(x, mods, win, cos, sin)


def _swa_kernel(q_ref, km_ref, k0_ref, kp_ref, vm_ref, v0_ref, vp_ref, kc_ref, vc_ref, sink_ref, o_ref, *, n_steps):
    step = pl.program_id(2)
    blk = SWA_BLOCK
    npair = q_ref.shape[2] // LANES
    kall = jnp.concatenate([km_ref[0], k0_ref[0], kp_ref[0]], axis=0)
    vall = jnp.concatenate([vm_ref[0], v0_ref[0], vp_ref[0]], axis=0)
    kc = kc_ref[0]
    vc = vc_ref[0]
    n_ctx = kc.shape[0]
    kc_st = jnp.concatenate([kc[:, :LANES], kc[:, LANES:]], axis=0)
    vc_st = jnp.concatenate([vc[:, :LANES], vc[:, LANES:]], axis=0)
    ind = _stack_pair((_pair_indicator(n_ctx, False), _pair_indicator(n_ctx, True)),
                      (_pair_indicator(3 * blk, False), _pair_indicator(3 * blk, True)))
    r = lax.broadcasted_iota(jnp.int32, (blk, 3 * blk), 0)
    c = lax.broadcasted_iota(jnp.int32, (blk, 3 * blk), 1)
    cc = c % blk
    lane = lax.broadcasted_iota(jnp.int32, (npair * blk, LANES), 1)
    lo = lane < HEAD_DIM

    def band(left_off, right_off):
        ok = ((c >= blk) & (c < 2 * blk)) | ((c < blk) & (cc >= r + left_off)) | ((c >= 2 * blk) & (cc <= r - right_off))
        neg = jnp.where(ok, 0.0, NEG_INF).astype(F32)
        return jnp.concatenate([neg] * npair, axis=0)

    for t in range(SWA_GROUP):
        left_off = jnp.where(step >= 1, 0, blk) if t == 0 else 0
        right_off = jnp.where(step <= n_steps - 2, 0, blk) if t == SWA_GROUP - 1 else 0
        neg = band(left_off, right_off)
        qb = q_ref[0, t * blk:(t + 1) * blk, :]
        qst = jnp.concatenate([qb[:, i * LANES:(i + 1) * LANES] for i in range(npair)], axis=0)
        kw = kall[t * blk:(t + 3) * blk]
        vw = vall[t * blk:(t + 3) * blk]
        k_st = jnp.concatenate([kc_st, kw[:, :LANES], kw[:, LANES:]], axis=0)
        v_st = jnp.concatenate([vc_st, vw[:, :LANES], vw[:, LANES:]], axis=0)
        s = _dot_nt(qst, k_st)
        p_ctx, p_lat, p_sink = [], [], []
        for par in range(2):
            sink = sink_ref[0, par]
            t_c = _tiles(s[:, par * n_ctx:(par + 1) * n_ctx])
            off = 2 * n_ctx + par * 3 * blk
            t_l = _tiles(s[:, off:off + 3 * blk] + neg)
            m = jnp.maximum(_rowmax128(t_c + t_l), sink)
            p_ctx += _exp_tiles(t_c, m)
            p_lat += _exp_tiles(t_l, m)
            p_sink.append(jnp.exp(sink - m))
        o = _dot(jnp.concatenate(p_ctx + p_lat, axis=1), jnp.concatenate([v_st, ind], axis=1))
        o = o[:, :LANES] / (o[:, LANES:] + jnp.where(lo, p_sink[0], p_sink[1]))
        for i in range(npair):
            o_ref[0, t * blk:(t + 1) * blk, i * LANES:(i + 1) * LANES] = o[i * blk:(i + 1) * blk].astype(o_ref.dtype)


def _swa_attention(q, kvar, vvar, kc, vc, sink_tab):
    b, s, nq = q.shape
    blk = SWA_BLOCK
    nb = s // blk
    n_steps = nb // SWA_GROUP
    gw = nq // SWA_KV_HEADS
    n_ctx = kc.shape[1]
    prev = lambda bi, g, j: (bi, jnp.maximum(SWA_GROUP * j - 1, 0), g)
    cur = lambda bi, g, j: (bi, j, g)
    nxt = lambda bi, g, j: (bi, jnp.minimum(SWA_GROUP * (j + 1), nb - 1), g)
    edge_spec = lambda f: pl.BlockSpec((1, blk, 2 * LANES), f)
    main_spec = pl.BlockSpec((1, SWA_GROUP * blk, 2 * LANES), cur)
    ctx_spec = pl.BlockSpec((1, n_ctx, 2 * LANES), lambda bi, g, j: (bi, 0, g))
    return pl.pallas_call(
        functools.partial(_swa_kernel, n_steps=n_steps),
        out_shape=jax.ShapeDtypeStruct((b, s, nq), BF16),
        grid=(b, SWA_KV_HEADS, n_steps),
        in_specs=[pl.BlockSpec((1, SWA_GROUP * blk, gw), cur),
                  edge_spec(prev), main_spec, edge_spec(nxt),
                  edge_spec(prev), main_spec, edge_spec(nxt),
                  ctx_spec, ctx_spec,
                  pl.BlockSpec((1, 2, sink_tab.shape[2], LANES), lambda bi, g, j: (g, 0, 0, 0))],
        out_specs=pl.BlockSpec((1, SWA_GROUP * blk, gw), cur),
        compiler_params=_cparams(("parallel", "parallel", "arbitrary")),
        name="swa_attention",
    )(q, kvar, kvar, kvar, vvar, vvar, vvar, kc, vc, sink_tab)


def _rope_tables(s, dim, pad_to):
    t = jnp.arange(s, dtype=jnp.int32)
    rows, cols = t // GRID_W, t % GRID_W
    half = dim // 2
    inv = ROPE_BASE ** (-jnp.arange(0, half, 2, dtype=F32) / half)
    ar = rows.astype(F32)[:, None] * inv
    ac = cols.astype(F32)[:, None] * inv
    cos = jnp.concatenate([jnp.cos(ar), jnp.cos(ar), jnp.cos(ac), jnp.cos(ac)], axis=1)
    sin = jnp.concatenate([-jnp.sin(ar), jnp.sin(ar), -jnp.sin(ac), jnp.sin(ac)], axis=1)
    if pad_to > dim:
        cos = jnp.concatenate([cos, jnp.ones((s, pad_to - dim), F32)], axis=1)
        sin = jnp.concatenate([sin, jnp.zeros((s, pad_to - dim), F32)], axis=1)
    reps = LANES // cos.shape[1]
    return jnp.tile(cos, (1, reps)), jnp.tile(sin, (1, reps))


def _even_weights(w_in, w_q_up, w_uk, w_uv, w_out):
    d = w_in.shape[0]
    win = jnp.concatenate([w_in[:, :MLA_IN], jnp.zeros((d, 4 * LANES - MLA_IN), w_in.dtype), w_in[:, MLA_IN:]], axis=1)
    wq3 = w_q_up.reshape(MLA_Q_RANK, MLA_HEADS, MLA_NOPE + MLA_ROPE)
    nope = wq3[:, :, :MLA_NOPE].reshape(MLA_Q_RANK, MLA_HEADS * MLA_NOPE)
    rope = jnp.pad(wq3[:, :, MLA_NOPE:], ((0, 0), (0, 0), (0, LANES - MLA_ROPE))).reshape(MLA_Q_RANK, MLA_HEADS * LANES)
    wq = jnp.concatenate([nope, rope], axis=1)
    eye = jnp.eye(MLA_HEADS, dtype=w_uk.dtype)
    wuk = jnp.einsum('hcn,hg->hngc', w_uk, eye).reshape(MLA_HEADS * MLA_NOPE, MLA_HEADS * MLA_KV_RANK)
    wuv = jnp.einsum('hcv,hg->hcgv', w_uv, eye).reshape(MLA_HEADS * MLA_KV_RANK, MLA_HEADS * MLA_V)
    n_mla = MLA_HEADS * MLA_V
    return (win.astype(BF16), wq.astype(BF16), wuk.astype(BF16), wuv.astype(BF16),
            w_out[:n_mla].astype(BF16), w_out[n_mla:].astype(BF16))


def _odd_weights(w_in):
    d = w_in.shape[0]
    nq = SWA_HEADS * HEAD_DIM
    z = jnp.zeros((d, HEAD_DIM), w_in.dtype)

    def variants(off):
        cols = []
        for g in range(SWA_KV_HEADS):
            w = w_in[:, off + g * HEAD_DIM:off + (g + 1) * HEAD_DIM]
            cols += [w, z, z, w]
        return jnp.concatenate(cols, axis=1)

    kcols = variants(nq)
    vcols = variants(nq + SWA_KV_HEADS * HEAD_DIM)
    return jnp.concatenate([w_in[:, :nq], kcols, vcols], axis=1).astype(BF16)


def _sink_table(sinks):
    g = SWA_HEADS // SWA_KV_HEADS
    t = sinks.astype(F32).reshape(SWA_KV_HEADS, g // 2, 2).transpose(0, 2, 1)
    t = jnp.broadcast_to(t[:, :, :, None, None], (SWA_KV_HEADS, 2, g // 2, SWA_BLOCK, LANES))
    return t.reshape(SWA_KV_HEADS, 2, (g // 2) * SWA_BLOCK, LANES)


def kernel(x, c, ctx, c_ctx, mod_w, mod_b, even_w_in, mla_q_norm, mla_kv_norm, mla_w_q_up, mla_w_uk, mla_w_uv,
           na_rel_bias, even_w_out, odd_w_in, swa_sinks, odd_w_out, ffn_w_gate_up, ffn_w_down, final_norm):
    b, s, d = x.shape
    n_ctx = ctx.shape[1]
    n_rows = s // GRID_W
    assert d == D_MODEL and s % (NA_RB * GRID_W) == 0 and n_rows >= NA_SPAN and s % (SWA_GROUP * SWA_BLOCK) == 0
    assert mod_w.shape[0] == 2 and b <= 4

    ctx_grp = 4
    c8 = jnp.zeros((8, d), F32).at[:b].set(c.astype(F32)).at[ctx_grp].set(c_ctx.astype(F32))
    mods = _modulation(c8, mod_w.astype(F32), mod_b.astype(F32))
    mods0 = mods[0].reshape(8, 1, 6 * d)
    mods1 = mods[1].reshape(8, 1, 6 * d)
    lat_grp = lambda bi: bi
    ctx_g = lambda bi: ctx_grp

    tm = min(512, s)
    tf = 256
    wgu = ffn_w_gate_up.astype(BF16)
    wdn = ffn_w_down.astype(BF16)

    win, wq, wuk, wuv, wo_mla, wo_na = _even_weights(even_w_in[0], mla_w_q_up[0], mla_w_uk[0], mla_w_uv[0],
                                                      even_w_out[0])
    qn = mla_q_norm[0].astype(F32).reshape(1, -1)
    kvn = mla_kv_norm[0].astype(F32).reshape(1, -1)
    cos_m, sin_m = _rope_tables(s, MLA_ROPE, LANES)
    one_c = jnp.ones((n_ctx, LANES), F32)
    zero_c = jnp.zeros((n_ctx, LANES), F32)

    q_l, k_l, naq_l, nak_l, nav_l = _inproj0(x, mods0, lat_grp, win, qn, wq, wuk, kvn, cos_m, sin_m, tm)
    q_c, k_c, naq_c, nak_c, nav_c = _inproj0(ctx, mods0, ctx_g, win, qn, wq, wuk, kvn, one_c, zero_c, n_ctx)

    k_all = jnp.concatenate([k_c, k_l], axis=1)
    kt_all = jnp.swapaxes(k_all, 1, 2)
    tk = 768 if (n_ctx + s) % 768 == 0 else 256
    o_mla_l = _mla_attention(q_l, kt_all, k_all, wuv, min(256, s), tk)
    o_mla_c = _mla_attention(q_c, jnp.swapaxes(k_c, 1, 2), k_c, wuv, n_ctx, n_ctx)

    bias = _na_bias_table(na_rel_bias[0], n_rows)
    o_na_l = _na_attention(naq_l, nak_l, nav_l, nak_c, nav_c, bias)
    o_na_c = _pair_ctx_attention(naq_c, nak_c, nav_c)

    h_lat = _ffn_block(x, mods0, lat_grp, [o_mla_l, o_na_l], [wo_mla, wo_na], wgu[0], wdn[0], None, tm, tf)
    h_ctx = _ffn_block(ctx, mods0, ctx_g, [o_mla_c, o_na_c], [wo_mla, wo_na], wgu[0], wdn[0], None, n_ctx, tf)

    win1 = _odd_weights(odd_w_in[0])
    cos_s, sin_s = _rope_tables(s, HEAD_DIM, HEAD_DIM)
    q1, k1, v1 = _inproj1(h_lat, mods1, lat_grp, win1, cos_s, sin_s, tm)
    _, k1c, v1c = _inproj1(h_ctx, mods1, ctx_g, win1, one_c, zero_c, n_ctx)
    o_swa = _swa_attention(q1, k1, v1, k1c, v1c, _sink_table(swa_sinks[0]))
    fn = final_norm.astype(F32).reshape(1, d)
    return _ffn_block(h_lat, mods1, lat_grp, [o_swa], [odd_w_out[0].astype(BF16)], wgu[1], wdn[1], fn, tm, tf)
```

```python
import functools

import numpy as np
import jax
import jax.numpy as jnp
from jax import lax
from jax.experimental import pallas as pl
from jax.experimental.pallas import tpu as pltpu

F32 = jnp.float32
BF16 = jnp.bfloat16

D_MODEL = 1024
GRID_W = 64
HEAD_DIM = 64
ROPE_BASE = 10000.0
EPS = 1e-6
NEG_INF = -1e30
LOG2_E = 1.4426950408889634

MLA_HEADS = 8
MLA_NOPE = 64
MLA_ROPE = 32
MLA_V = 64
MLA_Q_RANK = 256
MLA_KV_RANK = 128
MLA_IN = MLA_Q_RANK + MLA_KV_RANK + MLA_ROPE

NA_HEADS = 8
NA_KR = 8
NA_KC = 16
NA_RB = 8
NA_SPAN = 16
NA_STEP_BLOCKS = 4

SWA_HEADS = 16
SWA_KV_HEADS = 2
SWA_WINDOW = 128
SWA_BLOCK = 128
SWA_GROUP = 4

LANES = 128
MLA_QK_PAD = 256
ONES_LANE = MLA_ROPE
VMEM_LIMIT = 56 * 1024 * 1024


def _cparams(sem):
    return pltpu.CompilerParams(dimension_semantics=sem, vmem_limit_bytes=VMEM_LIMIT)


def _dot(a, b):
    return jnp.dot(a, b, preferred_element_type=F32)


def _dot_nt(a, b):
    return lax.dot_general(a, b, (((1,), (1,)), ((), ())), preferred_element_type=F32)


def _rms(x):
    return x * lax.rsqrt(jnp.mean(x * x, axis=-1, keepdims=True) + EPS)


def _silu(x):
    return x * (1.0 / (1.0 + jnp.exp(-x)))


def _rope_tile(x, cos, sin, quarter):
    lane = lax.broadcasted_iota(jnp.int32, x.shape, 1)
    first = (lane % (2 * quarter)) < quarter
    swapped = jnp.where(first, pltpu.roll(x, LANES - quarter, 1), pltpu.roll(x, quarter, 1))
    return x * cos + swapped * sin


def _tiles(s):
    return [s[:, j * LANES:(j + 1) * LANES] for j in range(s.shape[1] // LANES)]


def _rowmax128(tiles):
    mx = tiles[0]
    for t in tiles[1:]:
        mx = jnp.maximum(mx, t)
    return jnp.broadcast_to(jnp.max(mx, axis=1, keepdims=True), mx.shape)


def _rowmax(s):
    return _rowmax128(_tiles(s))


def _exp_tiles(tiles, m):
    return [jnp.exp(t - m).astype(BF16) for t in tiles]


def _pair_indicator(n, hi):
    lane = lax.broadcasted_iota(jnp.int32, (n, LANES), 1)
    return jnp.where((lane >= HEAD_DIM) == hi, 1.0, 0.0).astype(BF16)


def _mod_kernel(c_ref, w_ref, b_ref, o_ref):
    a = _silu(c_ref[...])
    o_ref[0] = jnp.dot(a, w_ref[0], precision=lax.Precision.HIGHEST, preferred_element_type=F32) + b_ref[0]


def _modulation(c8, mod_w, mod_b):
    depth, d, n = mod_w.shape
    tn = 1536
    return pl.pallas_call(
        _mod_kernel,
        out_shape=jax.ShapeDtypeStruct((depth, 8, n), F32),
        grid=(depth, n // tn),
        in_specs=[pl.BlockSpec((8, d), lambda l, j: (0, 0)),
                  pl.BlockSpec((1, d, tn), lambda l, j: (l, 0, j)),
                  pl.BlockSpec((1, 1, tn), lambda l, j: (l, 0, j))],
        out_specs=pl.BlockSpec((1, 8, tn), lambda l, j: (l, 0, j)),
        compiler_params=_cparams(("parallel", "parallel")),
        name="modulation",
    )(c8, mod_w, mod_b.reshape(depth, 1, n))


def _inproj0_kernel(x_ref, mod_ref, win_ref, qn_ref, wq_ref, wuk_ref, kvn_ref, cos_ref, sin_ref,
                    qmla_ref, kmla_ref, naq_ref, nak_ref, nav_ref):
    d = D_MODEL
    x = x_ref[0]
    shift = mod_ref[0, :, 0:d]
    scale = mod_ref[0, :, d:2 * d]
    a = (_rms(x) * (1.0 + scale) + shift).astype(BF16)
    p = _dot(a, win_ref[...])
    cos = cos_ref[...]
    sin = sin_ref[...]
    mla_scale = (MLA_NOPE + MLA_ROPE) ** -0.5 * LOG2_E

    cq = (_rms(p[:, 0:MLA_Q_RANK]) * qn_ref[...]).astype(BF16)
    q = _dot(cq, wq_ref[...])
    n_nope = MLA_HEADS * MLA_NOPE
    q_lat = _dot(q[:, 0:n_nope].astype(BF16), wuk_ref[...])
    for h in range(MLA_HEADS):
        qr = _rope_tile(q[:, n_nope + h * LANES:n_nope + (h + 1) * LANES], cos, sin, MLA_ROPE // 4)
        qmla_ref[0, h, :, 0:LANES] = (q_lat[:, h * LANES:(h + 1) * LANES] * mla_scale).astype(BF16)
        qmla_ref[0, h, :, LANES:2 * LANES] = (qr * mla_scale).astype(BF16)

    ckv = _rms(p[:, MLA_Q_RANK:MLA_Q_RANK + MLA_KV_RANK]) * kvn_ref[...]
    kr = _rope_tile(p[:, 3 * LANES:4 * LANES], cos, sin, MLA_ROPE // 4)
    lane = lax.broadcasted_iota(jnp.int32, kr.shape, 1)
    kr = jnp.where(lane == ONES_LANE, 1.0, kr)
    kmla_ref[0, :, 0:LANES] = ckv.astype(BF16)
    kmla_ref[0, :, LANES:2 * LANES] = kr.astype(BF16)

    w = NA_HEADS * HEAD_DIM
    naq_ref[0] = (p[:, 4 * LANES:4 * LANES + w] * (HEAD_DIM ** -0.5)).astype(BF16)
    nak_ref[0] = p[:, 4 * LANES + w:4 * LANES + 2 * w].astype(BF16)
    nav_ref[0] = p[:, 4 * LANES + 2 * w:4 * LANES + 3 * w].astype(BF16)


def _inproj0(x, mods, grp, win, qn, wq, wuk, kvn, cos, sin, tm):
    b, s, d = x.shape
    nt = s // tm
    w = NA_HEADS * HEAD_DIM
    const = lambda bi, i: (0, 0)
    return pl.pallas_call(
        _inproj0_kernel,
        out_shape=(jax.ShapeDtypeStruct((b, MLA_HEADS, s, MLA_QK_PAD), BF16),
                   jax.ShapeDtypeStruct((b, s, MLA_QK_PAD), BF16),
                   jax.ShapeDtypeStruct((b, s, w), BF16),
                   jax.ShapeDtypeStruct((b, s, w), BF16),
                   jax.ShapeDtypeStruct((b, s, w), BF16)),
        grid=(b, nt),
        in_specs=[pl.BlockSpec((1, tm, d), lambda bi, i: (bi, i, 0)),
                  pl.BlockSpec((1, 1, mods.shape[2]), lambda bi, i: (grp(bi), 0, 0)),
                  pl.BlockSpec(win.shape, const),
                  pl.BlockSpec(qn.shape, const),
                  pl.BlockSpec(wq.shape, const),
                  pl.BlockSpec(wuk.shape, const),
                  pl.BlockSpec(kvn.shape, const),
                  pl.BlockSpec((tm, LANES), lambda bi, i: (i, 0)),
                  pl.BlockSpec((tm, LANES), lambda bi, i: (i, 0))],
        out_specs=(pl.BlockSpec((1, MLA_HEADS, tm, MLA_QK_PAD), lambda bi, i: (bi, 0, i, 0)),
                   pl.BlockSpec((1, tm, MLA_QK_PAD), lambda bi, i: (bi, i, 0)),
                   pl.BlockSpec((1, tm, w), lambda bi, i: (bi, i, 0)),
                   pl.BlockSpec((1, tm, w), lambda bi, i: (bi, i, 0)),
                   pl.BlockSpec((1, tm, w), lambda bi, i: (bi, i, 0))),
        compiler_params=_cparams(("parallel", "parallel")),
        name="inproj_even",
    )(x, mods, win, qn, wq, wuk, kvn, cos, sin)


def _mla_kernel(q_ref, kt_ref, k_ref, wuv_ref, o_ref, acc_ref, m_ref, pa_ref, pb_ref, aa_ref, ab_ref, *, tk):
    n_k = k_ref.shape[1] // tk
    acc_ref[...] = jnp.zeros(acc_ref.shape, F32)
    m_ref[...] = jnp.full(m_ref.shape, -jnp.inf, F32)

    def scores(j, p_ref, a_ref):
        kt = kt_ref[0, :, pl.ds(pl.multiple_of(j * tk, tk), tk)]
        for h in range(MLA_HEADS):
            s = _dot(q_ref[0, h], kt)
            m_prev = m_ref[h]
            m_new = jnp.maximum(m_prev, _rowmax(s))
            p_ref[h] = jnp.concatenate([jnp.exp2(t - m_new).astype(BF16) for t in _tiles(s)], axis=1)
            a_ref[h] = jnp.exp2(m_prev - m_new)
            m_ref[h] = m_new

    def values(j, p_ref, a_ref):
        kv = k_ref[0, pl.ds(pl.multiple_of(j * tk, tk), tk), :]
        for h in range(MLA_HEADS):
            alpha = a_ref[h]
            acc_ref[h] = acc_ref[h] * jnp.concatenate([alpha, alpha], axis=1) + _dot(p_ref[h], kv)

    scores(0, pa_ref, aa_ref)

    def pair(i, carry):
        j = 2 * i + 1
        values(j - 1, pa_ref, aa_ref)
        scores(j, pb_ref, ab_ref)
        values(j, pb_ref, ab_ref)
        scores(j + 1, pa_ref, aa_ref)
        return carry

    lax.fori_loop(0, (n_k - 1) // 2, pair, 0)
    values(n_k - 1, pa_ref, aa_ref)

    outs = []
    for h in range(MLA_HEADS):
        acc = acc_ref[h]
        tail = acc[:, LANES:2 * LANES]
        lane = lax.broadcasted_iota(jnp.int32, tail.shape, 1)
        l = jnp.sum(jnp.where(lane == ONES_LANE, tail, 0.0), axis=1, keepdims=True)
        outs.append((acc[:, 0:LANES] / l).astype(BF16))
    o = jnp.concatenate(outs, axis=1)
    o_ref[0] = _dot(o, wuv_ref[...]).astype(o_ref.dtype)


def _mla_attention(q, kt, k, wuv, tq, tk):
    b, h, sq, _ = q.shape
    nk = k.shape[1]
    assert nk % tk == 0 and (nk // tk) % 2 == 1
    wo = wuv.shape[1]
    return pl.pallas_call(
        functools.partial(_mla_kernel, tk=tk),
        out_shape=jax.ShapeDtypeStruct((b, sq, wo), BF16),
        grid=(b, sq // tq),
        in_specs=[pl.BlockSpec((1, h, tq, MLA_QK_PAD), lambda bi, i: (bi, 0, i, 0)),
                  pl.BlockSpec((1, MLA_QK_PAD, nk), lambda bi, i: (bi, 0, 0)),
                  pl.BlockSpec((1, nk, MLA_QK_PAD), lambda bi, i: (bi, 0, 0)),
                  _resident(wuv.shape)],
        out_specs=pl.BlockSpec((1, tq, wo), lambda bi, i: (bi, i, 0)),
        scratch_shapes=[pltpu.VMEM((h, tq, MLA_QK_PAD), F32), pltpu.VMEM((h, tq, LANES), F32),
                        pltpu.VMEM((h, tq, tk), BF16), pltpu.VMEM((h, tq, tk), BF16),
                        pltpu.VMEM((h, tq, LANES), F32), pltpu.VMEM((h, tq, LANES), F32)],
        compiler_params=_cparams(("parallel", "arbitrary")),
        name="mla_attention",
    )(q, kt, k, wuv)


def _split_heads(t):
    lane = lax.broadcasted_iota(jnp.int32, t.shape, 1)
    lo = lane < HEAD_DIM
    zero = jnp.zeros_like(t)
    return jnp.where(lo, t, zero), jnp.where(lo, zero, t)


def _stack_pair(lo_hi_ctx, lo_hi_lat):
    return jnp.concatenate([lo_hi_ctx[0], lo_hi_ctx[1], lo_hi_lat[0], lo_hi_lat[1]], axis=0)


def _na_kernel(q_ref, k_ref, v_ref, kc_ref, vc_ref, bias_ref, o_ref, *, n_rows):
    step = pl.program_id(2)
    nrb = n_rows // NA_RB
    n_keys = NA_SPAN * GRID_W
    n_ctx = kc_ref.shape[1]
    tq = NA_RB * GRID_W
    kcs = _split_heads(kc_ref[0])
    vcs = _split_heads(vc_ref[0])
    ind = _stack_pair((_pair_indicator(n_ctx, False), _pair_indicator(n_ctx, True)),
                      (_pair_indicator(n_keys, False), _pair_indicator(n_keys, True)))
    for i in range(NA_STEP_BLOCKS):
        rb = step * NA_STEP_BLOCKS + i
        case = jnp.minimum(rb, 1) + jnp.maximum(rb - (nrb - 2), 0)
        base = jnp.clip(rb * NA_RB - NA_KR // 2, 0, n_rows - NA_SPAN)
        start = pl.multiple_of(base * GRID_W, 256)
        q = q_ref[0, i * tq:(i + 1) * tq, :]
        k_st = _stack_pair(kcs, _split_heads(k_ref[0, pl.ds(start, n_keys), :]))
        v_st = _stack_pair(vcs, _split_heads(v_ref[0, pl.ds(start, n_keys), :]))
        s = _dot_nt(q, k_st)
        p_ctx, p_lat = [], []
        for par in range(2):
            t_c = _tiles(s[:, par * n_ctx:(par + 1) * n_ctx])
            off = 2 * n_ctx + par * n_keys
            t_l = _tiles(s[:, off:off + n_keys] + bias_ref[case, par])
            m = _rowmax128(t_c + t_l)
            p_ctx += _exp_tiles(t_c, m)
            p_lat += _exp_tiles(t_l, m)
        o = _dot(jnp.concatenate(p_ctx + p_lat, axis=1), jnp.concatenate([v_st, ind], axis=1))
        o_ref[0, i * tq:(i + 1) * tq, :] = (o[:, :LANES] / o[:, LANES:]).astype(o_ref.dtype)


def _na_attention(q, k, v, kc, vc, bias):
    b, s, w = q.shape
    n_rows = s // GRID_W
    nrb = n_rows // NA_RB
    tq = NA_STEP_BLOCKS * NA_RB * GRID_W
    n_ctx = kc.shape[1]
    return pl.pallas_call(
        functools.partial(_na_kernel, n_rows=n_rows),
        out_shape=jax.ShapeDtypeStruct((b, s, w), BF16),
        grid=(w // LANES, b, nrb // NA_STEP_BLOCKS),
        in_specs=[pl.BlockSpec((1, tq, LANES), lambda j, bi, r: (bi, r, j)),
                  pl.BlockSpec((1, s, LANES), lambda j, bi, r: (bi, 0, j)),
                  pl.BlockSpec((1, s, LANES), lambda j, bi, r: (bi, 0, j)),
                  pl.BlockSpec((1, n_ctx, LANES), lambda j, bi, r: (bi, 0, j)),
                  pl.BlockSpec((1, n_ctx, LANES), lambda j, bi, r: (bi, 0, j)),
                  pl.BlockSpec((3, 2, NA_RB * GRID_W, NA_SPAN * GRID_W), lambda j, bi, r: (0, j, 0, 0))],
        out_specs=pl.BlockSpec((1, tq, LANES), lambda j, bi, r: (bi, r, j)),
        compiler_params=_cparams(("parallel", "parallel", "arbitrary")),
        name="na_attention",
    )(q, k, v, kc, vc, bias)


def _pair_ctx_kernel(q_ref, kc_ref, vc_ref, o_ref):
    q = q_ref[0]
    n_ctx = kc_ref.shape[1]
    kcs = _split_heads(kc_ref[0])
    vcs = _split_heads(vc_ref[0])
    s = _dot_nt(q, jnp.concatenate(kcs, axis=0))
    p = []
    for par in range(2):
        t_c = _tiles(s[:, par * n_ctx:(par + 1) * n_ctx])
        p += _exp_tiles(t_c, _rowmax128(t_c))
    ind = jnp.concatenate([_pair_indicator(n_ctx, False), _pair_indicator(n_ctx, True)], axis=0)
    o = _dot(jnp.concatenate(p, axis=1), jnp.concatenate([jnp.concatenate(vcs, axis=0), ind], axis=1))
    o_ref[0] = (o[:, :LANES] / o[:, LANES:]).astype(o_ref.dtype)


def _pair_ctx_attention(q, kc, vc):
    b, n, w = q.shape
    spec = pl.BlockSpec((1, n, LANES), lambda bi, j: (bi, 0, j))
    return pl.pallas_call(
        _pair_ctx_kernel,
        out_shape=jax.ShapeDtypeStruct((b, n, w), BF16),
        grid=(b, w // LANES),
        in_specs=[spec, spec, spec],
        out_specs=spec,
        compiler_params=_cparams(("parallel", "parallel")),
        name="na_ctx_attention",
    )(q, kc, vc)


def _na_bias_table(rel_bias, n_rows):
    qc = np.arange(GRID_W)
    cstart = np.clip(qc - NA_KC // 2, 0, GRID_W - NA_KC)
    kc = np.arange(GRID_W)
    col_ok = (kc[None, :] >= cstart[:, None]) & (kc[None, :] < cstart[:, None] + NA_KC)
    padw = GRID_W - NA_KC
    rbp = jnp.pad(rel_bias.astype(F32), ((0, 0), (0, 0), (padw, padw)))
    toe = jnp.stack([rbp[:, :, GRID_W - 1 - c:2 * GRID_W - 1 - c] for c in range(GRID_W)], axis=2)
    toe = jnp.where(jnp.asarray(col_ok)[None, None], toe, NEG_INF)
    masked = jnp.full((rel_bias.shape[0], 1, GRID_W, GRID_W), NEG_INF, F32)
    toe = jnp.concatenate([toe, masked], axis=1)
    n_dr = 2 * NA_KR - 1
    idx = np.full((3, NA_RB, NA_SPAN), n_dr, np.int32)
    nrb = n_rows // NA_RB
    for case, rb in enumerate((0, min(1, nrb - 1), nrb - 1)):
        r0 = rb * NA_RB
        base = int(np.clip(r0 - NA_KR // 2, 0, n_rows - NA_SPAN))
        for a in range(NA_RB):
            r = r0 + a
            rs = int(np.clip(r - NA_KR // 2, 0, n_rows - NA_KR))
            for t in range(NA_SPAN):
                kr = base + t
                if rs <= kr < rs + NA_KR:
                    idx[case, a, t] = kr - r + NA_KR - 1
    nh = rel_bias.shape[0]
    cases = []
    for case in range(3):
        rows = [jnp.concatenate([toe[:, int(idx[case, a, t])] for t in range(NA_SPAN)], axis=-1)
                for a in range(NA_RB)]
        cases.append(jnp.stack(rows, axis=1).reshape(nh, NA_RB * GRID_W, NA_SPAN * GRID_W))
    return jnp.stack(cases, axis=0)


def _ffn_kernel(*refs, n_attn, tf, final):
    h_ref, mod_ref = refs[0], refs[1]
    attn_refs = refs[2:2 + n_attn]
    wout_refs = refs[2 + n_attn:2 + 2 * n_attn]
    wgu_ref, wd_ref = refs[2 + 2 * n_attn:4 + 2 * n_attn]
    pos = 4 + 2 * n_attn
    fn_ref = refs[pos] if final else None
    pos += 1 if final else 0
    o_ref, a2_ref, f_ref = refs[pos:pos + 3]
    d = D_MODEL
    dff = wd_ref.shape[0]

    proj = _dot(attn_refs[0][0], wout_refs[0][...])
    for t in range(1, n_attn):
        proj = proj + _dot(attn_refs[t][0], wout_refs[t][...])
    h1 = h_ref[0] + mod_ref[0, :, 2 * d:3 * d] * proj
    o_ref[0] = h1
    a2_ref[...] = (_rms(h1) * (1.0 + mod_ref[0, :, 4 * d:5 * d]) + mod_ref[0, :, 3 * d:4 * d]).astype(BF16)

    for j in range(dff // tf):
        a2 = a2_ref[...]
        g = _dot(a2, wgu_ref[:, j * tf:(j + 1) * tf])
        u = _dot(a2, wgu_ref[:, dff + j * tf:dff + (j + 1) * tf])
        f_ref[:, j * tf:(j + 1) * tf] = (_silu(g) * u).astype(BF16)

    out = o_ref[0] + mod_ref[0, :, 5 * d:6 * d] * _dot(f_ref[...], wd_ref[...])
    if final:
        out = _rms(out) * fn_ref[...]
    o_ref[0] = out


def _resident(shape):
    return pl.BlockSpec(shape, lambda *_: (0,) * len(shape), pipeline_mode=pl.Buffered(1))


def _ffn_block(h, mods, grp, attns, wouts, wgu, wd, final_norm, tm, tf):
    b, s, d = h.shape
    dff = wd.shape[0]
    n_attn = len(attns)
    final = final_norm is not None
    in_specs = [pl.BlockSpec((1, tm, d), lambda bi, i: (bi, i, 0)),
                pl.BlockSpec((1, 1, mods.shape[2]), lambda bi, i: (grp(bi), 0, 0))]
    in_specs += [pl.BlockSpec((1, tm, a.shape[2]), lambda bi, i: (bi, i, 0)) for a in attns]
    in_specs += [_resident(w.shape) for w in wouts]
    in_specs += [_resident(wgu.shape), _resident(wd.shape)]
    args = [h, mods, *attns, *wouts, wgu, wd]
    if final:
        in_specs.append(_resident((1, d)))
        args.append(final_norm)
    return pl.pallas_call(
        functools.partial(_ffn_kernel, n_attn=n_attn, tf=tf, final=final),
        out_shape=jax.ShapeDtypeStruct((b, s, d), F32),
        grid=(b, s // tm),
        in_specs=in_specs,
        out_specs=pl.BlockSpec((1, tm, d), lambda bi, i: (bi, i, 0)),
        scratch_shapes=[pltpu.VMEM((tm, d), BF16), pltpu.VMEM((tm, dff), BF16)],
        compiler_params=_cparams(("parallel", "parallel")),
        name="outproj_ffn",
    )(*args)


def _inproj1_kernel(x_ref, mod_ref, win_ref, cos_ref, sin_ref, q_ref, k_ref, v_ref):
    d = D_MODEL
    x = x_ref[0]
    a = (_rms(x) * (1.0 + mod_ref[0, :, d:2 * d]) + mod_ref[0, :, 0:d]).astype(BF16)
    p = _dot(a, win_ref[...])
    cos = cos_ref[...]
    sin = sin_ref[...]
    nq = SWA_HEADS * HEAD_DIM
    nkv = 2 * SWA_KV_HEADS * LANES
    quarter = HEAD_DIM // 4
    for t in range(nq // LANES):
        q = _rope_tile(p[:, t * LANES:(t + 1) * LANES], cos, sin, quarter)
        q_ref[0, :, t * LANES:(t + 1) * LANES] = (q * (HEAD_DIM ** -0.5)).astype(BF16)
    for t in range(nkv // LANES):
        k = _rope_tile(p[:, nq + t * LANES:nq + (t + 1) * LANES], cos, sin, quarter)
        k_ref[0, :, t * LANES:(t + 1) * LANES] = k.astype(BF16)
    v_ref[0] = p[:, nq + nkv:nq + 2 * nkv].astype(BF16)


def _inproj1(x, mods, grp, win, cos, sin, tm):
    b, s, d = x.shape
    nq = SWA_HEADS * HEAD_DIM
    nkv = 2 * SWA_KV_HEADS * LANES
    return pl.pallas_call(
        _inproj1_kernel,
        out_shape=(jax.ShapeDtypeStruct((b, s, nq), BF16),
                   jax.ShapeDtypeStruct((b, s, nkv), BF16),
                   jax.ShapeDtypeStruct((b, s, nkv), BF16)),
        grid=(b, s // tm),
        in_specs=[pl.BlockSpec((1, tm, d), lambda bi, i: (bi, i, 0)),
                  pl.BlockSpec((1, 1, mods.shape[2]), lambda bi, i: (grp(bi), 0, 0)),
                  pl.BlockSpec(win.shape, lambda bi, i: (0, 0)),
                  pl.BlockSpec((tm, LANES), lambda bi, i: (i, 0)),
                  pl.BlockSpec((tm, LANES), lambda bi, i: (i, 0))],
        out_specs=(pl.BlockSpec((1, tm, nq), lambda bi, i: (bi, i, 0)),
                   pl.BlockSpec((1, tm, nkv), lambda bi, i: (bi, i, 0)),
                   pl.BlockSpec((1, tm, nkv), lambda bi, i: (bi, i, 0))),
        compiler_params=_cparams(("parallel", "parallel")),
        name="inproj_odd",
    )(x, mods, win, cos, sin)


def _swa_kernel(q_ref, km_ref, k0_ref, kp_ref, vm_ref, v0_ref, vp_ref, kc_ref, vc_ref, sink_ref, o_ref, *, n_steps):
    step = pl.program_id(2)
    blk = SWA_BLOCK
    npair = q_ref.shape[2] // LANES
    kall = jnp.concatenate([km_ref[0], k0_ref[0], kp_ref[0]], axis=0)
    vall = jnp.concatenate([vm_ref[0], v0_ref[0], vp_ref[0]], axis=0)
    kc = kc_ref[0]
    vc = vc_ref[0]
    n_ctx = kc.shape[0]
    kc_st = jnp.concatenate([kc[:, :LANES], kc[:, LANES:]], axis=0)
    vc_st = jnp.concatenate([vc[:, :LANES], vc[:, LANES:]], axis=0)
    ind = _stack_pair((_pair_indicator(n_ctx, False), _pair_indicator(n_ctx, True)),
                      (_pair_indicator(3 * blk, False), _pair_indicator(3 * blk, True)))
    r = lax.broadcasted_iota(jnp.int32, (blk, 3 * blk), 0)
    c = lax.broadcasted_iota(jnp.int32, (blk, 3 * blk), 1)
    cc = c % blk
    lane = lax.broadcasted_iota(jnp.int32, (npair * blk, LANES), 1)
    lo = lane < HEAD_DIM

    def band(left_off, right_off):
        ok = ((c >= blk) & (c < 2 * blk)) | ((c < blk) & (cc >= r + left_off)) | ((c >= 2 * blk) & (cc <= r - right_off))
        neg = jnp.where(ok, 0.0, NEG_INF).astype(F32)
        return jnp.concatenate([neg] * npair, axis=0)

    for t in range(SWA_GROUP):
        left_off = jnp.where(step >= 1, 0, blk) if t == 0 else 0
        right_off = jnp.where(step <= n_steps - 2, 0, blk) if t == SWA_GROUP - 1 else 0
        neg = band(left_off, right_off)
        qb = q_ref[0, t * blk:(t + 1) * blk, :]
        qst = jnp.concatenate([qb[:, i * LANES:(i + 1) * LANES] for i in range(npair)], axis=0)
        kw = kall[t * blk:(t + 3) * blk]
        vw = vall[t * blk:(t + 3) * blk]
        k_st = jnp.concatenate([kc_st, kw[:, :LANES], kw[:, LANES:]], axis=0)
        v_st = jnp.concatenate([vc_st, vw[:, :LANES], vw[:, LANES:]], axis=0)
        s = _dot_nt(qst, k_st)
        p_ctx, p_lat, p_sink = [], [], []
        for par in range(2):
            sink = sink_ref[0, par]
            t_c = _tiles(s[:, par * n_ctx:(par + 1) * n_ctx])
            off = 2 * n_ctx + par * 3 * blk
            t_l = _tiles(s[:, off:off + 3 * blk] + neg)
            m = jnp.maximum(_rowmax128(t_c + t_l), sink)
            p_ctx += _exp_tiles(t_c, m)
            p_lat += _exp_tiles(t_l, m)
            p_sink.append(jnp.exp(sink - m))
        o = _dot(jnp.concatenate(p_ctx + p_lat, axis=1), jnp.concatenate([v_st, ind], axis=1))
        o = o[:, :LANES] / (o[:, LANES:] + jnp.where(lo, p_sink[0], p_sink[1]))
        for i in range(npair):
            o_ref[0, t * blk:(t + 1) * blk, i * LANES:(i + 1) * LANES] = o[i * blk:(i + 1) * blk].astype(o_ref.dtype)


def _swa_attention(q, kvar, vvar, kc, vc, sink_tab):
    b, s, nq = q.shape
    blk = SWA_BLOCK
    nb = s // blk
    n_steps = nb // SWA_GROUP
    gw = nq // SWA_KV_HEADS
    n_ctx = kc.shape[1]
    prev = lambda bi, g, j: (bi, jnp.maximum(SWA_GROUP * j - 1, 0), g)
    cur = lambda bi, g, j: (bi, j, g)
    nxt = lambda bi, g, j: (bi, jnp.minimum(SWA_GROUP * (j + 1), nb - 1), g)
    edge_spec = lambda f: pl.BlockSpec((1, blk, 2 * LANES), f)
    main_spec = pl.BlockSpec((1, SWA_GROUP * blk, 2 * LANES), cur)
    ctx_spec = pl.BlockSpec((1, n_ctx, 2 * LANES), lambda bi, g, j: (bi, 0, g))
    return pl.pallas_call(
        functools.partial(_swa_kernel, n_steps=n_steps),
        out_shape=jax.ShapeDtypeStruct((b, s, nq), BF16),
        grid=(b, SWA_KV_HEADS, n_steps),
        in_specs=[pl.BlockSpec((1, SWA_GROUP * blk, gw), cur),
                  edge_spec(prev), main_spec, edge_spec(nxt),
                  edge_spec(prev), main_spec, edge_spec(nxt),
                  ctx_spec, ctx_spec,
                  pl.BlockSpec((1, 2, sink_tab.shape[2], LANES), lambda bi, g, j: (g, 0, 0, 0))],
        out_specs=pl.BlockSpec((1, SWA_GROUP * blk, gw), cur),
        compiler_params=_cparams(("parallel", "parallel", "arbitrary")),
        name="swa_attention",
    )(q, kvar, kvar, kvar, vvar, vvar, vvar, kc, vc, sink_tab)


def _rope_tables(s, dim, pad_to):
    t = jnp.arange(s, dtype=jnp.int32)
    rows, cols = t // GRID_W, t % GRID_W
    half = dim // 2
    inv = ROPE_BASE ** (-jnp.arange(0, half, 2, dtype=F32) / half)
    ar = rows.astype(F32)[:, None] * inv
    ac = cols.astype(F32)[:, None] * inv
    cos = jnp.concatenate([jnp.cos(ar), jnp.cos(ar), jnp.cos(ac), jnp.cos(ac)], axis=1)
    sin = jnp.concatenate([-jnp.sin(ar), jnp.sin(ar), -jnp.sin(ac), jnp.sin(ac)], axis=1)
    if pad_to > dim:
        cos = jnp.concatenate([cos, jnp.ones((s, pad_to - dim), F32)], axis=1)
        sin = jnp.concatenate([sin, jnp.zeros((s, pad_to - dim), F32)], axis=1)
    reps = LANES // cos.shape[1]
    return jnp.tile(cos, (1, reps)), jnp.tile(sin, (1, reps))


def _even_weights(w_in, w_q_up, w_uk, w_uv, w_out):
    d = w_in.shape[0]
    win = jnp.concatenate([w_in[:, :MLA_IN], jnp.zeros((d, 4 * LANES - MLA_IN), w_in.dtype), w_in[:, MLA_IN:]], axis=1)
    wq3 = w_q_up.reshape(MLA_Q_RANK, MLA_HEADS, MLA_NOPE + MLA_ROPE)
    nope = wq3[:, :, :MLA_NOPE].reshape(MLA_Q_RANK, MLA_HEADS * MLA_NOPE)
    rope = jnp.pad(wq3[:, :, MLA_NOPE:], ((0, 0), (0, 0), (0, LANES - MLA_ROPE))).reshape(MLA_Q_RANK, MLA_HEADS * LANES)
    wq = jnp.concatenate([nope, rope], axis=1)
    eye = jnp.eye(MLA_HEADS, dtype=w_uk.dtype)
    wuk = jnp.einsum('hcn,hg->hngc', w_uk, eye).reshape(MLA_HEADS * MLA_NOPE, MLA_HEADS * MLA_KV_RANK)
    wuv = jnp.einsum('hcv,hg->hcgv', w_uv, eye).reshape(MLA_HEADS * MLA_KV_RANK, MLA_HEADS * MLA_V)
    n_mla = MLA_HEADS * MLA_V
    return (win.astype(BF16), wq.astype(BF16), wuk.astype(BF16), wuv.astype(BF16),
            w_out[:n_mla].astype(BF16), w_out[n_mla:].astype(BF16))


def _odd_weights(w_in):
    d = w_in.shape[0]
    nq = SWA_HEADS * HEAD_DIM
    z = jnp.zeros((d, HEAD_DIM), w_in.dtype)

    def variants(off):
        cols = []
        for g in range(SWA_KV_HEADS):
            w = w_in[:, off + g * HEAD_DIM:off + (g + 1) * HEAD_DIM]
            cols += [w, z, z, w]
        return jnp.concatenate(cols, axis=1)

    kcols = variants(nq)
    vcols = variants(nq + SWA_KV_HEADS * HEAD_DIM)
    return jnp.concatenate([w_in[:, :nq], kcols, vcols], axis=1).astype(BF16)


def _sink_table(sinks):
    g = SWA_HEADS // SWA_KV_HEADS
    t = sinks.astype(F32).reshape(SWA_KV_HEADS, g // 2, 2).transpose(0, 2, 1)
    t = jnp.broadcast_to(t[:, :, :, None, None], (SWA_KV_HEADS, 2, g // 2, SWA_BLOCK, LANES))
    return t.reshape(SWA_KV_HEADS, 2, (g // 2) * SWA_BLOCK, LANES)


def kernel(x, c, ctx, c_ctx, mod_w, mod_b, even_w_in, mla_q_norm, mla_kv_norm, mla_w_q_up, mla_w_uk, mla_w_uv,
           na_rel_bias, even_w_out, odd_w_in, swa_sinks, odd_w_out, ffn_w_gate_up, ffn_w_down, final_norm):
    b, s, d = x.shape
    n_ctx = ctx.shape[1]
    n_rows = s // GRID_W
    assert d == D_MODEL and s % (NA_STEP_BLOCKS * NA_RB * GRID_W) == 0 and s % (SWA_GROUP * SWA_BLOCK) == 0
    assert mod_w.shape[0] == 2 and b <= 4

    ctx_grp = 4
    c8 = jnp.zeros((8, d), F32).at[:b].set(c.astype(F32)).at[ctx_grp].set(c_ctx.astype(F32))
    mods = _modulation(c8, mod_w.astype(F32), mod_b.astype(F32))
    mods0 = mods[0].reshape(8, 1, 6 * d)
    mods1 = mods[1].reshape(8, 1, 6 * d)
    lat_grp = lambda bi: bi
    ctx_g = lambda bi: ctx_grp

    tm = min(512, s)
    tf = 256
    wgu = ffn_w_gate_up.astype(BF16)
    wdn = ffn_w_down.astype(BF16)

    win, wq, wuk, wuv, wo_mla, wo_na = _even_weights(even_w_in[0], mla_w_q_up[0], mla_w_uk[0], mla_w_uv[0],
                                                      even_w_out[0])
    qn = mla_q_norm[0].astype(F32).reshape(1, -1)
    kvn = mla_kv_norm[0].astype(F32).reshape(1, -1)
    cos_m, sin_m = _rope_tables(s, MLA_ROPE, LANES)
    one_c = jnp.ones((n_ctx, LANES), F32)
    zero_c = jnp.zeros((n_ctx, LANES), F32)

    q_l, k_l, naq_l, nak_l, nav_l = _inproj0(x, mods0, lat_grp, win, qn, wq, wuk, kvn, cos_m, sin_m, tm)
    q_c, k_c, naq_c, nak_c, nav_c = _inproj0(ctx, mods0, ctx_g, win, qn, wq, wuk, kvn, one_c, zero_c, n_ctx)

    k_all = jnp.concatenate([k_c, k_l], axis=1)
    kt_all = jnp.swapaxes(k_all, 1, 2)
    tk = 768 if (n_ctx + s) % 768 == 0 else 256
    o_mla_l = _mla_attention(q_l, kt_all, k_all, wuv, min(256, s), tk)
    o_mla_c = _mla_attention(q_c, jnp.swapaxes(k_c, 1, 2), k_c, wuv, n_ctx, n_ctx)

    bias = _na_bias_table(na_rel_bias[0], n_rows)
    o_na_l = _na_attention(naq_l, nak_l, nav_l, nak_c, nav_c, bias)
    o_na_c = _pair_ctx_attention(naq_c, nak_c, nav_c)

    h_lat = _ffn_block(x, mods0, lat_grp, [o_mla_l, o_na_l], [wo_mla, wo_na], wgu[0], wdn[0], None, tm, tf)
    h_ctx = _ffn_block(ctx, mods0, ctx_g, [o_mla_c, o_na_c], [wo_mla, wo_na], wgu[0], wdn[0], None, n_ctx, tf)

    win1 = _odd_weights(odd_w_in[0])
    cos_s, sin_s = _rope_tables(s, HEAD_DIM, HEAD_DIM)
    q1, k1, v1 = _inproj1(h_lat, mods1, lat_grp, win1, cos_s, sin_s, tm)
    _, k1c, v1c = _inproj1(h_ctx, mods1, ctx_g, win1, one_c, zero_c, n_ctx)
    o_swa = _swa_attention(q1, k1, v1, k1c, v1c, _sink_table(swa_sinks[0]))
    fn = final_norm.astype(F32).reshape(1, d)
    return _ffn_block(h_lat, mods1, lat_grp, [o_swa], [odd_w_out[0].astype(BF16)], wgu[1], wdn[1], fn, tm, tf)
```

```python
import functools

import numpy as np
import jax
import jax.numpy as jnp
from jax import lax
from jax.experimental import pallas as pl
from jax.experimental.pallas import tpu as pltpu

F32 = jnp.float32
BF16 = jnp.bfloat16

D_MODEL = 1024
GRID_W = 64
HEAD_DIM = 64
ROPE_BASE = 10000.0
EPS = 1e-6
NEG_INF = -1e30
LOG2_E = 1.4426950408889634

MLA_HEADS = 8
MLA_NOPE = 64
MLA_ROPE = 32
MLA_V = 64
MLA_Q_RANK = 256
MLA_KV_RANK = 128
MLA_IN = MLA_Q_RANK + MLA_KV_RANK + MLA_ROPE

NA_HEADS = 8
NA_KR = 8
NA_KC = 16
NA_RB = 8
NA_SPAN = 16
NA_STEP_BLOCKS = 4

SWA_HEADS = 16
SWA_KV_HEADS = 2
SWA_WINDOW = 128
SWA_BLOCK = 128
SWA_GROUP = 4

LANES = 128
MLA_QK_PAD = 256
MLA_ITEM_ROWS = 256
MLA_KEY_CHUNK = 768
ONES_LANE = MLA_ROPE
VMEM_LIMIT = 56 * 1024 * 1024


def _cparams(sem):
    return pltpu.CompilerParams(dimension_semantics=sem, vmem_limit_bytes=VMEM_LIMIT)


def _dot(a, b):
    return jnp.dot(a, b, preferred_element_type=F32)


def _dot_nt(a, b):
    return lax.dot_general(a, b, (((1,), (1,)), ((), ())), preferred_element_type=F32)


def _rms(x):
    return x * lax.rsqrt(jnp.mean(x * x, axis=-1, keepdims=True) + EPS)


def _silu(x):
    return x * (1.0 / (1.0 + jnp.exp(-x)))


def _rope_tile(x, cos, sin, quarter):
    lane = lax.broadcasted_iota(jnp.int32, x.shape, 1)
    first = (lane % (2 * quarter)) < quarter
    swapped = jnp.where(first, pltpu.roll(x, LANES - quarter, 1), pltpu.roll(x, quarter, 1))
    return x * cos + swapped * sin


def _zero_of(x):
    u = lax.bitcast_convert_type(x, jnp.uint32)
    u = lax.shift_right_logical(lax.shift_right_logical(u, jnp.uint32(16)), jnp.uint32(16))
    return lax.bitcast_convert_type(u, F32)


def _tiles(s):
    return [s[:, j * LANES:(j + 1) * LANES] for j in range(s.shape[1] // LANES)]


def _rowmax128(tiles):
    mx = tiles[0]
    for t in tiles[1:]:
        mx = jnp.maximum(mx, t)
    return jnp.broadcast_to(jnp.max(mx, axis=1, keepdims=True), mx.shape)


def _rowmax(s):
    return _rowmax128(_tiles(s))


def _exp_tiles(tiles, m):
    return [jnp.exp(t - m).astype(BF16) for t in tiles]


def _pair_indicator(n, hi):
    lane = lax.broadcasted_iota(jnp.int32, (n, LANES), 1)
    return jnp.where((lane >= HEAD_DIM) == hi, 1.0, 0.0).astype(BF16)


def _mod_kernel(c_ref, w_ref, b_ref, o_ref):
    a = _silu(c_ref[...])
    o_ref[0] = jnp.dot(a, w_ref[0], precision=lax.Precision.HIGHEST, preferred_element_type=F32) + b_ref[0]


def _modulation(c8, mod_w, mod_b):
    depth, d, n = mod_w.shape
    tn = 1536
    return pl.pallas_call(
        _mod_kernel,
        out_shape=jax.ShapeDtypeStruct((depth, 8, n), F32),
        grid=(depth, n // tn),
        in_specs=[pl.BlockSpec((8, d), lambda l, j: (0, 0)),
                  pl.BlockSpec((1, d, tn), lambda l, j: (l, 0, j)),
                  pl.BlockSpec((1, 1, tn), lambda l, j: (l, 0, j))],
        out_specs=pl.BlockSpec((1, 8, tn), lambda l, j: (l, 0, j)),
        compiler_params=_cparams(("parallel", "parallel")),
        name="modulation",
    )(c8, mod_w, mod_b.reshape(depth, 1, n))


def _inproj0_kernel(x_ref, mod_ref, win_ref, qn_ref, wq_ref, wuk_ref, kvn_ref, cos_ref, sin_ref,
                    qmla_ref, kmla_ref, naq_ref, nak_ref, nav_ref):
    d = D_MODEL
    x = x_ref[0]
    shift = mod_ref[0, :, 0:d]
    scale = mod_ref[0, :, d:2 * d]
    a = (_rms(x) * (1.0 + scale) + shift).astype(BF16)
    p = _dot(a, win_ref[...])
    cos = cos_ref[...]
    sin = sin_ref[...]
    mla_scale = (MLA_NOPE + MLA_ROPE) ** -0.5 * LOG2_E

    cq = (_rms(p[:, 0:MLA_Q_RANK]) * qn_ref[...]).astype(BF16)
    q = _dot(cq, wq_ref[...])
    n_nope = MLA_HEADS * MLA_NOPE
    q_lat = _dot(q[:, 0:n_nope].astype(BF16), wuk_ref[...])
    for h in range(MLA_HEADS):
        qr = _rope_tile(q[:, n_nope + h * LANES:n_nope + (h + 1) * LANES], cos, sin, MLA_ROPE // 4)
        qmla_ref[0, h, :, 0:LANES] = (q_lat[:, h * LANES:(h + 1) * LANES] * mla_scale).astype(BF16)
        qmla_ref[0, h, :, LANES:2 * LANES] = (qr * mla_scale).astype(BF16)

    ckv = _rms(p[:, MLA_Q_RANK:MLA_Q_RANK + MLA_KV_RANK]) * kvn_ref[...]
    kr = _rope_tile(p[:, 3 * LANES:4 * LANES], cos, sin, MLA_ROPE // 4)
    lane = lax.broadcasted_iota(jnp.int32, kr.shape, 1)
    kr = jnp.where(lane == ONES_LANE, 1.0, kr)
    kmla_ref[0, :, 0:LANES] = ckv.astype(BF16)
    kmla_ref[0, :, LANES:2 * LANES] = kr.astype(BF16)

    w = NA_HEADS * HEAD_DIM
    naq_ref[0] = (p[:, 4 * LANES:4 * LANES + w] * (HEAD_DIM ** -0.5)).astype(BF16)
    nak_ref[0] = p[:, 4 * LANES + w:4 * LANES + 2 * w].astype(BF16)
    nav_ref[0] = p[:, 4 * LANES + 2 * w:4 * LANES + 3 * w].astype(BF16)


def _inproj0(x, mods, grp, win, qn, wq, wuk, kvn, cos, sin, tm):
    b, s, d = x.shape
    nt = s // tm
    w = NA_HEADS * HEAD_DIM
    const = lambda bi, i: (0, 0)
    return pl.pallas_call(
        _inproj0_kernel,
        out_shape=(jax.ShapeDtypeStruct((b, MLA_HEADS, s, MLA_QK_PAD), BF16),
                   jax.ShapeDtypeStruct((b, s, MLA_QK_PAD), BF16),
                   jax.ShapeDtypeStruct((b, s, w), BF16),
                   jax.ShapeDtypeStruct((b, s, w), BF16),
                   jax.ShapeDtypeStruct((b, s, w), BF16)),
        grid=(b, nt),
        in_specs=[pl.BlockSpec((1, tm, d), lambda bi, i: (bi, i, 0)),
                  pl.BlockSpec((1, 1, mods.shape[2]), lambda bi, i: (grp(bi), 0, 0)),
                  pl.BlockSpec(win.shape, const),
                  pl.BlockSpec(qn.shape, const),
                  pl.BlockSpec(wq.shape, const),
                  pl.BlockSpec(wuk.shape, const),
                  pl.BlockSpec(kvn.shape, const),
                  pl.BlockSpec((tm, LANES), lambda bi, i: (i, 0)),
                  pl.BlockSpec((tm, LANES), lambda bi, i: (i, 0))],
        out_specs=(pl.BlockSpec((1, MLA_HEADS, tm, MLA_QK_PAD), lambda bi, i: (bi, 0, i, 0)),
                   pl.BlockSpec((1, tm, MLA_QK_PAD), lambda bi, i: (bi, i, 0)),
                   pl.BlockSpec((1, tm, w), lambda bi, i: (bi, i, 0)),
                   pl.BlockSpec((1, tm, w), lambda bi, i: (bi, i, 0)),
                   pl.BlockSpec((1, tm, w), lambda bi, i: (bi, i, 0))),
        compiler_params=_cparams(("parallel", "parallel")),
        name="inproj_even",
    )(x, mods, win, qn, wq, wuk, kvn, cos, sin)


def _mla_kernel(q_ref, kt_ref, k_ref, wuv_ref, o_ref, s0_ref, s1_ref, p0_ref, p1_ref, m0_ref, m1_ref, oh_ref):
    n_heads, tq = q_ref.shape[1], q_ref.shape[2]
    rc, nk = s0_ref.shape
    n_items = (tq // rc) * n_heads
    s_refs, p_refs, m_refs = (s0_ref, s1_ref), (p0_ref, p1_ref), (m0_ref, m1_ref)

    def item(i):
        if isinstance(i, int):
            return i % n_heads, (i // n_heads) * rc
        return lax.rem(i, n_heads), pl.multiple_of(lax.div(i, n_heads) * rc, rc)

    tk = MLA_KEY_CHUNK if nk % MLA_KEY_CHUNK == 0 else 2 * LANES

    def round_(r, par):
        static = isinstance(r, int)
        do_qk = not static or r < n_items
        do_sm = not static or 1 <= r <= n_items
        do_pv = not static or 2 <= r <= n_items + 1
        if do_qk:
            h_q, row_q = item(r)
            q = q_ref[0, h_q, pl.ds(row_q, rc), :]
        rg = 32
        if do_pv:
            acc = None
        for c in range(nk // tk):
            cols = slice(c * tk, (c + 1) * tk)
            if do_qk:
                s = _dot(q, kt_ref[0, :, cols])
                s_refs[par][:, cols] = s
                tiles = _tiles(s)
                for j in range(0, len(tiles), 2):
                    pm = jnp.maximum(tiles[j], tiles[j + 1])
                    m_refs[par][...] = pm if c == 0 and j == 0 else jnp.maximum(m_refs[par][...], pm)
            if do_sm:
                zero = jnp.tile(_zero_of(s[0:8, 0:LANES]), (rg // 8, 1)) if do_qk else None
                for g in range(rc // rg):
                    rows = slice(g * rg, (g + 1) * rg)
                    m = m_refs[1 - par][rows, :]
                    m = m + zero if do_qk else m
                    for j in range(c * tk // LANES, (c + 1) * tk // LANES):
                        lanes = slice(j * LANES, (j + 1) * LANES)
                        p_refs[1 - par][rows, lanes] = jnp.exp2(s_refs[1 - par][rows, lanes] - m).astype(BF16)
            if do_pv:
                part = _dot(p_refs[par][:, cols], k_ref[0, cols, :])
                acc = part if acc is None else acc + part
        if do_qk:
            mx = m_refs[par][...]
            m_refs[par][...] = jnp.broadcast_to(jnp.max(mx, axis=1, keepdims=True), mx.shape)
        if do_pv:
            h_v, row_v = item(r - 2)
            tail = acc[:, LANES:2 * LANES]
            lane = lax.broadcasted_iota(jnp.int32, tail.shape, 1)
            l = jnp.sum(jnp.where(lane == ONES_LANE, tail, 0.0), axis=1, keepdims=True)
            oh_ref[h_v, pl.ds(row_v, rc), :] = (acc[:, 0:LANES] / l).astype(BF16)

    round_(0, 0)
    round_(1, 1)
    n_full = max(n_items - 2, 0)

    def pair(i, carry):
        r = 2 + 2 * i
        round_(r, 0)
        round_(r + 1, 1)
        return carry

    lax.fori_loop(0, n_full // 2, pair, 0)
    for r in range(2 + 2 * (n_full // 2), n_items):
        round_(r, r % 2)
    for r in range(max(n_items, 2), n_items + 2):
        round_(r, r % 2)

    o = jnp.concatenate([oh_ref[h] for h in range(n_heads)], axis=1)
    o_ref[0] = _dot(o, wuv_ref[...]).astype(o_ref.dtype)


def _mla_attention(q, kt, k, wuv, tq):
    b, h, sq, _ = q.shape
    nk = k.shape[1]
    rc = min(MLA_ITEM_ROWS, tq)
    assert nk % LANES == 0 and tq % rc == 0 and sq % tq == 0
    wo = wuv.shape[1]
    one_buffer = pl.Buffered(1)
    return pl.pallas_call(
        _mla_kernel,
        out_shape=jax.ShapeDtypeStruct((b, sq, wo), BF16),
        grid=(b, sq // tq),
        in_specs=[pl.BlockSpec((1, h, tq, MLA_QK_PAD), lambda bi, i: (bi, 0, i, 0)),
                  pl.BlockSpec((1, MLA_QK_PAD, nk), lambda bi, i: (bi, 0, 0), pipeline_mode=one_buffer),
                  pl.BlockSpec((1, nk, MLA_QK_PAD), lambda bi, i: (bi, 0, 0), pipeline_mode=one_buffer),
                  _resident(wuv.shape)],
        out_specs=pl.BlockSpec((1, tq, wo), lambda bi, i: (bi, i, 0)),
        scratch_shapes=[pltpu.VMEM((rc, nk), F32), pltpu.VMEM((rc, nk), F32),
                        pltpu.VMEM((rc, nk), BF16), pltpu.VMEM((rc, nk), BF16),
                        pltpu.VMEM((rc, LANES), F32), pltpu.VMEM((rc, LANES), F32),
                        pltpu.VMEM((h, tq, LANES), BF16)],
        compiler_params=_cparams(("parallel", "arbitrary")),
        name="mla_attention",
    )(q, kt, k, wuv)


def _split_heads(t):
    lane = lax.broadcasted_iota(jnp.int32, t.shape, 1)
    lo = lane < HEAD_DIM
    zero = jnp.zeros_like(t)
    return jnp.where(lo, t, zero), jnp.where(lo, zero, t)


def _stack_pair(lo_hi_ctx, lo_hi_lat):
    return jnp.concatenate([lo_hi_ctx[0], lo_hi_ctx[1], lo_hi_lat[0], lo_hi_lat[1]], axis=0)


def _na_kernel(q_ref, k_ref, v_ref, kc_ref, vc_ref, bias_ref, o_ref, *, n_rows):
    step = pl.program_id(2)
    nrb = n_rows // NA_RB
    n_keys = NA_SPAN * GRID_W
    n_ctx = kc_ref.shape[1]
    tq = NA_RB * GRID_W
    kcs = _split_heads(kc_ref[0])
    vcs = _split_heads(vc_ref[0])
    ind = _stack_pair((_pair_indicator(n_ctx, False), _pair_indicator(n_ctx, True)),
                      (_pair_indicator(n_keys, False), _pair_indicator(n_keys, True)))
    for i in range(NA_STEP_BLOCKS):
        rb = step * NA_STEP_BLOCKS + i
        case = jnp.minimum(rb, 1) + jnp.maximum(rb - (nrb - 2), 0)
        base = jnp.clip(rb * NA_RB - NA_KR // 2, 0, n_rows - NA_SPAN)
        start = pl.multiple_of(base * GRID_W, 256)
        q = q_ref[0, i * tq:(i + 1) * tq, :]
        k_st = _stack_pair(kcs, _split_heads(k_ref[0, pl.ds(start, n_keys), :]))
        v_st = _stack_pair(vcs, _split_heads(v_ref[0, pl.ds(start, n_keys), :]))
        s = _dot_nt(q, k_st)
        p_ctx, p_lat = [], []
        for par in range(2):
            t_c = _tiles(s[:, par * n_ctx:(par + 1) * n_ctx])
            off = 2 * n_ctx + par * n_keys
            t_l = _tiles(s[:, off:off + n_keys] + bias_ref[case, par])
            m = _rowmax128(t_c + t_l)
            p_ctx += _exp_tiles(t_c, m)
            p_lat += _exp_tiles(t_l, m)
        o = _dot(jnp.concatenate(p_ctx + p_lat, axis=1), jnp.concatenate([v_st, ind], axis=1))
        o_ref[0, i * tq:(i + 1) * tq, :] = (o[:, :LANES] / o[:, LANES:]).astype(o_ref.dtype)


def _na_attention(q, k, v, kc, vc, bias):
    b, s, w = q.shape
    n_rows = s // GRID_W
    nrb = n_rows // NA_RB
    tq = NA_STEP_BLOCKS * NA_RB * GRID_W
    n_ctx = kc.shape[1]
    return pl.pallas_call(
        functools.partial(_na_kernel, n_rows=n_rows),
        out_shape=jax.ShapeDtypeStruct((b, s, w), BF16),
        grid=(w // LANES, b, nrb // NA_STEP_BLOCKS),
        in_specs=[pl.BlockSpec((1, tq, LANES), lambda j, bi, r: (bi, r, j)),
                  pl.BlockSpec((1, s, LANES), lambda j, bi, r: (bi, 0, j)),
                  pl.BlockSpec((1, s, LANES), lambda j, bi, r: (bi, 0, j)),
                  pl.BlockSpec((1, n_ctx, LANES), lambda j, bi, r: (bi, 0, j)),
                  pl.BlockSpec((1, n_ctx, LANES), lambda j, bi, r: (bi, 0, j)),
                  pl.BlockSpec((3, 2, NA_RB * GRID_W, NA_SPAN * GRID_W), lambda j, bi, r: (0, j, 0, 0))],
        out_specs=pl.BlockSpec((1, tq, LANES), lambda j, bi, r: (bi, r, j)),
        compiler_params=_cparams(("parallel", "parallel", "arbitrary")),
        name="na_attention",
    )(q, k, v, kc, vc, bias)


def _pair_ctx_kernel(q_ref, kc_ref, vc_ref, o_ref):
    q = q_ref[0]
    n_ctx = kc_ref.shape[1]
    kcs = _split_heads(kc_ref[0])
    vcs = _split_heads(vc_ref[0])
    s = _dot_nt(q, jnp.concatenate(kcs, axis=0))
    p = []
    for par in range(2):
        t_c = _tiles(s[:, par * n_ctx:(par + 1) * n_ctx])
        p += _exp_tiles(t_c, _rowmax128(t_c))
    ind = jnp.concatenate([_pair_indicator(n_ctx, False), _pair_indicator(n_ctx, True)], axis=0)
    o = _dot(jnp.concatenate(p, axis=1), jnp.concatenate([jnp.concatenate(vcs, axis=0), ind], axis=1))
    o_ref[0] = (o[:, :LANES] / o[:, LANES:]).astype(o_ref.dtype)


def _pair_ctx_attention(q, kc, vc):
    b, n, w = q.shape
    spec = pl.BlockSpec((1, n, LANES), lambda bi, j: (bi, 0, j))
    return pl.pallas_call(
        _pair_ctx_kernel,
        out_shape=jax.ShapeDtypeStruct((b, n, w), BF16),
        grid=(b, w // LANES),
        in_specs=[spec, spec, spec],
        out_specs=spec,
        compiler_params=_cparams(("parallel", "parallel")),
        name="na_ctx_attention",
    )(q, kc, vc)


def _na_bias_table(rel_bias, n_rows):
    qc = np.arange(GRID_W)
    cstart = np.clip(qc - NA_KC // 2, 0, GRID_W - NA_KC)
    kc = np.arange(GRID_W)
    col_ok = (kc[None, :] >= cstart[:, None]) & (kc[None, :] < cstart[:, None] + NA_KC)
    padw = GRID_W - NA_KC
    rbp = jnp.pad(rel_bias.astype(F32), ((0, 0), (0, 0), (padw, padw)))
    toe = jnp.stack([rbp[:, :, GRID_W - 1 - c:2 * GRID_W - 1 - c] for c in range(GRID_W)], axis=2)
    toe = jnp.where(jnp.asarray(col_ok)[None, None], toe, NEG_INF)
    masked = jnp.full((rel_bias.shape[0], 1, GRID_W, GRID_W), NEG_INF, F32)
    toe = jnp.concatenate([toe, masked], axis=1)
    n_dr = 2 * NA_KR - 1
    idx = np.full((3, NA_RB, NA_SPAN), n_dr, np.int32)
    nrb = n_rows // NA_RB
    for case, rb in enumerate((0, min(1, nrb - 1), nrb - 1)):
        r0 = rb * NA_RB
        base = int(np.clip(r0 - NA_KR // 2, 0, n_rows - NA_SPAN))
        for a in range(NA_RB):
            r = r0 + a
            rs = int(np.clip(r - NA_KR // 2, 0, n_rows - NA_KR))
            for t in range(NA_SPAN):
                kr = base + t
                if rs <= kr < rs + NA_KR:
                    idx[case, a, t] = kr - r + NA_KR - 1
    nh = rel_bias.shape[0]
    cases = []
    for case in range(3):
        rows = [jnp.concatenate([toe[:, int(idx[case, a, t])] for t in range(NA_SPAN)], axis=-1)
                for a in range(NA_RB)]
        cases.append(jnp.stack(rows, axis=1).reshape(nh, NA_RB * GRID_W, NA_SPAN * GRID_W))
    return jnp.stack(cases, axis=0)


def _ffn_kernel(*refs, n_attn, tf, final):
    h_ref, mod_ref = refs[0], refs[1]
    attn_refs = refs[2:2 + n_attn]
    wout_refs = refs[2 + n_attn:2 + 2 * n_attn]
    wgu_ref, wd_ref = refs[2 + 2 * n_attn:4 + 2 * n_attn]
    pos = 4 + 2 * n_attn
    fn_ref = refs[pos] if final else None
    pos += 1 if final else 0
    o_ref, a2_ref, f_ref = refs[pos:pos + 3]
    d = D_MODEL
    dff = wd_ref.shape[0]

    proj = _dot(attn_refs[0][0], wout_refs[0][...])
    for t in range(1, n_attn):
        proj = proj + _dot(attn_refs[t][0], wout_refs[t][...])
    h1 = h_ref[0] + mod_ref[0, :, 2 * d:3 * d] * proj
    o_ref[0] = h1
    a2_ref[...] = (_rms(h1) * (1.0 + mod_ref[0, :, 4 * d:5 * d]) + mod_ref[0, :, 3 * d:4 * d]).astype(BF16)

    for j in range(dff // tf):
        a2 = a2_ref[...]
        g = _dot(a2, wgu_ref[:, j * tf:(j + 1) * tf])
        u = _dot(a2, wgu_ref[:, dff + j * tf:dff + (j + 1) * tf])
        f_ref[:, j * tf:(j + 1) * tf] = (_silu(g) * u).astype(BF16)

    out = o_ref[0] + mod_ref[0, :, 5 * d:6 * d] * _dot(f_ref[...], wd_ref[...])
    if final:
        out = _rms(out) * fn_ref[...]
    o_ref[0] = out


def _resident(shape):
    return pl.BlockSpec(shape, lambda *_: (0,) * len(shape), pipeline_mode=pl.Buffered(1))


def _ffn_block(h, mods, grp, attns, wouts, wgu, wd, final_norm, tm, tf):
    b, s, d = h.shape
    dff = wd.shape[0]
    n_attn = len(attns)
    final = final_norm is not None
    in_specs = [pl.BlockSpec((1, tm, d), lambda bi, i: (bi, i, 0)),
                pl.BlockSpec((1, 1, mods.shape[2]), lambda bi, i: (grp(bi), 0, 0))]
    in_specs += [pl.BlockSpec((1, tm, a.shape[2]), lambda bi, i: (bi, i, 0)) for a in attns]
    in_specs += [_resident(w.shape) for w in wouts]
    in_specs += [_resident(wgu.shape), _resident(wd.shape)]
    args = [h, mods, *attns, *wouts, wgu, wd]
    if final:
        in_specs.append(_resident((1, d)))
        args.append(final_norm)
    return pl.pallas_call(
        functools.partial(_ffn_kernel, n_attn=n_attn, tf=tf, final=final),
        out_shape=jax.ShapeDtypeStruct((b, s, d), F32),
        grid=(b, s // tm),
        in_specs=in_specs,
        out_specs=pl.BlockSpec((1, tm, d), lambda bi, i: (bi, i, 0)),
        scratch_shapes=[pltpu.VMEM((tm, d), BF16), pltpu.VMEM((tm, dff), BF16)],
        compiler_params=_cparams(("parallel", "parallel")),
        name="outproj_ffn",
    )(*args)


def _inproj1_kernel(x_ref, mod_ref, win_ref, cos_ref, sin_ref, q_ref, k_ref, v_ref):
    d = D_MODEL
    x = x_ref[0]
    a = (_rms(x) * (1.0 + mod_ref[0, :, d:2 * d]) + mod_ref[0, :, 0:d]).astype(BF16)
    p = _dot(a, win_ref[...])
    cos = cos_ref[...]
    sin = sin_ref[...]
    nq = SWA_HEADS * HEAD_DIM
    nkv = 2 * SWA_KV_HEADS * LANES
    quarter = HEAD_DIM // 4
    for t in range(nq // LANES):
        q = _rope_tile(p[:, t * LANES:(t + 1) * LANES], cos, sin, quarter)
        q_ref[0, :, t * LANES:(t + 1) * LANES] = (q * (HEAD_DIM ** -0.5)).astype(BF16)
    for t in range(nkv // LANES):
        k = _rope_tile(p[:, nq + t * LANES:nq + (t + 1) * LANES], cos, sin, quarter)
        k_ref[0, :, t * LANES:(t + 1) * LANES] = k.astype(BF16)
    v_ref[0] = p[:, nq + nkv:nq + 2 * nkv].astype(BF16)


def _inproj1(x, mods, grp, win, cos, sin, tm):
    b, s, d = x.shape
    nq = SWA_HEADS * HEAD_DIM
    nkv = 2 * SWA_KV_HEADS * LANES
    return pl.pallas_call(
        _inproj1_kernel,
        out_shape=(jax.ShapeDtypeStruct((b, s, nq), BF16),
                   jax.ShapeDtypeStruct((b, s, nkv), BF16),
                   jax.ShapeDtypeStruct((b, s, nkv), BF16)),
        grid=(b, s // tm),
        in_specs=[pl.BlockSpec((1, tm, d), lambda bi, i: (bi, i, 0)),
                  pl.BlockSpec((1, 1, mods.shape[2]), lambda bi, i: (grp(bi), 0, 0)),
                  pl.BlockSpec(win.shape, lambda bi, i: (0, 0)),
                  pl.BlockSpec((tm, LANES), lambda bi, i: (i, 0)),
                  pl.BlockSpec((tm, LANES), lambda bi, i: (i, 0))],
        out_specs=(pl.BlockSpec((1, tm, nq), lambda bi, i: (bi, i, 0)),
                   pl.BlockSpec((1, tm, nkv), lambda bi, i: (bi, i, 0)),
                   pl.BlockSpec((1, tm, nkv), lambda bi, i: (bi, i, 0))),
        compiler_params=_cparams(("parallel", "parallel")),
        name="inproj_odd",
    )(x, mods, win, cos, sin)


def _swa_kernel(q_ref, km_ref, k0_ref, kp_ref, vm_ref, v0_ref, vp_ref, kc_ref, vc_ref, sink_ref, o_ref, *, n_steps):
    step = pl.program_id(2)
    blk = SWA_BLOCK
    npair = q_ref.shape[2] // LANES
    kall = jnp.concatenate([km_ref[0], k0_ref[0], kp_ref[0]], axis=0)
    vall = jnp.concatenate([vm_ref[0], v0_ref[0], vp_ref[0]], axis=0)
    kc = kc_ref[0]
    vc = vc_ref[0]
    n_ctx = kc.shape[0]
    kc_st = jnp.concatenate([kc[:, :LANES], kc[:, LANES:]], axis=0)
    vc_st = jnp.concatenate([vc[:, :LANES], vc[:, LANES:]], axis=0)
    ind = _stack_pair((_pair_indicator(n_ctx, False), _pair_indicator(n_ctx, True)),
                      (_pair_indicator(3 * blk, False), _pair_indicator(3 * blk, True)))
    r = lax.broadcasted_iota(jnp.int32, (blk, 3 * blk), 0)
    c = lax.broadcasted_iota(jnp.int32, (blk, 3 * blk), 1)
    cc = c % blk
    lane = lax.broadcasted_iota(jnp.int32, (npair * blk, LANES), 1)
    lo = lane < HEAD_DIM

    def band(left_off, right_off):
        ok = ((c >= blk) & (c < 2 * blk)) | ((c < blk) & (cc >= r + left_off)) | ((c >= 2 * blk) & (cc <= r - right_off))
        neg = jnp.where(ok, 0.0, NEG_INF).astype(F32)
        return jnp.concatenate([neg] * npair, axis=0)

    for t in range(SWA_GROUP):
        left_off = jnp.where(step >= 1, 0, blk) if t == 0 else 0
        right_off = jnp.where(step <= n_steps - 2, 0, blk) if t == SWA_GROUP - 1 else 0
        neg = band(left_off, right_off)
        qb = q_ref[0, t * blk:(t + 1) * blk, :]
        qst = jnp.concatenate([qb[:, i * LANES:(i + 1) * LANES] for i in range(npair)], axis=0)
        kw = kall[t * blk:(t + 3) * blk]
        vw = vall[t * blk:(t + 3) * blk]
        k_st = jnp.concatenate([kc_st, kw[:, :LANES], kw[:, LANES:]], axis=0)
        v_st = jnp.concatenate([vc_st, vw[:, :LANES], vw[:, LANES:]], axis=0)
        s = _dot_nt(qst, k_st)
        p_ctx, p_lat, p_sink = [], [], []
        for par in range(2):
            sink = sink_ref[0, par]
            t_c = _tiles(s[:, par * n_ctx:(par + 1) * n_ctx])
            off = 2 * n_ctx + par * 3 * blk
            t_l = _tiles(s[:, off:off + 3 * blk] + neg)
            m = jnp.maximum(_rowmax128(t_c + t_l), sink)
            p_ctx += _exp_tiles(t_c, m)
            p_lat += _exp_tiles(t_l, m)
            p_sink.append(jnp.exp(sink - m))
        o = _dot(jnp.concatenate(p_ctx + p_lat, axis=1), jnp.concatenate([v_st, ind], axis=1))
        o = o[:, :LANES] / (o[:, LANES:] + jnp.where(lo, p_sink[0], p_sink[1]))
        for i in range(npair):
            o_ref[0, t * blk:(t + 1) * blk, i * LANES:(i + 1) * LANES] = o[i * blk:(i + 1) * blk].astype(o_ref.dtype)


def _swa_attention(q, kvar, vvar, kc, vc, sink_tab):
    b, s, nq = q.shape
    blk = SWA_BLOCK
    nb = s // blk
    n_steps = nb // SWA_GROUP
    gw = nq // SWA_KV_HEADS
    n_ctx = kc.shape[1]
    prev = lambda bi, g, j: (bi, jnp.maximum(SWA_GROUP * j - 1, 0), g)
    cur = lambda bi, g, j: (bi, j, g)
    nxt = lambda bi, g, j: (bi, jnp.minimum(SWA_GROUP * (j + 1), nb - 1), g)
    edge_spec = lambda f: pl.BlockSpec((1, blk, 2 * LANES), f)
    main_spec = pl.BlockSpec((1, SWA_GROUP * blk, 2 * LANES), cur)
    ctx_spec = pl.BlockSpec((1, n_ctx, 2 * LANES), lambda bi, g, j: (bi, 0, g))
    return pl.pallas_call(
        functools.partial(_swa_kernel, n_steps=n_steps),
        out_shape=jax.ShapeDtypeStruct((b, s, nq), BF16),
        grid=(b, SWA_KV_HEADS, n_steps),
        in_specs=[pl.BlockSpec((1, SWA_GROUP * blk, gw), cur),
                  edge_spec(prev), main_spec, edge_spec(nxt),
                  edge_spec(prev), main_spec, edge_spec(nxt),
                  ctx_spec, ctx_spec,
                  pl.BlockSpec((1, 2, sink_tab.shape[2], LANES), lambda bi, g, j: (g, 0, 0, 0))],
        out_specs=pl.BlockSpec((1, SWA_GROUP * blk, gw), cur),
        compiler_params=_cparams(("parallel", "parallel", "arbitrary")),
        name="swa_attention",
    )(q, kvar, kvar, kvar, vvar, vvar, vvar, kc, vc, sink_tab)


def _rope_tables(s, dim, pad_to):
    t = jnp.arange(s, dtype=jnp.int32)
    rows, cols = t // GRID_W, t % GRID_W
    half = dim // 2
    inv = ROPE_BASE ** (-jnp.arange(0, half, 2, dtype=F32) / half)
    ar = rows.astype(F32)[:, None] * inv
    ac = cols.astype(F32)[:, None] * inv
    cos = jnp.concatenate([jnp.cos(ar), jnp.cos(ar), jnp.cos(ac), jnp.cos(ac)], axis=1)
    sin = jnp.concatenate([-jnp.sin(ar), jnp.sin(ar), -jnp.sin(ac), jnp.sin(ac)], axis=1)
    if pad_to > dim:
        cos = jnp.concatenate([cos, jnp.ones((s, pad_to - dim), F32)], axis=1)
        sin = jnp.concatenate([sin, jnp.zeros((s, pad_to - dim), F32)], axis=1)
    reps = LANES // cos.shape[1]
    return jnp.tile(cos, (1, reps)), jnp.tile(sin, (1, reps))


def _even_weights(w_in, w_q_up, w_uk, w_uv, w_out):
    d = w_in.shape[0]
    win = jnp.concatenate([w_in[:, :MLA_IN], jnp.zeros((d, 4 * LANES - MLA_IN), w_in.dtype), w_in[:, MLA_IN:]], axis=1)
    wq3 = w_q_up.reshape(MLA_Q_RANK, MLA_HEADS, MLA_NOPE + MLA_ROPE)
    nope = wq3[:, :, :MLA_NOPE].reshape(MLA_Q_RANK, MLA_HEADS * MLA_NOPE)
    rope = jnp.pad(wq3[:, :, MLA_NOPE:], ((0, 0), (0, 0), (0, LANES - MLA_ROPE))).reshape(MLA_Q_RANK, MLA_HEADS * LANES)
    wq = jnp.concatenate([nope, rope], axis=1)
    eye = jnp.eye(MLA_HEADS, dtype=w_uk.dtype)
    wuk = jnp.einsum('hcn,hg->hngc', w_uk, eye).reshape(MLA_HEADS * MLA_NOPE, MLA_HEADS * MLA_KV_RANK)
    wuv = jnp.einsum('hcv,hg->hcgv', w_uv, eye).reshape(MLA_HEADS * MLA_KV_RANK, MLA_HEADS * MLA_V)
    n_mla = MLA_HEADS * MLA_V
    return (win.astype(BF16), wq.astype(BF16), wuk.astype(BF16), wuv.astype(BF16),
            w_out[:n_mla].astype(BF16), w_out[n_mla:].astype(BF16))


def _odd_weights(w_in):
    d = w_in.shape[0]
    nq = SWA_HEADS * HEAD_DIM
    z = jnp.zeros((d, HEAD_DIM), w_in.dtype)

    def variants(off):
        cols = []
        for g in range(SWA_KV_HEADS):
            w = w_in[:, off + g * HEAD_DIM:off + (g + 1) * HEAD_DIM]
            cols += [w, z, z, w]
        return jnp.concatenate(cols, axis=1)

    kcols = variants(nq)
    vcols = variants(nq + SWA_KV_HEADS * HEAD_DIM)
    return jnp.concatenate([w_in[:, :nq], kcols, vcols], axis=1).astype(BF16)


def _sink_table(sinks):
    g = SWA_HEADS // SWA_KV_HEADS
    t = sinks.astype(F32).reshape(SWA_KV_HEADS, g // 2, 2).transpose(0, 2, 1)
    t = jnp.broadcast_to(t[:, :, :, None, None], (SWA_KV_HEADS, 2, g // 2, SWA_BLOCK, LANES))
    return t.reshape(SWA_KV_HEADS, 2, (g // 2) * SWA_BLOCK, LANES)


def kernel(x, c, ctx, c_ctx, mod_w, mod_b, even_w_in, mla_q_norm, mla_kv_norm, mla_w_q_up, mla_w_uk, mla_w_uv,
           na_rel_bias, even_w_out, odd_w_in, swa_sinks, odd_w_out, ffn_w_gate_up, ffn_w_down, final_norm):
    b, s, d = x.shape
    n_ctx = ctx.shape[1]
    n_rows = s // GRID_W
    assert d == D_MODEL and s % (NA_STEP_BLOCKS * NA_RB * GRID_W) == 0 and s % (SWA_GROUP * SWA_BLOCK) == 0
    assert mod_w.shape[0] == 2 and b <= 4

    ctx_grp = 4
    c8 = jnp.zeros((8, d), F32).at[:b].set(c.astype(F32)).at[ctx_grp].set(c_ctx.astype(F32))
    mods = _modulation(c8, mod_w.astype(F32), mod_b.astype(F32))
    mods0 = mods[0].reshape(8, 1, 6 * d)
    mods1 = mods[1].reshape(8, 1, 6 * d)
    lat_grp = lambda bi: bi
    ctx_g = lambda bi: ctx_grp

    tm = min(512, s)
    tf = 256
    wgu = ffn_w_gate_up.astype(BF16)
    wdn = ffn_w_down.astype(BF16)

    win, wq, wuk, wuv, wo_mla, wo_na = _even_weights(even_w_in[0], mla_w_q_up[0], mla_w_uk[0], mla_w_uv[0],
                                                      even_w_out[0])
    qn = mla_q_norm[0].astype(F32).reshape(1, -1)
    kvn = mla_kv_norm[0].astype(F32).reshape(1, -1)
    cos_m, sin_m = _rope_tables(s, MLA_ROPE, LANES)
    one_c = jnp.ones((n_ctx, LANES), F32)
    zero_c = jnp.zeros((n_ctx, LANES), F32)

    q_l, k_l, naq_l, nak_l, nav_l = _inproj0(x, mods0, lat_grp, win, qn, wq, wuk, kvn, cos_m, sin_m, tm)
    q_c, k_c, naq_c, nak_c, nav_c = _inproj0(ctx, mods0, ctx_g, win, qn, wq, wuk, kvn, one_c, zero_c, n_ctx)

    k_all = jnp.concatenate([k_c, k_l], axis=1)
    kt_all = jnp.swapaxes(k_all, 1, 2)
    o_mla_l = _mla_attention(q_l, kt_all, k_all, wuv, min(1024, s))
    o_mla_c = _mla_attention(q_c, jnp.swapaxes(k_c, 1, 2), k_c, wuv, n_ctx)

    bias = _na_bias_table(na_rel_bias[0], n_rows)
    o_na_l = _na_attention(naq_l, nak_l, nav_l, nak_c, nav_c, bias)
    o_na_c = _pair_ctx_attention(naq_c, nak_c, nav_c)

    h_lat = _ffn_block(x, mods0, lat_grp, [o_mla_l, o_na_l], [wo_mla, wo_na], wgu[0], wdn[0], None, tm, tf)
    h_ctx = _ffn_block(ctx, mods0, ctx_g, [o_mla_c, o_na_c], [wo_mla, wo_na], wgu[0], wdn[0], None, n_ctx, tf)

    win1 = _odd_weights(odd_w_in[0])
    cos_s, sin_s = _rope_tables(s, HEAD_DIM, HEAD_DIM)
    q1, k1, v1 = _inproj1(h_lat, mods1, lat_grp, win1, cos_s, sin_s, tm)
    _, k1c, v1c = _inproj1(h_ctx, mods1, ctx_g, win1, one_c, zero_c, n_ctx)
    o_swa = _swa_attention(q1, k1, v1, k1c, v1c, _sink_table(swa_sinks[0]))
    fn = final_norm.astype(F32).reshape(1, d)
    return _ffn_block(h_lat, mods1, lat_grp, [o_swa], [odd_w_out[0].astype(BF16)], wgu[1], wdn[1], fn, tm, tf)
```

```python
import functools

import numpy as np
import jax
import jax.numpy as jnp
from jax import lax
from jax.experimental import pallas as pl
from jax.experimental.pallas import tpu as pltpu

F32 = jnp.float32
BF16 = jnp.bfloat16

D_MODEL = 1024
GRID_W = 64
HEAD_DIM = 64
ROPE_BASE = 10000.0
EPS = 1e-6
NEG_INF = -1e30
LOG2_E = 1.4426950408889634

MLA_HEADS = 8
MLA_NOPE = 64
MLA_ROPE = 32
MLA_V = 64
MLA_Q_RANK = 256
MLA_KV_RANK = 128
MLA_IN = MLA_Q_RANK + MLA_KV_RANK + MLA_ROPE

NA_HEADS = 8
NA_KR = 8
NA_KC = 16
NA_RB = 8
NA_SPAN = 16
NA_STEP_BLOCKS = 4

SWA_HEADS = 16
SWA_KV_HEADS = 2
SWA_WINDOW = 128
SWA_BLOCK = 128
SWA_GROUP = 4

LANES = 128
MLA_QK_PAD = 256
MLA_ITEM_ROWS = 256
MLA_KEY_CHUNK = 768
ONES_LANE = MLA_ROPE
VMEM_LIMIT = 56 * 1024 * 1024


def _cparams(sem):
    return pltpu.CompilerParams(dimension_semantics=sem, vmem_limit_bytes=VMEM_LIMIT)


def _dot(a, b):
    return jnp.dot(a, b, preferred_element_type=F32)


def _dot_nt(a, b):
    return lax.dot_general(a, b, (((1,), (1,)), ((), ())), preferred_element_type=F32)


def _rms(x):
    return x * lax.rsqrt(jnp.mean(x * x, axis=-1, keepdims=True) + EPS)


def _silu(x):
    return x * (1.0 / (1.0 + jnp.exp(-x)))


def _rope_tile(x, cos, sin, quarter):
    lane = lax.broadcasted_iota(jnp.int32, x.shape, 1)
    first = (lane % (2 * quarter)) < quarter
    swapped = jnp.where(first, pltpu.roll(x, LANES - quarter, 1), pltpu.roll(x, quarter, 1))
    return x * cos + swapped * sin


def _zero_of(x):
    u = lax.bitcast_convert_type(x, jnp.uint32)
    u = lax.shift_right_logical(lax.shift_right_logical(u, jnp.uint32(16)), jnp.uint32(16))
    return lax.bitcast_convert_type(u, F32)


def _tiles(s):
    return [s[:, j * LANES:(j + 1) * LANES] for j in range(s.shape[1] // LANES)]


def _rowmax128(tiles):
    mx = tiles[0]
    for t in tiles[1:]:
        mx = jnp.maximum(mx, t)
    return jnp.broadcast_to(jnp.max(mx, axis=1, keepdims=True), mx.shape)


def _rowmax(s):
    return _rowmax128(_tiles(s))


def _exp_tiles(tiles, m):
    return [jnp.exp2(t - m).astype(BF16) for t in tiles]


def _pair_indicator(n, hi):
    lane = lax.broadcasted_iota(jnp.int32, (n, LANES), 1)
    return jnp.where((lane >= HEAD_DIM) == hi, 1.0, 0.0).astype(BF16)


def _mod_kernel(c_ref, w_ref, b_ref, o_ref):
    a = _silu(c_ref[...])
    o_ref[0] = jnp.dot(a, w_ref[0], precision=lax.Precision.HIGHEST, preferred_element_type=F32) + b_ref[0]


def _modulation(c8, mod_w, mod_b):
    depth, d, n = mod_w.shape
    tn = 1536
    return pl.pallas_call(
        _mod_kernel,
        out_shape=jax.ShapeDtypeStruct((depth, 8, n), F32),
        grid=(depth, n // tn),
        in_specs=[pl.BlockSpec((8, d), lambda l, j: (0, 0)),
                  pl.BlockSpec((1, d, tn), lambda l, j: (l, 0, j)),
                  pl.BlockSpec((1, 1, tn), lambda l, j: (l, 0, j))],
        out_specs=pl.BlockSpec((1, 8, tn), lambda l, j: (l, 0, j)),
        compiler_params=_cparams(("parallel", "parallel")),
        name="modulation",
    )(c8, mod_w, mod_b.reshape(depth, 1, n))


def _inproj0_kernel(x_ref, mod_ref, win_ref, qn_ref, wq_ref, wuk_ref, kvn_ref, cos_ref, sin_ref,
                    qmla_ref, kmla_ref, naq_ref, nak_ref, nav_ref):
    d = D_MODEL
    x = x_ref[0]
    shift = mod_ref[0, :, 0:d]
    scale = mod_ref[0, :, d:2 * d]
    a = (_rms(x) * (1.0 + scale) + shift).astype(BF16)
    p = _dot(a, win_ref[...])
    cos = cos_ref[...]
    sin = sin_ref[...]
    mla_scale = (MLA_NOPE + MLA_ROPE) ** -0.5 * LOG2_E

    cq = (_rms(p[:, 0:MLA_Q_RANK]) * qn_ref[...]).astype(BF16)
    q = _dot(cq, wq_ref[...])
    n_nope = MLA_HEADS * MLA_NOPE
    q_lat = _dot(q[:, 0:n_nope].astype(BF16), wuk_ref[...])
    for h in range(MLA_HEADS):
        qr = _rope_tile(q[:, n_nope + h * LANES:n_nope + (h + 1) * LANES], cos, sin, MLA_ROPE // 4)
        qmla_ref[0, h, :, 0:LANES] = (q_lat[:, h * LANES:(h + 1) * LANES] * mla_scale).astype(BF16)
        qmla_ref[0, h, :, LANES:2 * LANES] = (qr * mla_scale).astype(BF16)

    ckv = _rms(p[:, MLA_Q_RANK:MLA_Q_RANK + MLA_KV_RANK]) * kvn_ref[...]
    kr = _rope_tile(p[:, 3 * LANES:4 * LANES], cos, sin, MLA_ROPE // 4)
    lane = lax.broadcasted_iota(jnp.int32, kr.shape, 1)
    kr = jnp.where(lane == ONES_LANE, 1.0, kr)
    kmla_ref[0, :, 0:LANES] = ckv.astype(BF16)
    kmla_ref[0, :, LANES:2 * LANES] = kr.astype(BF16)

    w = NA_HEADS * HEAD_DIM
    naq_ref[0] = (p[:, 4 * LANES:4 * LANES + w] * (HEAD_DIM ** -0.5 * LOG2_E)).astype(BF16)
    nak_ref[0] = p[:, 4 * LANES + w:4 * LANES + 2 * w].astype(BF16)
    nav_ref[0] = p[:, 4 * LANES + 2 * w:4 * LANES + 3 * w].astype(BF16)


def _inproj0(x, mods, grp, win, qn, wq, wuk, kvn, cos, sin, tm):
    b, s, d = x.shape
    nt = s // tm
    w = NA_HEADS * HEAD_DIM
    const = lambda bi, i: (0, 0)
    return pl.pallas_call(
        _inproj0_kernel,
        out_shape=(jax.ShapeDtypeStruct((b, MLA_HEADS, s, MLA_QK_PAD), BF16),
                   jax.ShapeDtypeStruct((b, s, MLA_QK_PAD), BF16),
                   jax.ShapeDtypeStruct((b, s, w), BF16),
                   jax.ShapeDtypeStruct((b, s, w), BF16),
                   jax.ShapeDtypeStruct((b, s, w), BF16)),
        grid=(b, nt),
        in_specs=[pl.BlockSpec((1, tm, d), lambda bi, i: (bi, i, 0)),
                  pl.BlockSpec((1, 1, mods.shape[2]), lambda bi, i: (grp(bi), 0, 0)),
                  pl.BlockSpec(win.shape, const),
                  pl.BlockSpec(qn.shape, const),
                  pl.BlockSpec(wq.shape, const),
                  pl.BlockSpec(wuk.shape, const),
                  pl.BlockSpec(kvn.shape, const),
                  pl.BlockSpec((tm, LANES), lambda bi, i: (i, 0)),
                  pl.BlockSpec((tm, LANES), lambda bi, i: (i, 0))],
        out_specs=(pl.BlockSpec((1, MLA_HEADS, tm, MLA_QK_PAD), lambda bi, i: (bi, 0, i, 0)),
                   pl.BlockSpec((1, tm, MLA_QK_PAD), lambda bi, i: (bi, i, 0)),
                   pl.BlockSpec((1, tm, w), lambda bi, i: (bi, i, 0)),
                   pl.BlockSpec((1, tm, w), lambda bi, i: (bi, i, 0)),
                   pl.BlockSpec((1, tm, w), lambda bi, i: (bi, i, 0))),
        compiler_params=_cparams(("parallel", "parallel")),
        name="inproj_even",
    )(x, mods, win, qn, wq, wuk, kvn, cos, sin)


def _mla_kernel(q_ref, kt_ref, k_ref, wuv_ref, o_ref, s0_ref, s1_ref, p0_ref, p1_ref, m0_ref, m1_ref, oh_ref):
    n_heads, tq = q_ref.shape[1], q_ref.shape[2]
    rc, nk = s0_ref.shape
    n_items = (tq // rc) * n_heads
    s_refs, p_refs, m_refs = (s0_ref, s1_ref), (p0_ref, p1_ref), (m0_ref, m1_ref)

    def item(i):
        if isinstance(i, int):
            return i % n_heads, (i // n_heads) * rc
        return lax.rem(i, n_heads), pl.multiple_of(lax.div(i, n_heads) * rc, rc)

    tk = MLA_KEY_CHUNK if nk % MLA_KEY_CHUNK == 0 else 2 * LANES

    def round_(r, par):
        static = isinstance(r, int)
        do_qk = not static or r < n_items
        do_sm = not static or 1 <= r <= n_items
        do_pv = not static or 2 <= r <= n_items + 1
        if do_qk:
            h_q, row_q = item(r)
            q = q_ref[0, h_q, pl.ds(row_q, rc), :]
        rg = 32
        if do_pv:
            acc = None
        for c in range(nk // tk):
            cols = slice(c * tk, (c + 1) * tk)
            if do_qk:
                s = _dot(q, kt_ref[0, :, cols])
                s_refs[par][:, cols] = s
                tiles = _tiles(s)
                for j in range(0, len(tiles), 2):
                    pm = jnp.maximum(tiles[j], tiles[j + 1])
                    m_refs[par][...] = pm if c == 0 and j == 0 else jnp.maximum(m_refs[par][...], pm)
            if do_sm:
                zero = jnp.tile(_zero_of(s[0:8, 0:LANES]), (rg // 8, 1)) if do_qk else None
                for g in range(rc // rg):
                    rows = slice(g * rg, (g + 1) * rg)
                    m = m_refs[1 - par][rows, :]
                    m = m + zero if do_qk else m
                    for j in range(c * tk // LANES, (c + 1) * tk // LANES):
                        lanes = slice(j * LANES, (j + 1) * LANES)
                        p_refs[1 - par][rows, lanes] = jnp.exp2(s_refs[1 - par][rows, lanes] - m).astype(BF16)
            if do_pv:
                part = _dot(p_refs[par][:, cols], k_ref[0, cols, :])
                acc = part if acc is None else acc + part
        if do_qk:
            mx = m_refs[par][...]
            m_refs[par][...] = jnp.broadcast_to(jnp.max(mx, axis=1, keepdims=True), mx.shape)
        if do_pv:
            h_v, row_v = item(r - 2)
            tail = acc[:, LANES:2 * LANES]
            lane = lax.broadcasted_iota(jnp.int32, tail.shape, 1)
            l = jnp.sum(jnp.where(lane == ONES_LANE, tail, 0.0), axis=1, keepdims=True)
            oh_ref[h_v, pl.ds(row_v, rc), :] = (acc[:, 0:LANES] / l).astype(BF16)

    round_(0, 0)
    round_(1, 1)
    n_full = max(n_items - 2, 0)

    def pair(i, carry):
        r = 2 + 2 * i
        round_(r, 0)
        round_(r + 1, 1)
        return carry

    lax.fori_loop(0, n_full // 2, pair, 0)
    for r in range(2 + 2 * (n_full // 2), n_items):
        round_(r, r % 2)
    for r in range(max(n_items, 2), n_items + 2):
        round_(r, r % 2)

    o = jnp.concatenate([oh_ref[h] for h in range(n_heads)], axis=1)
    o_ref[0] = _dot(o, wuv_ref[...]).astype(o_ref.dtype)


def _mla_attention(q, kt, k, wuv, tq):
    b, h, sq, _ = q.shape
    nk = k.shape[1]
    rc = min(MLA_ITEM_ROWS, tq)
    assert nk % LANES == 0 and tq % rc == 0 and sq % tq == 0
    wo = wuv.shape[1]
    one_buffer = pl.Buffered(1)
    return pl.pallas_call(
        _mla_kernel,
        out_shape=jax.ShapeDtypeStruct((b, sq, wo), BF16),
        grid=(b, sq // tq),
        in_specs=[pl.BlockSpec((1, h, tq, MLA_QK_PAD), lambda bi, i: (bi, 0, i, 0)),
                  pl.BlockSpec((1, MLA_QK_PAD, nk), lambda bi, i: (bi, 0, 0), pipeline_mode=one_buffer),
                  pl.BlockSpec((1, nk, MLA_QK_PAD), lambda bi, i: (bi, 0, 0), pipeline_mode=one_buffer),
                  _resident(wuv.shape)],
        out_specs=pl.BlockSpec((1, tq, wo), lambda bi, i: (bi, i, 0)),
        scratch_shapes=[pltpu.VMEM((rc, nk), F32), pltpu.VMEM((rc, nk), F32),
                        pltpu.VMEM((rc, nk), BF16), pltpu.VMEM((rc, nk), BF16),
                        pltpu.VMEM((rc, LANES), F32), pltpu.VMEM((rc, LANES), F32),
                        pltpu.VMEM((h, tq, LANES), BF16)],
        compiler_params=_cparams(("parallel", "arbitrary")),
        name="mla_attention",
    )(q, kt, k, wuv)


def _split_heads(t):
    lane = lax.broadcasted_iota(jnp.int32, t.shape, 1)
    lo = lane < HEAD_DIM
    zero = jnp.zeros_like(t)
    return jnp.where(lo, t, zero), jnp.where(lo, zero, t)


def _stack_pair(lo_hi_ctx, lo_hi_lat):
    return jnp.concatenate([lo_hi_ctx[0], lo_hi_ctx[1], lo_hi_lat[0], lo_hi_lat[1]], axis=0)


def _na_kernel(q_ref, k_ref, v_ref, kc_ref, vc_ref, bias_ref, o_ref, *, n_rows):
    step = pl.program_id(2)
    nrb = n_rows // NA_RB
    n_keys = NA_SPAN * GRID_W
    n_ctx = kc_ref.shape[1]
    tq = NA_RB * GRID_W
    kcs = _split_heads(kc_ref[0])
    vcs = _split_heads(vc_ref[0])
    ind = _stack_pair((_pair_indicator(n_ctx, False), _pair_indicator(n_ctx, True)),
                      (_pair_indicator(n_keys, False), _pair_indicator(n_keys, True)))
    for i in range(NA_STEP_BLOCKS):
        rb = step * NA_STEP_BLOCKS + i
        case = jnp.minimum(rb, 1) + jnp.maximum(rb - (nrb - 2), 0)
        base = jnp.clip(rb * NA_RB - NA_KR // 2, 0, n_rows - NA_SPAN)
        start = pl.multiple_of(base * GRID_W, 256)
        q = q_ref[0, i * tq:(i + 1) * tq, :]
        k_st = _stack_pair(kcs, _split_heads(k_ref[0, pl.ds(start, n_keys), :]))
        v_st = _stack_pair(vcs, _split_heads(v_ref[0, pl.ds(start, n_keys), :]))
        s = _dot_nt(q, k_st)
        p_ctx, p_lat = [], []
        for par in range(2):
            t_c = _tiles(s[:, par * n_ctx:(par + 1) * n_ctx])
            off = 2 * n_ctx + par * n_keys
            t_l = _tiles(s[:, off:off + n_keys] + bias_ref[case, par])
            m = _rowmax128(t_c + t_l)
            p_ctx += _exp_tiles(t_c, m)
            p_lat += _exp_tiles(t_l, m)
        o = _dot(jnp.concatenate(p_ctx + p_lat, axis=1), jnp.concatenate([v_st, ind], axis=1))
        o_ref[0, i * tq:(i + 1) * tq, :] = (o[:, :LANES] / o[:, LANES:]).astype(o_ref.dtype)


def _na_attention(q, k, v, kc, vc, bias):
    b, s, w = q.shape
    n_rows = s // GRID_W
    nrb = n_rows // NA_RB
    tq = NA_STEP_BLOCKS * NA_RB * GRID_W
    n_ctx = kc.shape[1]
    return pl.pallas_call(
        functools.partial(_na_kernel, n_rows=n_rows),
        out_shape=jax.ShapeDtypeStruct((b, s, w), BF16),
        grid=(w // LANES, b, nrb // NA_STEP_BLOCKS),
        in_specs=[pl.BlockSpec((1, tq, LANES), lambda j, bi, r: (bi, r, j)),
                  pl.BlockSpec((1, s, LANES), lambda j, bi, r: (bi, 0, j)),
                  pl.BlockSpec((1, s, LANES), lambda j, bi, r: (bi, 0, j)),
                  pl.BlockSpec((1, n_ctx, LANES), lambda j, bi, r: (bi, 0, j)),
                  pl.BlockSpec((1, n_ctx, LANES), lambda j, bi, r: (bi, 0, j)),
                  pl.BlockSpec((3, 2, NA_RB * GRID_W, NA_SPAN * GRID_W), lambda j, bi, r: (0, j, 0, 0))],
        out_specs=pl.BlockSpec((1, tq, LANES), lambda j, bi, r: (bi, r, j)),
        compiler_params=_cparams(("parallel", "parallel", "arbitrary")),
        name="na_attention",
    )(q, k, v, kc, vc, bias)


def _pair_ctx_kernel(q_ref, kc_ref, vc_ref, o_ref):
    q = q_ref[0]
    n_ctx = kc_ref.shape[1]
    kcs = _split_heads(kc_ref[0])
    vcs = _split_heads(vc_ref[0])
    s = _dot_nt(q, jnp.concatenate(kcs, axis=0))
    p = []
    for par in range(2):
        t_c = _tiles(s[:, par * n_ctx:(par + 1) * n_ctx])
        p += _exp_tiles(t_c, _rowmax128(t_c))
    ind = jnp.concatenate([_pair_indicator(n_ctx, False), _pair_indicator(n_ctx, True)], axis=0)
    o = _dot(jnp.concatenate(p, axis=1), jnp.concatenate([jnp.concatenate(vcs, axis=0), ind], axis=1))
    o_ref[0] = (o[:, :LANES] / o[:, LANES:]).astype(o_ref.dtype)


def _pair_ctx_attention(q, kc, vc):
    b, n, w = q.shape
    spec = pl.BlockSpec((1, n, LANES), lambda bi, j: (bi, 0, j))
    return pl.pallas_call(
        _pair_ctx_kernel,
        out_shape=jax.ShapeDtypeStruct((b, n, w), BF16),
        grid=(b, w // LANES),
        in_specs=[spec, spec, spec],
        out_specs=spec,
        compiler_params=_cparams(("parallel", "parallel")),
        name="na_ctx_attention",
    )(q, kc, vc)


def _na_bias_table(rel_bias, n_rows):
    qc = np.arange(GRID_W)
    cstart = np.clip(qc - NA_KC // 2, 0, GRID_W - NA_KC)
    kc = np.arange(GRID_W)
    col_ok = (kc[None, :] >= cstart[:, None]) & (kc[None, :] < cstart[:, None] + NA_KC)
    n_dr, n_dc = 2 * NA_KR - 1, 2 * NA_KC - 1
    dc = kc[None, :] - qc[:, None] + NA_KC - 1
    sel_col = ((dc[None] == np.arange(n_dc)[:, None, None]) & col_ok[None]).astype(np.float32)
    sel_row = np.zeros((3, NA_RB, NA_SPAN, n_dr), np.float32)
    nrb = n_rows // NA_RB
    for case, rb in enumerate((0, min(1, nrb - 1), nrb - 1)):
        r0 = rb * NA_RB
        base = int(np.clip(r0 - NA_KR // 2, 0, n_rows - NA_SPAN))
        for a in range(NA_RB):
            r = r0 + a
            rs = int(np.clip(r - NA_KR // 2, 0, n_rows - NA_KR))
            for t in range(NA_SPAN):
                kr = base + t
                if rs <= kr < rs + NA_KR:
                    sel_row[case, a, t, kr - r + NA_KR - 1] = 1.0
    valid = (sel_row.sum(-1) > 0)[:, None, :, None, :, None] & col_ok[None, None, None, :, None, :]
    tab = jnp.einsum('catd,hdu,uqk->chaqtk', jnp.asarray(sel_row), rel_bias.astype(F32), jnp.asarray(sel_col),
                     precision=lax.Precision.HIGHEST)
    tab = jnp.where(jnp.asarray(valid), tab * LOG2_E, NEG_INF)
    nh = rel_bias.shape[0]
    return tab.reshape(3, nh, NA_RB * GRID_W, NA_SPAN * GRID_W)


def _ffn_kernel(*refs, n_attn, tf, final):
    h_ref, mod_ref = refs[0], refs[1]
    attn_refs = refs[2:2 + n_attn]
    wout_refs = refs[2 + n_attn:2 + 2 * n_attn]
    wgu_ref, wd_ref = refs[2 + 2 * n_attn:4 + 2 * n_attn]
    pos = 4 + 2 * n_attn
    fn_ref = refs[pos] if final else None
    pos += 1 if final else 0
    o_ref, a2_ref, f_ref = refs[pos:pos + 3]
    d = D_MODEL
    dff = wd_ref.shape[0]

    proj = _dot(attn_refs[0][0], wout_refs[0][...])
    for t in range(1, n_attn):
        proj = proj + _dot(attn_refs[t][0], wout_refs[t][...])
    h1 = h_ref[0] + mod_ref[0, :, 2 * d:3 * d] * proj
    o_ref[0] = h1
    a2_ref[...] = (_rms(h1) * (1.0 + mod_ref[0, :, 4 * d:5 * d]) + mod_ref[0, :, 3 * d:4 * d]).astype(BF16)

    for j in range(dff // tf):
        a2 = a2_ref[...]
        g = _dot(a2, wgu_ref[:, j * tf:(j + 1) * tf])
        u = _dot(a2, wgu_ref[:, dff + j * tf:dff + (j + 1) * tf])
        f_ref[:, j * tf:(j + 1) * tf] = (_silu(g) * u).astype(BF16)

    out = o_ref[0] + mod_ref[0, :, 5 * d:6 * d] * _dot(f_ref[...], wd_ref[...])
    if final:
        out = _rms(out) * fn_ref[...]
    o_ref[0] = out


def _resident(shape):
    return pl.BlockSpec(shape, lambda *_: (0,) * len(shape), pipeline_mode=pl.Buffered(1))


def _ffn_block(h, mods, grp, attns, wouts, wgu, wd, final_norm, tm, tf):
    b, s, d = h.shape
    dff = wd.shape[0]
    n_attn = len(attns)
    final = final_norm is not None
    in_specs = [pl.BlockSpec((1, tm, d), lambda bi, i: (bi, i, 0)),
                pl.BlockSpec((1, 1, mods.shape[2]), lambda bi, i: (grp(bi), 0, 0))]
    in_specs += [pl.BlockSpec((1, tm, a.shape[2]), lambda bi, i: (bi, i, 0)) for a in attns]
    in_specs += [_resident(w.shape) for w in wouts]
    in_specs += [_resident(wgu.shape), _resident(wd.shape)]
    args = [h, mods, *attns, *wouts, wgu, wd]
    if final:
        in_specs.append(_resident((1, d)))
        args.append(final_norm)
    return pl.pallas_call(
        functools.partial(_ffn_kernel, n_attn=n_attn, tf=tf, final=final),
        out_shape=jax.ShapeDtypeStruct((b, s, d), F32),
        grid=(b, s // tm),
        in_specs=in_specs,
        out_specs=pl.BlockSpec((1, tm, d), lambda bi, i: (bi, i, 0)),
        scratch_shapes=[pltpu.VMEM((tm, d), BF16), pltpu.VMEM((tm, dff), BF16)],
        compiler_params=_cparams(("parallel", "parallel")),
        name="outproj_ffn",
    )(*args)


def _inproj1_kernel(x_ref, mod_ref, win_ref, cos_ref, sin_ref, q_ref, k_ref, v_ref):
    d = D_MODEL
    x = x_ref[0]
    a = (_rms(x) * (1.0 + mod_ref[0, :, d:2 * d]) + mod_ref[0, :, 0:d]).astype(BF16)
    p = _dot(a, win_ref[...])
    cos = cos_ref[...]
    sin = sin_ref[...]
    nq = SWA_HEADS * HEAD_DIM
    nkv = 2 * SWA_KV_HEADS * LANES
    quarter = HEAD_DIM // 4
    for t in range(nq // LANES):
        q = _rope_tile(p[:, t * LANES:(t + 1) * LANES], cos, sin, quarter)
        q_ref[0, :, t * LANES:(t + 1) * LANES] = (q * (HEAD_DIM ** -0.5 * LOG2_E)).astype(BF16)
    for t in range(nkv // LANES):
        k = _rope_tile(p[:, nq + t * LANES:nq + (t + 1) * LANES], cos, sin, quarter)
        k_ref[0, :, t * LANES:(t + 1) * LANES] = k.astype(BF16)
    v_ref[0] = p[:, nq + nkv:nq + 2 * nkv].astype(BF16)


def _inproj1(x, mods, grp, win, cos, sin, tm):
    b, s, d = x.shape
    nq = SWA_HEADS * HEAD_DIM
    nkv = 2 * SWA_KV_HEADS * LANES
    return pl.pallas_call(
        _inproj1_kernel,
        out_shape=(jax.ShapeDtypeStruct((b, s, nq), BF16),
                   jax.ShapeDtypeStruct((b, s, nkv), BF16),
                   jax.ShapeDtypeStruct((b, s, nkv), BF16)),
        grid=(b, s // tm),
        in_specs=[pl.BlockSpec((1, tm, d), lambda bi, i: (bi, i, 0)),
                  pl.BlockSpec((1, 1, mods.shape[2]), lambda bi, i: (grp(bi), 0, 0)),
                  pl.BlockSpec(win.shape, lambda bi, i: (0, 0)),
                  pl.BlockSpec((tm, LANES), lambda bi, i: (i, 0)),
                  pl.BlockSpec((tm, LANES), lambda bi, i: (i, 0))],
        out_specs=(pl.BlockSpec((1, tm, nq), lambda bi, i: (bi, i, 0)),
                   pl.BlockSpec((1, tm, nkv), lambda bi, i: (bi, i, 0)),
                   pl.BlockSpec((1, tm, nkv), lambda bi, i: (bi, i, 0))),
        compiler_params=_cparams(("parallel", "parallel")),
        name="inproj_odd",
    )(x, mods, win, cos, sin)


def _swa_kernel(q_ref, km_ref, k0_ref, kp_ref, vm_ref, v0_ref, vp_ref, kc_ref, vc_ref, sink_ref, o_ref, *, n_steps):
    step = pl.program_id(2)
    blk = SWA_BLOCK
    npair = q_ref.shape[2] // LANES
    kall = jnp.concatenate([km_ref[0], k0_ref[0], kp_ref[0]], axis=0)
    vall = jnp.concatenate([vm_ref[0], v0_ref[0], vp_ref[0]], axis=0)
    kc = kc_ref[0]
    vc = vc_ref[0]
    n_ctx = kc.shape[0]
    kc_st = jnp.concatenate([kc[:, :LANES], kc[:, LANES:]], axis=0)
    vc_st = jnp.concatenate([vc[:, :LANES], vc[:, LANES:]], axis=0)
    ind = _stack_pair((_pair_indicator(n_ctx, False), _pair_indicator(n_ctx, True)),
                      (_pair_indicator(3 * blk, False), _pair_indicator(3 * blk, True)))
    r = lax.broadcasted_iota(jnp.int32, (blk, 3 * blk), 0)
    c = lax.broadcasted_iota(jnp.int32, (blk, 3 * blk), 1)
    cc = c % blk
    lane = lax.broadcasted_iota(jnp.int32, (npair * blk, LANES), 1)
    lo = lane < HEAD_DIM

    def band(left_off, right_off):
        ok = ((c >= blk) & (c < 2 * blk)) | ((c < blk) & (cc >= r + left_off)) | ((c >= 2 * blk) & (cc <= r - right_off))
        neg = jnp.where(ok, 0.0, NEG_INF).astype(F32)
        return jnp.concatenate([neg] * npair, axis=0)

    for t in range(SWA_GROUP):
        left_off = jnp.where(step >= 1, 0, blk) if t == 0 else 0
        right_off = jnp.where(step <= n_steps - 2, 0, blk) if t == SWA_GROUP - 1 else 0
        neg = band(left_off, right_off)
        qb = q_ref[0, t * blk:(t + 1) * blk, :]
        qst = jnp.concatenate([qb[:, i * LANES:(i + 1) * LANES] for i in range(npair)], axis=0)
        kw = kall[t * blk:(t + 3) * blk]
        vw = vall[t * blk:(t + 3) * blk]
        k_st = jnp.concatenate([kc_st, kw[:, :LANES], kw[:, LANES:]], axis=0)
        v_st = jnp.concatenate([vc_st, vw[:, :LANES], vw[:, LANES:]], axis=0)
        s = _dot_nt(qst, k_st)
        p_ctx, p_lat, p_sink = [], [], []
        for par in range(2):
            sink = sink_ref[0, par]
            t_c = _tiles(s[:, par * n_ctx:(par + 1) * n_ctx])
            off = 2 * n_ctx + par * 3 * blk
            t_l = _tiles(s[:, off:off + 3 * blk] + neg)
            m = jnp.maximum(_rowmax128(t_c + t_l), sink)
            p_ctx += _exp_tiles(t_c, m)
            p_lat += _exp_tiles(t_l, m)
            p_sink.append(jnp.exp2(sink - m))
        o = _dot(jnp.concatenate(p_ctx + p_lat, axis=1), jnp.concatenate([v_st, ind], axis=1))
        o = o[:, :LANES] / (o[:, LANES:] + jnp.where(lo, p_sink[0], p_sink[1]))
        for i in range(npair):
            o_ref[0, t * blk:(t + 1) * blk, i * LANES:(i + 1) * LANES] = o[i * blk:(i + 1) * blk].astype(o_ref.dtype)


def _swa_attention(q, kvar, vvar, kc, vc, sink_tab):
    b, s, nq = q.shape
    blk = SWA_BLOCK
    nb = s // blk
    n_steps = nb // SWA_GROUP
    gw = nq // SWA_KV_HEADS
    n_ctx = kc.shape[1]
    prev = lambda bi, g, j: (bi, jnp.maximum(SWA_GROUP * j - 1, 0), g)
    cur = lambda bi, g, j: (bi, j, g)
    nxt = lambda bi, g, j: (bi, jnp.minimum(SWA_GROUP * (j + 1), nb - 1), g)
    edge_spec = lambda f: pl.BlockSpec((1, blk, 2 * LANES), f)
    main_spec = pl.BlockSpec((1, SWA_GROUP * blk, 2 * LANES), cur)
    ctx_spec = pl.BlockSpec((1, n_ctx, 2 * LANES), lambda bi, g, j: (bi, 0, g))
    return pl.pallas_call(
        functools.partial(_swa_kernel, n_steps=n_steps),
        out_shape=jax.ShapeDtypeStruct((b, s, nq), BF16),
        grid=(b, SWA_KV_HEADS, n_steps),
        in_specs=[pl.BlockSpec((1, SWA_GROUP * blk, gw), cur),
                  edge_spec(prev), main_spec, edge_spec(nxt),
                  edge_spec(prev), main_spec, edge_spec(nxt),
                  ctx_spec, ctx_spec,
                  pl.BlockSpec((1, 2, sink_tab.shape[2], LANES), lambda bi, g, j: (g, 0, 0, 0))],
        out_specs=pl.BlockSpec((1, SWA_GROUP * blk, gw), cur),
        compiler_params=_cparams(("parallel", "parallel", "arbitrary")),
        name="swa_attention",
    )(q, kvar, kvar, kvar, vvar, vvar, vvar, kc, vc, sink_tab)


def _rope_tables(s, dim, pad_to):
    n_rows = s // GRID_W
    half = dim // 2
    inv = ROPE_BASE ** (-jnp.arange(0, half, 2, dtype=F32) / half)
    ar = jnp.arange(n_rows, dtype=jnp.int32).astype(F32)[:, None] * inv
    ac = jnp.arange(GRID_W, dtype=jnp.int32).astype(F32)[:, None] * inv
    expand_r = lambda a: jnp.repeat(a, GRID_W, axis=0)
    expand_c = lambda a: jnp.tile(a, (n_rows, 1))
    cos = jnp.concatenate([expand_r(jnp.cos(ar))] * 2 + [expand_c(jnp.cos(ac))] * 2, axis=1)
    sin = jnp.concatenate([expand_r(-jnp.sin(ar)), expand_r(jnp.sin(ar)),
                           expand_c(-jnp.sin(ac)), expand_c(jnp.sin(ac))], axis=1)
    if pad_to > dim:
        cos = jnp.concatenate([cos, jnp.ones((s, pad_to - dim), F32)], axis=1)
        sin = jnp.concatenate([sin, jnp.zeros((s, pad_to - dim), F32)], axis=1)
    reps = LANES // cos.shape[1]
    return jnp.tile(cos, (1, reps)), jnp.tile(sin, (1, reps))


def _even_weights(w_in, w_q_up, w_uk, w_uv, w_out):
    d = w_in.shape[0]
    win = jnp.concatenate([w_in[:, :MLA_IN], jnp.zeros((d, 4 * LANES - MLA_IN), w_in.dtype), w_in[:, MLA_IN:]], axis=1)
    wq3 = w_q_up.reshape(MLA_Q_RANK, MLA_HEADS, MLA_NOPE + MLA_ROPE)
    nope = wq3[:, :, :MLA_NOPE].reshape(MLA_Q_RANK, MLA_HEADS * MLA_NOPE)
    rope = jnp.pad(wq3[:, :, MLA_NOPE:], ((0, 0), (0, 0), (0, LANES - MLA_ROPE))).reshape(MLA_Q_RANK, MLA_HEADS * LANES)
    wq = jnp.concatenate([nope, rope], axis=1)
    eye = jnp.eye(MLA_HEADS, dtype=w_uk.dtype)
    wuk = jnp.einsum('hcn,hg->hngc', w_uk, eye).reshape(MLA_HEADS * MLA_NOPE, MLA_HEADS * MLA_KV_RANK)
    wuv = jnp.einsum('hcv,hg->hcgv', w_uv, eye).reshape(MLA_HEADS * MLA_KV_RANK, MLA_HEADS * MLA_V)
    n_mla = MLA_HEADS * MLA_V
    return (win.astype(BF16), wq.astype(BF16), wuk.astype(BF16), wuv.astype(BF16),
            w_out[:n_mla].astype(BF16), w_out[n_mla:].astype(BF16))


def _odd_weights(w_in):
    d = w_in.shape[0]
    nq = SWA_HEADS * HEAD_DIM
    z = jnp.zeros((d, HEAD_DIM), w_in.dtype)

    def variants(off):
        cols = []
        for g in range(SWA_KV_HEADS):
            w = w_in[:, off + g * HEAD_DIM:off + (g + 1) * HEAD_DIM]
            cols += [w, z, z, w]
        return jnp.concatenate(cols, axis=1)

    kcols = variants(nq)
    vcols = variants(nq + SWA_KV_HEADS * HEAD_DIM)
    return jnp.concatenate([w_in[:, :nq], kcols, vcols], axis=1).astype(BF16)


def _sink_table(sinks):
    g = SWA_HEADS // SWA_KV_HEADS
    t = (sinks.astype(F32) * LOG2_E).reshape(SWA_KV_HEADS, g // 2, 2).transpose(0, 2, 1)
    t = jnp.broadcast_to(t[:, :, :, None, None], (SWA_KV_HEADS, 2, g // 2, SWA_BLOCK, LANES))
    return t.reshape(SWA_KV_HEADS, 2, (g // 2) * SWA_BLOCK, LANES)


def kernel(x, c, ctx, c_ctx, mod_w, mod_b, even_w_in, mla_q_norm, mla_kv_norm, mla_w_q_up, mla_w_uk, mla_w_uv,
           na_rel_bias, even_w_out, odd_w_in, swa_sinks, odd_w_out, ffn_w_gate_up, ffn_w_down, final_norm):
    b, s, d = x.shape
    n_ctx = ctx.shape[1]
    n_rows = s // GRID_W
    assert d == D_MODEL and s % (NA_STEP_BLOCKS * NA_RB * GRID_W) == 0 and s % (SWA_GROUP * SWA_BLOCK) == 0
    assert mod_w.shape[0] == 2 and b <= 4

    ctx_grp = 4
    c8 = jnp.zeros((8, d), F32).at[:b].set(c.astype(F32)).at[ctx_grp].set(c_ctx.astype(F32))
    mods = _modulation(c8, mod_w.astype(F32), mod_b.astype(F32))
    mods0 = mods[0].reshape(8, 1, 6 * d)
    mods1 = mods[1].reshape(8, 1, 6 * d)
    lat_grp = lambda bi: bi
    ctx_g = lambda bi: ctx_grp

    tm = min(512, s)
    tf = 256
    wgu = ffn_w_gate_up.astype(BF16)
    wdn = ffn_w_down.astype(BF16)

    win, wq, wuk, wuv, wo_mla, wo_na = _even_weights(even_w_in[0], mla_w_q_up[0], mla_w_uk[0], mla_w_uv[0],
                                                      even_w_out[0])
    qn = mla_q_norm[0].astype(F32).reshape(1, -1)
    kvn = mla_kv_norm[0].astype(F32).reshape(1, -1)
    cos_m, sin_m = _rope_tables(s, MLA_ROPE, LANES)
    one_c = jnp.ones((n_ctx, LANES), F32)
    zero_c = jnp.zeros((n_ctx, LANES), F32)

    q_l, k_l, naq_l, nak_l, nav_l = _inproj0(x, mods0, lat_grp, win, qn, wq, wuk, kvn, cos_m, sin_m, tm)
    q_c, k_c, naq_c, nak_c, nav_c = _inproj0(ctx, mods0, ctx_g, win, qn, wq, wuk, kvn, one_c, zero_c, n_ctx)

    k_all = jnp.concatenate([k_c, k_l], axis=1)
    kt_all = jnp.swapaxes(k_all, 1, 2)
    o_mla_l = _mla_attention(q_l, kt_all, k_all, wuv, min(1024, s))
    o_mla_c = _mla_attention(q_c, jnp.swapaxes(k_c, 1, 2), k_c, wuv, n_ctx)

    bias = _na_bias_table(na_rel_bias[0], n_rows)
    o_na_l = _na_attention(naq_l, nak_l, nav_l, nak_c, nav_c, bias)
    o_na_c = _pair_ctx_attention(naq_c, nak_c, nav_c)

    h_lat = _ffn_block(x, mods0, lat_grp, [o_mla_l, o_na_l], [wo_mla, wo_na], wgu[0], wdn[0], None, tm, tf)
    h_ctx = _ffn_block(ctx, mods0, ctx_g, [o_mla_c, o_na_c], [wo_mla, wo_na], wgu[0], wdn[0], None, n_ctx, tf)

    win1 = _odd_weights(odd_w_in[0])
    cos_s, sin_s = _rope_tables(s, HEAD_DIM, HEAD_DIM)
    q1, k1, v1 = _inproj1(h_lat, mods1, lat_grp, win1, cos_s, sin_s, tm)
    _, k1c, v1c = _inproj1(h_ctx, mods1, ctx_g, win1, one_c, zero_c, n_ctx)
    o_swa = _swa_attention(q1, k1, v1, k1c, v1c, _sink_table(swa_sinks[0]))
    fn = final_norm.astype(F32).reshape(1, d)
    return _ffn_block(h_lat, mods1, lat_grp, [o_swa], [odd_w_out[0].astype(BF16)], wgu[1], wdn[1], fn, tm, tf)
```

```python
import functools

import numpy as np
import jax
import jax.numpy as jnp
from jax import lax
from jax.experimental import pallas as pl
from jax.experimental.pallas import tpu as pltpu

F32 = jnp.float32
BF16 = jnp.bfloat16

D_MODEL = 1024
GRID_W = 64
HEAD_DIM = 64
ROPE_BASE = 10000.0
EPS = 1e-6
NEG_INF = -1e30
LOG2_E = 1.4426950408889634

MLA_HEADS = 8
MLA_NOPE = 64
MLA_ROPE = 32
MLA_V = 64
MLA_Q_RANK = 256
MLA_KV_RANK = 128
MLA_IN = MLA_Q_RANK + MLA_KV_RANK + MLA_ROPE

NA_HEADS = 8
NA_KR = 8
NA_KC = 16
NA_RB = 4
NA_SPAN = 12
NA_STEP_BLOCKS = 8

SWA_HEADS = 16
SWA_KV_HEADS = 2
SWA_WINDOW = 128
SWA_BLOCK = 128
SWA_GROUP = 8

LANES = 128
MLA_QK_PAD = 256
MLA_ITEM_ROWS = 256
MLA_KEY_CHUNK = 768
ONES_LANE = MLA_ROPE
VMEM_LIMIT = 56 * 1024 * 1024


def _cparams(sem):
    return pltpu.CompilerParams(dimension_semantics=sem, vmem_limit_bytes=VMEM_LIMIT)


def _dot(a, b):
    return jnp.dot(a, b, preferred_element_type=F32)


def _dot_nt(a, b):
    return lax.dot_general(a, b, (((1,), (1,)), ((), ())), preferred_element_type=F32)


def _rms(x):
    return x * lax.rsqrt(jnp.mean(x * x, axis=-1, keepdims=True) + EPS)


def _silu(x):
    return x * (1.0 / (1.0 + jnp.exp(-x)))


def _rope_tile(x, cos, sin, quarter):
    lane = lax.broadcasted_iota(jnp.int32, x.shape, 1)
    first = (lane % (2 * quarter)) < quarter
    swapped = jnp.where(first, pltpu.roll(x, LANES - quarter, 1), pltpu.roll(x, quarter, 1))
    return x * cos + swapped * sin


def _zero_of(x):
    u = lax.bitcast_convert_type(x, jnp.uint32)
    u = lax.shift_right_logical(lax.shift_right_logical(u, jnp.uint32(16)), jnp.uint32(16))
    return lax.bitcast_convert_type(u, F32)


def _tiles(s):
    return [s[:, j * LANES:(j + 1) * LANES] for j in range(s.shape[1] // LANES)]


def _rowmax128(tiles):
    mx = tiles[0]
    for t in tiles[1:]:
        mx = jnp.maximum(mx, t)
    return jnp.broadcast_to(jnp.max(mx, axis=1, keepdims=True), mx.shape)


def _rowmax(s):
    return _rowmax128(_tiles(s))


def _exp_tiles(tiles, m):
    return [jnp.exp2(t - m).astype(BF16) for t in tiles]


def _pair_indicator(n, hi):
    lane = lax.broadcasted_iota(jnp.int32, (n, LANES), 1)
    return jnp.where((lane >= HEAD_DIM) == hi, 1.0, 0.0).astype(BF16)


def _mod_kernel(c_ref, w_ref, b_ref, o_ref):
    a = _silu(c_ref[...])
    o_ref[0] = jnp.dot(a, w_ref[0], precision=lax.Precision.HIGHEST, preferred_element_type=F32) + b_ref[0]


def _modulation(c8, mod_w, mod_b):
    depth, d, n = mod_w.shape
    tn = 1536
    return pl.pallas_call(
        _mod_kernel,
        out_shape=jax.ShapeDtypeStruct((depth, 8, n), F32),
        grid=(depth, n // tn),
        in_specs=[pl.BlockSpec((8, d), lambda l, j: (0, 0)),
                  pl.BlockSpec((1, d, tn), lambda l, j: (l, 0, j)),
                  pl.BlockSpec((1, 1, tn), lambda l, j: (l, 0, j))],
        out_specs=pl.BlockSpec((1, 8, tn), lambda l, j: (l, 0, j)),
        compiler_params=_cparams(("parallel", "parallel")),
        name="modulation",
    )(c8, mod_w, mod_b.reshape(depth, 1, n))


def _inproj0_kernel(x_ref, mod_ref, win_ref, qn_ref, wq_ref, wuk_ref, kvn_ref, cos_ref, sin_ref,
                    qmla_ref, kmla_ref, naq_ref, nak_ref, nav_ref):
    d = D_MODEL
    x = x_ref[0]
    shift = mod_ref[0, :, 0:d]
    scale = mod_ref[0, :, d:2 * d]
    a = (_rms(x) * (1.0 + scale) + shift).astype(BF16)
    p = _dot(a, win_ref[...])
    cos = cos_ref[...]
    sin = sin_ref[...]
    mla_scale = (MLA_NOPE + MLA_ROPE) ** -0.5 * LOG2_E

    cq = (_rms(p[:, 0:MLA_Q_RANK]) * qn_ref[...]).astype(BF16)
    q = _dot(cq, wq_ref[...])
    n_nope = MLA_HEADS * MLA_NOPE
    q_lat = _dot(q[:, 0:n_nope].astype(BF16), wuk_ref[...])
    for h in range(MLA_HEADS):
        qr = _rope_tile(q[:, n_nope + h * LANES:n_nope + (h + 1) * LANES], cos, sin, MLA_ROPE // 4)
        qmla_ref[0, h, :, 0:LANES] = (q_lat[:, h * LANES:(h + 1) * LANES] * mla_scale).astype(BF16)
        qmla_ref[0, h, :, LANES:2 * LANES] = (qr * mla_scale).astype(BF16)

    ckv = _rms(p[:, MLA_Q_RANK:MLA_Q_RANK + MLA_KV_RANK]) * kvn_ref[...]
    kr = _rope_tile(p[:, 3 * LANES:4 * LANES], cos, sin, MLA_ROPE // 4)
    lane = lax.broadcasted_iota(jnp.int32, kr.shape, 1)
    kr = jnp.where(lane == ONES_LANE, 1.0, kr)
    kmla_ref[0, :, 0:LANES] = ckv.astype(BF16)
    kmla_ref[0, :, LANES:2 * LANES] = kr.astype(BF16)

    w = NA_HEADS * HEAD_DIM
    naq_ref[0] = (p[:, 4 * LANES:4 * LANES + w] * (HEAD_DIM ** -0.5 * LOG2_E)).astype(BF16)
    nak_ref[0] = p[:, 4 * LANES + w:4 * LANES + 2 * w].astype(BF16)
    nav_ref[0] = p[:, 4 * LANES + 2 * w:4 * LANES + 3 * w].astype(BF16)


def _inproj0(x, mods, grp, win, qn, wq, wuk, kvn, cos, sin, tm):
    b, s, d = x.shape
    nt = s // tm
    w = NA_HEADS * HEAD_DIM
    const = lambda bi, i: (0, 0)
    return pl.pallas_call(
        _inproj0_kernel,
        out_shape=(jax.ShapeDtypeStruct((b, MLA_HEADS, s, MLA_QK_PAD), BF16),
                   jax.ShapeDtypeStruct((b, s, MLA_QK_PAD), BF16),
                   jax.ShapeDtypeStruct((b, s, w), BF16),
                   jax.ShapeDtypeStruct((b, s, w), BF16),
                   jax.ShapeDtypeStruct((b, s, w), BF16)),
        grid=(b, nt),
        in_specs=[pl.BlockSpec((1, tm, d), lambda bi, i: (bi, i, 0)),
                  pl.BlockSpec((1, 1, mods.shape[2]), lambda bi, i: (grp(bi), 0, 0)),
                  pl.BlockSpec(win.shape, const),
                  pl.BlockSpec(qn.shape, const),
                  pl.BlockSpec(wq.shape, const),
                  pl.BlockSpec(wuk.shape, const),
                  pl.BlockSpec(kvn.shape, const),
                  pl.BlockSpec((tm, LANES), lambda bi, i: (i, 0)),
                  pl.BlockSpec((tm, LANES), lambda bi, i: (i, 0))],
        out_specs=(pl.BlockSpec((1, MLA_HEADS, tm, MLA_QK_PAD), lambda bi, i: (bi, 0, i, 0)),
                   pl.BlockSpec((1, tm, MLA_QK_PAD), lambda bi, i: (bi, i, 0)),
                   pl.BlockSpec((1, tm, w), lambda bi, i: (bi, i, 0)),
                   pl.BlockSpec((1, tm, w), lambda bi, i: (bi, i, 0)),
                   pl.BlockSpec((1, tm, w), lambda bi, i: (bi, i, 0))),
        compiler_params=_cparams(("parallel", "parallel")),
        name="inproj_even",
    )(x, mods, win, qn, wq, wuk, kvn, cos, sin)


def _mla_kernel(q_ref, kt_ref, k_ref, wuv_ref, o_ref, s0_ref, s1_ref, p0_ref, p1_ref, m0_ref, m1_ref, oh_ref):
    n_heads, tq = q_ref.shape[1], q_ref.shape[2]
    rc, nk = s0_ref.shape
    n_items = (tq // rc) * n_heads
    s_refs, p_refs, m_refs = (s0_ref, s1_ref), (p0_ref, p1_ref), (m0_ref, m1_ref)

    def item(i):
        if isinstance(i, int):
            return i % n_heads, (i // n_heads) * rc
        return lax.rem(i, n_heads), pl.multiple_of(lax.div(i, n_heads) * rc, rc)

    tk = MLA_KEY_CHUNK if nk % MLA_KEY_CHUNK == 0 else 2 * LANES

    def round_(r, par):
        static = isinstance(r, int)
        do_qk = not static or r < n_items
        do_sm = not static or 1 <= r <= n_items
        do_pv = not static or 2 <= r <= n_items + 1
        if do_qk:
            h_q, row_q = item(r)
            q = q_ref[0, h_q, pl.ds(row_q, rc), :]
        rg = 32
        if do_pv:
            acc = None
        for c in range(nk // tk):
            cols = slice(c * tk, (c + 1) * tk)
            if do_qk:
                s = _dot(q, kt_ref[0, :, cols])
                s_refs[par][:, cols] = s
                tiles = _tiles(s)
                for j in range(0, len(tiles), 2):
                    pm = jnp.maximum(tiles[j], tiles[j + 1])
                    m_refs[par][...] = pm if c == 0 and j == 0 else jnp.maximum(m_refs[par][...], pm)
            if do_sm:
                zero = jnp.tile(_zero_of(s[0:8, 0:LANES]), (rg // 8, 1)) if do_qk else None
                for g in range(rc // rg):
                    rows = slice(g * rg, (g + 1) * rg)
                    m = m_refs[1 - par][rows, :]
                    m = m + zero if do_qk else m
                    for j in range(c * tk // LANES, (c + 1) * tk // LANES):
                        lanes = slice(j * LANES, (j + 1) * LANES)
                        p_refs[1 - par][rows, lanes] = jnp.exp2(s_refs[1 - par][rows, lanes] - m).astype(BF16)
            if do_pv:
                part = _dot(p_refs[par][:, cols], k_ref[0, cols, :])
                acc = part if acc is None else acc + part
        if do_qk:
            mx = m_refs[par][...]
            m_refs[par][...] = jnp.broadcast_to(jnp.max(mx, axis=1, keepdims=True), mx.shape)
        if do_pv:
            h_v, row_v = item(r - 2)
            tail = acc[:, LANES:2 * LANES]
            lane = lax.broadcasted_iota(jnp.int32, tail.shape, 1)
            l = jnp.sum(jnp.where(lane == ONES_LANE, tail, 0.0), axis=1, keepdims=True)
            oh_ref[h_v, pl.ds(row_v, rc), :] = (acc[:, 0:LANES] / l).astype(BF16)

    round_(0, 0)
    round_(1, 1)
    n_full = max(n_items - 2, 0)

    def pair(i, carry):
        r = 2 + 2 * i
        round_(r, 0)
        round_(r + 1, 1)
        return carry

    lax.fori_loop(0, n_full // 2, pair, 0)
    for r in range(2 + 2 * (n_full // 2), n_items):
        round_(r, r % 2)
    for r in range(max(n_items, 2), n_items + 2):
        round_(r, r % 2)

    o = jnp.concatenate([oh_ref[h] for h in range(n_heads)], axis=1)
    o_ref[0] = _dot(o, wuv_ref[...]).astype(o_ref.dtype)


def _mla_attention(q, kt, k, wuv, tq):
    b, h, sq, _ = q.shape
    nk = k.shape[1]
    rc = min(MLA_ITEM_ROWS, tq)
    assert nk % LANES == 0 and tq % rc == 0 and sq % tq == 0
    wo = wuv.shape[1]
    one_buffer = pl.Buffered(1)
    return pl.pallas_call(
        _mla_kernel,
        out_shape=jax.ShapeDtypeStruct((b, sq, wo), BF16),
        grid=(b, sq // tq),
        in_specs=[pl.BlockSpec((1, h, tq, MLA_QK_PAD), lambda bi, i: (bi, 0, i, 0)),
                  pl.BlockSpec((1, MLA_QK_PAD, nk), lambda bi, i: (bi, 0, 0), pipeline_mode=one_buffer),
                  pl.BlockSpec((1, nk, MLA_QK_PAD), lambda bi, i: (bi, 0, 0), pipeline_mode=one_buffer),
                  _resident(wuv.shape)],
        out_specs=pl.BlockSpec((1, tq, wo), lambda bi, i: (bi, i, 0)),
        scratch_shapes=[pltpu.VMEM((rc, nk), F32), pltpu.VMEM((rc, nk), F32),
                        pltpu.VMEM((rc, nk), BF16), pltpu.VMEM((rc, nk), BF16),
                        pltpu.VMEM((rc, LANES), F32), pltpu.VMEM((rc, LANES), F32),
                        pltpu.VMEM((h, tq, LANES), BF16)],
        compiler_params=_cparams(("parallel", "arbitrary")),
        name="mla_attention",
    )(q, kt, k, wuv)


def _split_heads(t):
    lane = lax.broadcasted_iota(jnp.int32, t.shape, 1)
    lo = lane < HEAD_DIM
    zero = jnp.zeros_like(t)
    return jnp.where(lo, t, zero), jnp.where(lo, zero, t)


def _stack_pair(lo_hi_ctx, lo_hi_lat):
    return jnp.concatenate([lo_hi_ctx[0], lo_hi_ctx[1], lo_hi_lat[0], lo_hi_lat[1]], axis=0)


def _na_kernel(q_ref, k_ref, v_ref, kc_ref, vc_ref, bias_ref, o_ref, *, n_rows):
    step = pl.program_id(2)
    nrb = n_rows // NA_RB
    n_keys = NA_SPAN * GRID_W
    n_ctx = kc_ref.shape[1]
    tq = NA_RB * GRID_W
    kcs = _split_heads(kc_ref[0])
    vcs = _split_heads(vc_ref[0])
    ind = _stack_pair((_pair_indicator(n_ctx, False), _pair_indicator(n_ctx, True)),
                      (_pair_indicator(n_keys, False), _pair_indicator(n_keys, True)))
    for i in range(NA_STEP_BLOCKS):
        rb = step * NA_STEP_BLOCKS + i
        case = jnp.minimum(rb, 1) + jnp.maximum(rb - (nrb - 2), 0)
        base = jnp.clip(rb * NA_RB - NA_KR // 2, 0, n_rows - NA_SPAN)
        start = pl.multiple_of(base * GRID_W, 256)
        q = q_ref[0, i * tq:(i + 1) * tq, :]
        k_st = _stack_pair(kcs, _split_heads(k_ref[0, pl.ds(start, n_keys), :]))
        v_st = _stack_pair(vcs, _split_heads(v_ref[0, pl.ds(start, n_keys), :]))
        s = _dot_nt(q, k_st)
        p_ctx, p_lat = [], []
        for par in range(2):
            t_c = _tiles(s[:, par * n_ctx:(par + 1) * n_ctx])
            off = 2 * n_ctx + par * n_keys
            t_l = _tiles(s[:, off:off + n_keys] + bias_ref[case, par])
            m = _rowmax128(t_c + t_l)
            p_ctx += _exp_tiles(t_c, m)
            p_lat += _exp_tiles(t_l, m)
        o = _dot(jnp.concatenate(p_ctx + p_lat, axis=1), jnp.concatenate([v_st, ind], axis=1))
        o_ref[0, i * tq:(i + 1) * tq, :] = (o[:, :LANES] / o[:, LANES:]).astype(o_ref.dtype)


def _na_attention(q, k, v, kc, vc, bias):
    b, s, w = q.shape
    n_rows = s // GRID_W
    nrb = n_rows // NA_RB
    tq = NA_STEP_BLOCKS * NA_RB * GRID_W
    n_ctx = kc.shape[1]
    return pl.pallas_call(
        functools.partial(_na_kernel, n_rows=n_rows),
        out_shape=jax.ShapeDtypeStruct((b, s, w), BF16),
        grid=(w // LANES, b, nrb // NA_STEP_BLOCKS),
        in_specs=[pl.BlockSpec((1, tq, LANES), lambda j, bi, r: (bi, r, j)),
                  pl.BlockSpec((1, s, LANES), lambda j, bi, r: (bi, 0, j)),
                  pl.BlockSpec((1, s, LANES), lambda j, bi, r: (bi, 0, j)),
                  pl.BlockSpec((1, n_ctx, LANES), lambda j, bi, r: (bi, 0, j)),
                  pl.BlockSpec((1, n_ctx, LANES), lambda j, bi, r: (bi, 0, j)),
                  pl.BlockSpec((3, 2, NA_RB * GRID_W, NA_SPAN * GRID_W), lambda j, bi, r: (0, j, 0, 0))],
        out_specs=pl.BlockSpec((1, tq, LANES), lambda j, bi, r: (bi, r, j)),
        compiler_params=_cparams(("parallel", "parallel", "arbitrary")),
        name="na_attention",
    )(q, k, v, kc, vc, bias)


def _pair_ctx_kernel(q_ref, kc_ref, vc_ref, o_ref):
    q = q_ref[0]
    n_ctx = kc_ref.shape[1]
    kcs = _split_heads(kc_ref[0])
    vcs = _split_heads(vc_ref[0])
    s = _dot_nt(q, jnp.concatenate(kcs, axis=0))
    p = []
    for par in range(2):
        t_c = _tiles(s[:, par * n_ctx:(par + 1) * n_ctx])
        p += _exp_tiles(t_c, _rowmax128(t_c))
    ind = jnp.concatenate([_pair_indicator(n_ctx, False), _pair_indicator(n_ctx, True)], axis=0)
    o = _dot(jnp.concatenate(p, axis=1), jnp.concatenate([jnp.concatenate(vcs, axis=0), ind], axis=1))
    o_ref[0] = (o[:, :LANES] / o[:, LANES:]).astype(o_ref.dtype)


def _pair_ctx_attention(q, kc, vc):
    b, n, w = q.shape
    spec = pl.BlockSpec((1, n, LANES), lambda bi, j: (bi, 0, j))
    return pl.pallas_call(
        _pair_ctx_kernel,
        out_shape=jax.ShapeDtypeStruct((b, n, w), BF16),
        grid=(b, w // LANES),
        in_specs=[spec, spec, spec],
        out_specs=spec,
        compiler_params=_cparams(("parallel", "parallel")),
        name="na_ctx_attention",
    )(q, kc, vc)


def _na_bias_table(rel_bias, n_rows):
    qc = np.arange(GRID_W)
    cstart = np.clip(qc - NA_KC // 2, 0, GRID_W - NA_KC)
    kc = np.arange(GRID_W)
    col_ok = (kc[None, :] >= cstart[:, None]) & (kc[None, :] < cstart[:, None] + NA_KC)
    n_dr, n_dc = 2 * NA_KR - 1, 2 * NA_KC - 1
    nh = rel_bias.shape[0]
    dc = kc[None, :] - qc[:, None] + NA_KC - 1
    sel_col = ((dc[None] == np.arange(n_dc)[:, None, None]) & col_ok[None]).astype(np.float32)
    toe = jnp.einsum('hdu,uqk->hdqk', rel_bias.astype(F32), jnp.asarray(sel_col), precision=lax.Precision.HIGHEST)
    toe = jnp.where(jnp.asarray(col_ok)[None, None], toe * LOG2_E, NEG_INF)
    toe = jnp.concatenate([toe, jnp.full((nh, 1, GRID_W, GRID_W), NEG_INF, F32)], axis=1)
    toe = jnp.concatenate([toe, toe], axis=-1)
    idx = np.full((3, NA_RB, NA_SPAN), n_dr, np.int32)
    nrb = n_rows // NA_RB
    for case, rb in enumerate((0, min(1, nrb - 1), nrb - 1)):
        r0 = rb * NA_RB
        base = int(np.clip(r0 - NA_KR // 2, 0, n_rows - NA_SPAN))
        for a in range(NA_RB):
            r = r0 + a
            rs = int(np.clip(r - NA_KR // 2, 0, n_rows - NA_KR))
            for t in range(NA_SPAN):
                kr = base + t
                if rs <= kr < rs + NA_KR:
                    idx[case, a, t] = kr - r + NA_KR - 1

    def assemble(toe_ref, o_ref):
        lo = lax.broadcasted_iota(jnp.int32, (GRID_W, LANES), 1) < GRID_W
        for case in range(3):
            for a in range(NA_RB):
                for tp in range(NA_SPAN // 2):
                    d0, d1 = int(idx[case, a, 2 * tp]), int(idx[case, a, 2 * tp + 1])
                    tile = toe_ref[0, d0] if d0 == d1 else jnp.where(lo, toe_ref[0, d0], toe_ref[0, d1])
                    o_ref[case, 0, a * GRID_W:(a + 1) * GRID_W, tp * LANES:(tp + 1) * LANES] = tile

    tq, tkeys = NA_RB * GRID_W, NA_SPAN * GRID_W
    return pl.pallas_call(
        assemble,
        out_shape=jax.ShapeDtypeStruct((3, nh, tq, tkeys), F32),
        grid=(nh,),
        in_specs=[pl.BlockSpec((1, n_dr + 1, GRID_W, LANES), lambda h: (h, 0, 0, 0))],
        out_specs=pl.BlockSpec((3, 1, tq, tkeys), lambda h: (0, h, 0, 0)),
        compiler_params=_cparams(("parallel",)),
        name="na_bias_table",
    )(toe)


def _ffn_kernel(*refs, n_attn, tf, final):
    h_ref, mod_ref = refs[0], refs[1]
    attn_refs = refs[2:2 + n_attn]
    wout_refs = refs[2 + n_attn:2 + 2 * n_attn]
    wgu_ref, wd_ref = refs[2 + 2 * n_attn:4 + 2 * n_attn]
    pos = 4 + 2 * n_attn
    fn_ref = refs[pos] if final else None
    pos += 1 if final else 0
    o_ref, a2_ref, f_ref = refs[pos:pos + 3]
    d = D_MODEL
    dff = wd_ref.shape[0]

    proj = _dot(attn_refs[0][0], wout_refs[0][...])
    for t in range(1, n_attn):
        proj = proj + _dot(attn_refs[t][0], wout_refs[t][...])
    h1 = h_ref[0] + mod_ref[0, :, 2 * d:3 * d] * proj
    o_ref[0] = h1
    a2_ref[...] = (_rms(h1) * (1.0 + mod_ref[0, :, 4 * d:5 * d]) + mod_ref[0, :, 3 * d:4 * d]).astype(BF16)

    for j in range(dff // tf):
        a2 = a2_ref[...]
        g = _dot(a2, wgu_ref[:, j * tf:(j + 1) * tf])
        u = _dot(a2, wgu_ref[:, dff + j * tf:dff + (j + 1) * tf])
        f_ref[:, j * tf:(j + 1) * tf] = (_silu(g) * u).astype(BF16)

    out = o_ref[0] + mod_ref[0, :, 5 * d:6 * d] * _dot(f_ref[...], wd_ref[...])
    if final:
        out = _rms(out) * fn_ref[...]
    o_ref[0] = out


def _resident(shape):
    return pl.BlockSpec(shape, lambda *_: (0,) * len(shape), pipeline_mode=pl.Buffered(1))


def _ffn_block(h, mods, grp, attns, wouts, wgu, wd, final_norm, tm, tf):
    b, s, d = h.shape
    dff = wd.shape[0]
    n_attn = len(attns)
    final = final_norm is not None
    in_specs = [pl.BlockSpec((1, tm, d), lambda bi, i: (bi, i, 0)),
                pl.BlockSpec((1, 1, mods.shape[2]), lambda bi, i: (grp(bi), 0, 0))]
    in_specs += [pl.BlockSpec((1, tm, a.shape[2]), lambda bi, i: (bi, i, 0)) for a in attns]
    in_specs += [_resident(w.shape) for w in wouts]
    in_specs += [_resident(wgu.shape), _resident(wd.shape)]
    args = [h, mods, *attns, *wouts, wgu, wd]
    if final:
        in_specs.append(_resident((1, d)))
        args.append(final_norm)
    return pl.pallas_call(
        functools.partial(_ffn_kernel, n_attn=n_attn, tf=tf, final=final),
        out_shape=jax.ShapeDtypeStruct((b, s, d), F32),
        grid=(b, s // tm),
        in_specs=in_specs,
        out_specs=pl.BlockSpec((1, tm, d), lambda bi, i: (bi, i, 0)),
        scratch_shapes=[pltpu.VMEM((tm, d), BF16), pltpu.VMEM((tm, dff), BF16)],
        compiler_params=_cparams(("parallel", "parallel")),
        name="outproj_ffn",
    )(*args)


def _inproj1_kernel(x_ref, mod_ref, win_ref, cos_ref, sin_ref, q_ref, k_ref, v_ref):
    d = D_MODEL
    x = x_ref[0]
    a = (_rms(x) * (1.0 + mod_ref[0, :, d:2 * d]) + mod_ref[0, :, 0:d]).astype(BF16)
    p = _dot(a, win_ref[...])
    cos = cos_ref[...]
    sin = sin_ref[...]
    nq = SWA_HEADS * HEAD_DIM
    nkv = 2 * SWA_KV_HEADS * LANES
    quarter = HEAD_DIM // 4
    for t in range(nq // LANES):
        q = _rope_tile(p[:, t * LANES:(t + 1) * LANES], cos, sin, quarter)
        q_ref[0, :, t * LANES:(t + 1) * LANES] = (q * (HEAD_DIM ** -0.5 * LOG2_E)).astype(BF16)
    for t in range(nkv // LANES):
        k = _rope_tile(p[:, nq + t * LANES:nq + (t + 1) * LANES], cos, sin, quarter)
        k_ref[0, :, t * LANES:(t + 1) * LANES] = k.astype(BF16)
    v_ref[0] = p[:, nq + nkv:nq + 2 * nkv].astype(BF16)


def _inproj1(x, mods, grp, win, cos, sin, tm):
    b, s, d = x.shape
    nq = SWA_HEADS * HEAD_DIM
    nkv = 2 * SWA_KV_HEADS * LANES
    return pl.pallas_call(
        _inproj1_kernel,
        out_shape=(jax.ShapeDtypeStruct((b, s, nq), BF16),
                   jax.ShapeDtypeStruct((b, s, nkv), BF16),
                   jax.ShapeDtypeStruct((b, s, nkv), BF16)),
        grid=(b, s // tm),
        in_specs=[pl.BlockSpec((1, tm, d), lambda bi, i: (bi, i, 0)),
                  pl.BlockSpec((1, 1, mods.shape[2]), lambda bi, i: (grp(bi), 0, 0)),
                  pl.BlockSpec(win.shape, lambda bi, i: (0, 0)),
                  pl.BlockSpec((tm, LANES), lambda bi, i: (i, 0)),
                  pl.BlockSpec((tm, LANES), lambda bi, i: (i, 0))],
        out_specs=(pl.BlockSpec((1, tm, nq), lambda bi, i: (bi, i, 0)),
                   pl.BlockSpec((1, tm, nkv), lambda bi, i: (bi, i, 0)),
                   pl.BlockSpec((1, tm, nkv), lambda bi, i: (bi, i, 0))),
        compiler_params=_cparams(("parallel", "parallel")),
        name="inproj_odd",
    )(x, mods, win, cos, sin)


def _swa_kernel(q_ref, km_ref, k0_ref, kp_ref, vm_ref, v0_ref, vp_ref, kc_ref, vc_ref, sink_ref, o_ref, *, n_steps):
    step = pl.program_id(2)
    blk = SWA_BLOCK
    npair = q_ref.shape[2] // LANES
    kall = jnp.concatenate([km_ref[0], k0_ref[0], kp_ref[0]], axis=0)
    vall = jnp.concatenate([vm_ref[0], v0_ref[0], vp_ref[0]], axis=0)
    kc = kc_ref[0]
    vc = vc_ref[0]
    n_ctx = kc.shape[0]
    kc_st = jnp.concatenate([kc[:, :LANES], kc[:, LANES:]], axis=0)
    vc_st = jnp.concatenate([vc[:, :LANES], vc[:, LANES:]], axis=0)
    ind = _stack_pair((_pair_indicator(n_ctx, False), _pair_indicator(n_ctx, True)),
                      (_pair_indicator(3 * blk, False), _pair_indicator(3 * blk, True)))
    r = lax.broadcasted_iota(jnp.int32, (blk, 3 * blk), 0)
    c = lax.broadcasted_iota(jnp.int32, (blk, 3 * blk), 1)
    cc = c % blk
    lane = lax.broadcasted_iota(jnp.int32, (npair * blk, LANES), 1)
    lo = lane < HEAD_DIM

    def band(left_off, right_off):
        ok = ((c >= blk) & (c < 2 * blk)) | ((c < blk) & (cc >= r + left_off)) | ((c >= 2 * blk) & (cc <= r - right_off))
        neg = jnp.where(ok, 0.0, NEG_INF).astype(F32)
        return jnp.concatenate([neg] * npair, axis=0)

    for t in range(SWA_GROUP):
        left_off = jnp.where(step >= 1, 0, blk) if t == 0 else 0
        right_off = jnp.where(step <= n_steps - 2, 0, blk) if t == SWA_GROUP - 1 else 0
        neg = band(left_off, right_off)
        qb = q_ref[0, t * blk:(t + 1) * blk, :]
        qst = jnp.concatenate([qb[:, i * LANES:(i + 1) * LANES] for i in range(npair)], axis=0)
        kw = kall[t * blk:(t + 3) * blk]
        vw = vall[t * blk:(t + 3) * blk]
        k_st = jnp.concatenate([kc_st, kw[:, :LANES], kw[:, LANES:]], axis=0)
        v_st = jnp.concatenate([vc_st, vw[:, :LANES], vw[:, LANES:]], axis=0)
        s = _dot_nt(qst, k_st)
        p_ctx, p_lat, p_sink = [], [], []
        for par in range(2):
            sink = sink_ref[0, par]
            t_c = _tiles(s[:, par * n_ctx:(par + 1) * n_ctx])
            off = 2 * n_ctx + par * 3 * blk
            t_l = _tiles(s[:, off:off + 3 * blk] + neg)
            m = jnp.maximum(_rowmax128(t_c + t_l), sink)
            p_ctx += _exp_tiles(t_c, m)
            p_lat += _exp_tiles(t_l, m)
            p_sink.append(jnp.exp2(sink - m))
        o = _dot(jnp.concatenate(p_ctx + p_lat, axis=1), jnp.concatenate([v_st, ind], axis=1))
        o = o[:, :LANES] / (o[:, LANES:] + jnp.where(lo, p_sink[0], p_sink[1]))
        for i in range(npair):
            o_ref[0, t * blk:(t + 1) * blk, i * LANES:(i + 1) * LANES] = o[i * blk:(i + 1) * blk].astype(o_ref.dtype)


def _swa_attention(q, kvar, vvar, kc, vc, sink_tab):
    b, s, nq = q.shape
    blk = SWA_BLOCK
    nb = s // blk
    n_steps = nb // SWA_GROUP
    gw = nq // SWA_KV_HEADS
    n_ctx = kc.shape[1]
    prev = lambda bi, g, j: (bi, jnp.maximum(SWA_GROUP * j - 1, 0), g)
    cur = lambda bi, g, j: (bi, j, g)
    nxt = lambda bi, g, j: (bi, jnp.minimum(SWA_GROUP * (j + 1), nb - 1), g)
    edge_spec = lambda f: pl.BlockSpec((1, blk, 2 * LANES), f)
    main_spec = pl.BlockSpec((1, SWA_GROUP * blk, 2 * LANES), cur)
    ctx_spec = pl.BlockSpec((1, n_ctx, 2 * LANES), lambda bi, g, j: (bi, 0, g))
    return pl.pallas_call(
        functools.partial(_swa_kernel, n_steps=n_steps),
        out_shape=jax.ShapeDtypeStruct((b, s, nq), BF16),
        grid=(b, SWA_KV_HEADS, n_steps),
        in_specs=[pl.BlockSpec((1, SWA_GROUP * blk, gw), cur),
                  edge_spec(prev), main_spec, edge_spec(nxt),
                  edge_spec(prev), main_spec, edge_spec(nxt),
                  ctx_spec, ctx_spec,
                  pl.BlockSpec((1, 2, sink_tab.shape[2], LANES), lambda bi, g, j: (g, 0, 0, 0))],
        out_specs=pl.BlockSpec((1, SWA_GROUP * blk, gw), cur),
        compiler_params=_cparams(("parallel", "parallel", "arbitrary")),
        name="swa_attention",
    )(q, kvar, kvar, kvar, vvar, vvar, vvar, kc, vc, sink_tab)


def _rope_tables(s, dim, pad_to):
    n_rows = s // GRID_W
    half = dim // 2
    inv = ROPE_BASE ** (-jnp.arange(0, half, 2, dtype=F32) / half)
    ar = jnp.arange(n_rows, dtype=jnp.int32).astype(F32)[:, None] * inv
    ac = jnp.arange(GRID_W, dtype=jnp.int32).astype(F32)[:, None] * inv
    expand_r = lambda a: jnp.repeat(a, GRID_W, axis=0)
    expand_c = lambda a: jnp.tile(a, (n_rows, 1))
    cos = jnp.concatenate([expand_r(jnp.cos(ar))] * 2 + [expand_c(jnp.cos(ac))] * 2, axis=1)
    sin = jnp.concatenate([expand_r(-jnp.sin(ar)), expand_r(jnp.sin(ar)),
                           expand_c(-jnp.sin(ac)), expand_c(jnp.sin(ac))], axis=1)
    if pad_to > dim:
        cos = jnp.concatenate([cos, jnp.ones((s, pad_to - dim), F32)], axis=1)
        sin = jnp.concatenate([sin, jnp.zeros((s, pad_to - dim), F32)], axis=1)
    reps = LANES // cos.shape[1]
    return jnp.tile(cos, (1, reps)), jnp.tile(sin, (1, reps))


def _even_weights(w_in, w_q_up, w_uk, w_uv, w_out):
    d = w_in.shape[0]
    win = jnp.concatenate([w_in[:, :MLA_IN], jnp.zeros((d, 4 * LANES - MLA_IN), w_in.dtype), w_in[:, MLA_IN:]], axis=1)
    wq3 = w_q_up.reshape(MLA_Q_RANK, MLA_HEADS, MLA_NOPE + MLA_ROPE)
    nope = wq3[:, :, :MLA_NOPE].reshape(MLA_Q_RANK, MLA_HEADS * MLA_NOPE)
    rope = jnp.pad(wq3[:, :, MLA_NOPE:], ((0, 0), (0, 0), (0, LANES - MLA_ROPE))).reshape(MLA_Q_RANK, MLA_HEADS * LANES)
    wq = jnp.concatenate([nope, rope], axis=1)
    eye = jnp.eye(MLA_HEADS, dtype=w_uk.dtype)
    wuk = jnp.einsum('hcn,hg->hngc', w_uk, eye).reshape(MLA_HEADS * MLA_NOPE, MLA_HEADS * MLA_KV_RANK)
    wuv = jnp.einsum('hcv,hg->hcgv', w_uv, eye).reshape(MLA_HEADS * MLA_KV_RANK, MLA_HEADS * MLA_V)
    n_mla = MLA_HEADS * MLA_V
    return (win.astype(BF16), wq.astype(BF16), wuk.astype(BF16), wuv.astype(BF16),
            w_out[:n_mla].astype(BF16), w_out[n_mla:].astype(BF16))


def _odd_weights(w_in):
    d = w_in.shape[0]
    nq = SWA_HEADS * HEAD_DIM
    z = jnp.zeros((d, HEAD_DIM), w_in.dtype)

    def variants(off):
        cols = []
        for g in range(SWA_KV_HEADS):
            w = w_in[:, off + g * HEAD_DIM:off + (g + 1) * HEAD_DIM]
            cols += [w, z, z, w]
        return jnp.concatenate(cols, axis=1)

    kcols = variants(nq)
    vcols = variants(nq + SWA_KV_HEADS * HEAD_DIM)
    return jnp.concatenate([w_in[:, :nq], kcols, vcols], axis=1).astype(BF16)


def _sink_table(sinks):
    g = SWA_HEADS // SWA_KV_HEADS
    t = (sinks.astype(F32) * LOG2_E).reshape(SWA_KV_HEADS, g // 2, 2).transpose(0, 2, 1)
    t = jnp.broadcast_to(t[:, :, :, None, None], (SWA_KV_HEADS, 2, g // 2, SWA_BLOCK, LANES))
    return t.reshape(SWA_KV_HEADS, 2, (g // 2) * SWA_BLOCK, LANES)


def kernel(x, c, ctx, c_ctx, mod_w, mod_b, even_w_in, mla_q_norm, mla_kv_norm, mla_w_q_up, mla_w_uk, mla_w_uv,
           na_rel_bias, even_w_out, odd_w_in, swa_sinks, odd_w_out, ffn_w_gate_up, ffn_w_down, final_norm):
    b, s, d = x.shape
    n_ctx = ctx.shape[1]
    n_rows = s // GRID_W
    assert d == D_MODEL and s % (NA_STEP_BLOCKS * NA_RB * GRID_W) == 0 and s % (SWA_GROUP * SWA_BLOCK) == 0
    assert mod_w.shape[0] == 2 and b <= 4

    ctx_grp = 4
    c8 = jnp.zeros((8, d), F32).at[:b].set(c.astype(F32)).at[ctx_grp].set(c_ctx.astype(F32))
    mods = _modulation(c8, mod_w.astype(F32), mod_b.astype(F32))
    mods0 = mods[0].reshape(8, 1, 6 * d)
    mods1 = mods[1].reshape(8, 1, 6 * d)
    lat_grp = lambda bi: bi
    ctx_g = lambda bi: ctx_grp

    tm = min(512, s)
    tf = 256
    wgu = ffn_w_gate_up.astype(BF16)
    wdn = ffn_w_down.astype(BF16)

    win, wq, wuk, wuv, wo_mla, wo_na = _even_weights(even_w_in[0], mla_w_q_up[0], mla_w_uk[0], mla_w_uv[0],
                                                      even_w_out[0])
    qn = mla_q_norm[0].astype(F32).reshape(1, -1)
    kvn = mla_kv_norm[0].astype(F32).reshape(1, -1)
    cos_m, sin_m = _rope_tables(s, MLA_ROPE, LANES)
    one_c = jnp.ones((n_ctx, LANES), F32)
    zero_c = jnp.zeros((n_ctx, LANES), F32)

    q_l, k_l, naq_l, nak_l, nav_l = _inproj0(x, mods0, lat_grp, win, qn, wq, wuk, kvn, cos_m, sin_m, tm)
    q_c, k_c, naq_c, nak_c, nav_c = _inproj0(ctx, mods0, ctx_g, win, qn, wq, wuk, kvn, one_c, zero_c, n_ctx)

    k_all = jnp.concatenate([k_c, k_l], axis=1)
    kt_all = jnp.swapaxes(k_all, 1, 2)
    o_mla_l = _mla_attention(q_l, kt_all, k_all, wuv, min(1024, s))
    o_mla_c = _mla_attention(q_c, jnp.swapaxes(k_c, 1, 2), k_c, wuv, n_ctx)

    bias = _na_bias_table(na_rel_bias[0], n_rows)
    o_na_l = _na_attention(naq_l, nak_l, nav_l, nak_c, nav_c, bias)
    o_na_c = _pair_ctx_attention(naq_c, nak_c, nav_c)

    h_lat = _ffn_block(x, mods0, lat_grp, [o_mla_l, o_na_l], [wo_mla, wo_na], wgu[0], wdn[0], None, tm, tf)
    h_ctx = _ffn_block(ctx, mods0, ctx_g, [o_mla_c, o_na_c], [wo_mla, wo_na], wgu[0], wdn[0], None, n_ctx, tf)

    win1 = _odd_weights(odd_w_in[0])
    cos_s, sin_s = _rope_tables(s, HEAD_DIM, HEAD_DIM)
    q1, k1, v1 = _inproj1(h_lat, mods1, lat_grp, win1, cos_s, sin_s, tm)
    _, k1c, v1c = _inproj1(h_ctx, mods1, ctx_g, win1, one_c, zero_c, n_ctx)
    o_swa = _swa_attention(q1, k1, v1, k1c, v1c, _sink_table(swa_sinks[0]))
    fn = final_norm.astype(F32).reshape(1, d)
    return _ffn_block(h_lat, mods1, lat_grp, [o_swa], [odd_w_out[0].astype(BF16)], wgu[1], wdn[1], fn, tm, tf)
```

```python
import functools

import numpy as np
import jax
import jax.numpy as jnp
from jax import lax
from jax.experimental import pallas as pl
from jax.experimental.pallas import tpu as pltpu

F32 = jnp.float32
BF16 = jnp.bfloat16

D_MODEL = 1024
GRID_W = 64
HEAD_DIM = 64
ROPE_BASE = 10000.0
EPS = 1e-6
NEG_INF = -1e30
LOG2_E = 1.4426950408889634

MLA_HEADS = 8
MLA_NOPE = 64
MLA_ROPE = 32
MLA_V = 64
MLA_Q_RANK = 256
MLA_KV_RANK = 128
MLA_IN = MLA_Q_RANK + MLA_KV_RANK + MLA_ROPE

NA_HEADS = 8
NA_KR = 8
NA_KC = 16
NA_RB = 4
NA_SPAN = 12
NA_STEP_BLOCKS = 8

SWA_HEADS = 16
SWA_KV_HEADS = 2
SWA_WINDOW = 128
SWA_BLOCK = 128
SWA_GROUP = 8

LANES = 128
MLA_QK_PAD = 256
MLA_ITEM_ROWS = 256
MLA_KEY_CHUNK = 768
ONES_LANE = 0
ROPE_LANE = 8
MLA_V_ROWS = 144
VMEM_LIMIT = 56 * 1024 * 1024


def _cparams(sem):
    return pltpu.CompilerParams(dimension_semantics=sem, vmem_limit_bytes=VMEM_LIMIT)


def _dot(a, b):
    return jnp.dot(a, b, preferred_element_type=F32)


def _dot_nt(a, b):
    return lax.dot_general(a, b, (((1,), (1,)), ((), ())), preferred_element_type=F32)


def _rms(x):
    return x * lax.rsqrt(jnp.mean(x * x, axis=-1, keepdims=True) + EPS)


def _silu(x):
    return x * (1.0 / (1.0 + jnp.exp(-x)))


def _rope_tile(x, cos, sin, quarter, first_lane=0):
    lane = lax.broadcasted_iota(jnp.int32, x.shape, 1)
    first = ((lane + (2 * quarter - first_lane)) % (2 * quarter)) < quarter
    swapped = jnp.where(first, pltpu.roll(x, LANES - quarter, 1), pltpu.roll(x, quarter, 1))
    return x * cos + swapped * sin


def _zero_of(x):
    u = lax.bitcast_convert_type(x, jnp.uint32)
    u = lax.shift_right_logical(lax.shift_right_logical(u, jnp.uint32(16)), jnp.uint32(16))
    return lax.bitcast_convert_type(u, F32)


def _tiles(s):
    return [s[:, j * LANES:(j + 1) * LANES] for j in range(s.shape[1] // LANES)]


def _rowmax128(tiles):
    mx = tiles[0]
    for t in tiles[1:]:
        mx = jnp.maximum(mx, t)
    return jnp.broadcast_to(jnp.max(mx, axis=1, keepdims=True), mx.shape)


def _rowmax(s):
    return _rowmax128(_tiles(s))


def _exp_tiles(tiles, m):
    return [jnp.exp2(t - m).astype(BF16) for t in tiles]


def _pair_indicator(n, hi):
    lane = lax.broadcasted_iota(jnp.int32, (n, LANES), 1)
    return jnp.where((lane >= HEAD_DIM) == hi, 1.0, 0.0).astype(BF16)


def _mod_kernel(c_ref, w_ref, b_ref, o_ref):
    a = _silu(c_ref[...])
    o_ref[0] = jnp.dot(a, w_ref[0], precision=lax.Precision.HIGHEST, preferred_element_type=F32) + b_ref[0]


def _modulation(c8, mod_w, mod_b):
    depth, d, n = mod_w.shape
    tn = 1536
    return pl.pallas_call(
        _mod_kernel,
        out_shape=jax.ShapeDtypeStruct((depth, 8, n), F32),
        grid=(depth, n // tn),
        in_specs=[pl.BlockSpec((8, d), lambda l, j: (0, 0)),
                  pl.BlockSpec((1, d, tn), lambda l, j: (l, 0, j)),
                  pl.BlockSpec((1, 1, tn), lambda l, j: (l, 0, j))],
        out_specs=pl.BlockSpec((1, 8, tn), lambda l, j: (l, 0, j)),
        compiler_params=_cparams(("parallel", "parallel")),
        name="modulation",
    )(c8, mod_w, mod_b.reshape(depth, 1, n))


def _inproj0_kernel(x_ref, mod_ref, win_ref, qn_ref, wq_ref, wuk_ref, kvn_ref, cos_ref, sin_ref,
                    qmla_ref, kmla_ref, naq_ref, nak_ref, nav_ref):
    d = D_MODEL
    x = x_ref[0]
    shift = mod_ref[0, :, 0:d]
    scale = mod_ref[0, :, d:2 * d]
    a = (_rms(x) * (1.0 + scale) + shift).astype(BF16)
    p = _dot(a, win_ref[...])
    cos = cos_ref[...]
    sin = sin_ref[...]
    mla_scale = (MLA_NOPE + MLA_ROPE) ** -0.5 * LOG2_E

    cq = (_rms(p[:, 0:MLA_Q_RANK]) * qn_ref[...]).astype(BF16)
    q = _dot(cq, wq_ref[...])
    n_nope = MLA_HEADS * MLA_NOPE
    q_lat = _dot(q[:, 0:n_nope].astype(BF16), wuk_ref[...])
    for h in range(MLA_HEADS):
        qr = _rope_tile(q[:, n_nope + h * LANES:n_nope + (h + 1) * LANES], cos, sin, MLA_ROPE // 4, ROPE_LANE)
        qmla_ref[0, h, :, 0:LANES] = (q_lat[:, h * LANES:(h + 1) * LANES] * mla_scale).astype(BF16)
        qmla_ref[0, h, :, LANES:2 * LANES] = (qr * mla_scale).astype(BF16)

    ckv = _rms(p[:, MLA_Q_RANK:MLA_Q_RANK + MLA_KV_RANK]) * kvn_ref[...]
    kr = _rope_tile(p[:, 3 * LANES:4 * LANES], cos, sin, MLA_ROPE // 4, ROPE_LANE)
    lane = lax.broadcasted_iota(jnp.int32, kr.shape, 1)
    kr = jnp.where(lane == ONES_LANE, 1.0, kr)
    kmla_ref[0, :, 0:LANES] = ckv.astype(BF16)
    kmla_ref[0, :, LANES:2 * LANES] = kr.astype(BF16)

    w = NA_HEADS * HEAD_DIM
    naq_ref[0] = (p[:, 4 * LANES:4 * LANES + w] * (HEAD_DIM ** -0.5 * LOG2_E)).astype(BF16)
    nak_ref[0] = p[:, 4 * LANES + w:4 * LANES + 2 * w].astype(BF16)
    nav_ref[0] = p[:, 4 * LANES + 2 * w:4 * LANES + 3 * w].astype(BF16)


def _inproj0(x, mods, grp, win, qn, wq, wuk, kvn, cos, sin, tm):
    b, s, d = x.shape
    nt = s // tm
    w = NA_HEADS * HEAD_DIM
    const = lambda bi, i: (0, 0)
    return pl.pallas_call(
        _inproj0_kernel,
        out_shape=(jax.ShapeDtypeStruct((b, MLA_HEADS, s, MLA_QK_PAD), BF16),
                   jax.ShapeDtypeStruct((b, s, MLA_QK_PAD), BF16),
                   jax.ShapeDtypeStruct((b, s, w), BF16),
                   jax.ShapeDtypeStruct((b, s, w), BF16),
                   jax.ShapeDtypeStruct((b, s, w), BF16)),
        grid=(b, nt),
        in_specs=[pl.BlockSpec((1, tm, d), lambda bi, i: (bi, i, 0)),
                  pl.BlockSpec((1, 1, mods.shape[2]), lambda bi, i: (grp(bi), 0, 0)),
                  pl.BlockSpec(win.shape, const),
                  pl.BlockSpec(qn.shape, const),
                  pl.BlockSpec(wq.shape, const),
                  pl.BlockSpec(wuk.shape, const),
                  pl.BlockSpec(kvn.shape, const),
                  pl.BlockSpec((tm, LANES), lambda bi, i: (i, 0)),
                  pl.BlockSpec((tm, LANES), lambda bi, i: (i, 0))],
        out_specs=(pl.BlockSpec((1, MLA_HEADS, tm, MLA_QK_PAD), lambda bi, i: (bi, 0, i, 0)),
                   pl.BlockSpec((1, tm, MLA_QK_PAD), lambda bi, i: (bi, i, 0)),
                   pl.BlockSpec((1, tm, w), lambda bi, i: (bi, i, 0)),
                   pl.BlockSpec((1, tm, w), lambda bi, i: (bi, i, 0)),
                   pl.BlockSpec((1, tm, w), lambda bi, i: (bi, i, 0))),
        compiler_params=_cparams(("parallel", "parallel")),
        name="inproj_even",
    )(x, mods, win, qn, wq, wuk, kvn, cos, sin)


def _mla_kernel(q_ref, kt_ref, k_ref, wuv_ref, o_ref, s0_ref, s1_ref, p0_ref, p1_ref, m0_ref, m1_ref, oh_ref):
    n_heads, tq = q_ref.shape[1], q_ref.shape[2]
    rc, nk = s0_ref.shape
    n_items = (tq // rc) * n_heads
    s_refs, p_refs, m_refs = (s0_ref, s1_ref), (p0_ref, p1_ref), (m0_ref, m1_ref)

    def item(i):
        if isinstance(i, int):
            return i % n_heads, (i // n_heads) * rc
        return lax.rem(i, n_heads), pl.multiple_of(lax.div(i, n_heads) * rc, rc)

    tk = MLA_KEY_CHUNK if nk % MLA_KEY_CHUNK == 0 else 2 * LANES

    def round_(r, par):
        static = isinstance(r, int)
        do_qk = not static or r < n_items
        do_sm = not static or 1 <= r <= n_items
        do_pv = not static or 2 <= r <= n_items + 1
        if do_qk:
            h_q, row_q = item(r)
            q = q_ref[0, h_q, pl.ds(row_q, rc), :]
        rg = 32
        if do_pv:
            acc = None
        for c in range(nk // tk):
            cols = slice(c * tk, (c + 1) * tk)
            if do_qk:
                s = _dot(q, kt_ref[0, :, cols])
                s_refs[par][:, cols] = s
                tiles = _tiles(s)
                for j in range(0, len(tiles), 2):
                    pm = jnp.maximum(tiles[j], tiles[j + 1])
                    m_refs[par][...] = pm if c == 0 and j == 0 else jnp.maximum(m_refs[par][...], pm)
            if do_sm:
                zero = jnp.tile(_zero_of(s[0:8, 0:LANES]), (rg // 8, 1)) if do_qk else None
                for g in range(rc // rg):
                    rows = slice(g * rg, (g + 1) * rg)
                    m = m_refs[1 - par][rows, :]
                    m = m + zero if do_qk else m
                    for j in range(c * tk // LANES, (c + 1) * tk // LANES):
                        lanes = slice(j * LANES, (j + 1) * LANES)
                        p_refs[1 - par][rows, lanes] = jnp.exp2(s_refs[1 - par][rows, lanes] - m).astype(BF16)
            if do_pv:
                part = _dot(p_refs[par][:, cols], k_ref[0, cols, :])
                acc = part if acc is None else acc + part
        if do_qk:
            mx = m_refs[par][...]
            m_refs[par][...] = jnp.broadcast_to(jnp.max(mx, axis=1, keepdims=True), mx.shape)
        if do_pv:
            h_v, row_v = item(r - 2)
            tail = acc[:, LANES:2 * LANES]
            lane = lax.broadcasted_iota(jnp.int32, tail.shape, 1)
            l = jnp.sum(jnp.where(lane == ONES_LANE, tail, 0.0), axis=1, keepdims=True)
            oh_ref[h_v, pl.ds(row_v, rc), :] = (acc[:, 0:LANES] / l).astype(BF16)

    round_(0, 0)
    round_(1, 1)
    n_full = max(n_items - 2, 0)

    def pair(i, carry):
        r = 2 + 2 * i
        round_(r, 0)
        round_(r + 1, 1)
        return carry

    lax.fori_loop(0, n_full // 2, pair, 0)
    for r in range(2 + 2 * (n_full // 2), n_items):
        round_(r, r % 2)
    for r in range(max(n_items, 2), n_items + 2):
        round_(r, r % 2)

    o = jnp.concatenate([oh_ref[h] for h in range(n_heads)], axis=1)
    o_ref[0] = _dot(o, wuv_ref[...]).astype(o_ref.dtype)


def _mla_attention(q, kt, k, wuv, tq):
    b, h, sq, _ = q.shape
    nk = k.shape[1]
    rc = min(MLA_ITEM_ROWS, tq)
    assert nk % LANES == 0 and tq % rc == 0 and sq % tq == 0
    wo = wuv.shape[1]
    one_buffer = pl.Buffered(1)
    return pl.pallas_call(
        _mla_kernel,
        out_shape=jax.ShapeDtypeStruct((b, sq, wo), BF16),
        grid=(b, sq // tq),
        in_specs=[pl.BlockSpec((1, h, tq, MLA_QK_PAD), lambda bi, i: (bi, 0, i, 0)),
                  pl.BlockSpec((1, MLA_QK_PAD, nk), lambda bi, i: (bi, 0, 0), pipeline_mode=one_buffer),
                  pl.BlockSpec((1, nk, MLA_QK_PAD), lambda bi, i: (bi, 0, 0), pipeline_mode=one_buffer),
                  _resident(wuv.shape)],
        out_specs=pl.BlockSpec((1, tq, wo), lambda bi, i: (bi, i, 0)),
        scratch_shapes=[pltpu.VMEM((rc, nk), F32), pltpu.VMEM((rc, nk), F32),
                        pltpu.VMEM((rc, nk), BF16), pltpu.VMEM((rc, nk), BF16),
                        pltpu.VMEM((rc, LANES), F32), pltpu.VMEM((rc, LANES), F32),
                        pltpu.VMEM((h, tq, LANES), BF16)],
        compiler_params=_cparams(("parallel", "arbitrary")),
        name="mla_attention",
    )(q, kt, k, wuv)


def _mla_t_kernel(q_ref, k_ref, vt_ref, wuvt_ref, o_ref, s0_ref, s1_ref, p0_ref, p1_ref, m0_ref, m1_ref, oh_ref):
    n_heads, tq = q_ref.shape[1], q_ref.shape[2]
    nk, rc = s0_ref.shape
    n_items = (tq // rc) * n_heads
    s_refs, p_refs, m_refs = (s0_ref, s1_ref), (p0_ref, p1_ref), (m0_ref, m1_ref)
    tk = MLA_KEY_CHUNK if nk % MLA_KEY_CHUNK == 0 else 2 * LANES
    rg = 128

    def item(i):
        if isinstance(i, int):
            return i % n_heads, i // n_heads
        return lax.rem(i, n_heads), lax.div(i, n_heads)

    def round_(r, par):
        static = isinstance(r, int)
        do_qk = not static or r < n_items
        do_sm = not static or 1 <= r <= n_items
        do_pv = not static or 2 <= r <= n_items + 1
        if do_qk:
            h_q, c_q = item(r)
            row_q = c_q * rc if static else pl.multiple_of(c_q * rc, rc)
            q = q_ref[0, h_q, pl.ds(row_q, rc), :]
        if do_sm:
            m8 = m_refs[1 - par][...]
        if do_pv:
            acc = None
        for c in range(nk // tk):
            keys = slice(c * tk, (c + 1) * tk)
            if do_qk:
                st = _dot_nt(k_ref[0, keys, :], q)
                s_refs[par][keys, :] = st
                cm = jnp.max(st.reshape(tk // 8, 8, rc), axis=0)
                m_refs[par][...] = cm if c == 0 else jnp.maximum(m_refs[par][...], cm)
            if do_sm:
                m = m8 + _zero_of(st[0:8, :]) if do_qk else m8
                for g in range(tk // rg):
                    blk = slice(c * tk + g * rg, c * tk + (g + 1) * rg)
                    sb = s_refs[1 - par][blk, :].reshape(rg // 8, 8, rc)
                    p_refs[1 - par][blk, :] = jnp.exp2(sb - m[None]).reshape(rg, rc).astype(BF16)
            if do_pv:
                part = _dot(vt_ref[0, :, keys], p_refs[par][keys, :])
                acc = part if acc is None else acc + part
        if do_qk:
            mx = m_refs[par][...]
            m_refs[par][...] = jnp.broadcast_to(jnp.max(mx, axis=0, keepdims=True), mx.shape)
        if do_pv:
            h_v, c_v = item(r - 2)
            l = acc[MLA_KV_RANK + ONES_LANE:MLA_KV_RANK + ONES_LANE + 1, :]
            oh_ref[c_v, h_v] = (acc[0:MLA_KV_RANK, :] / l).astype(BF16)

    round_(0, 0)
    round_(1, 1)
    n_full = max(n_items - 2, 0)

    def pair(i, carry):
        r = 2 + 2 * i
        round_(r, 0)
        round_(r + 1, 1)
        return carry

    lax.fori_loop(0, n_full // 2, pair, 0)
    for r in range(2 + 2 * (n_full // 2), n_items):
        round_(r, r % 2)
    for r in range(max(n_items, 2), n_items + 2):
        round_(r, r % 2)

    for ci in range(tq // rc):
        ot = jnp.concatenate([oh_ref[ci, h] for h in range(n_heads)], axis=0)
        out_t = _dot(wuvt_ref[...], ot)
        o_ref[0, ci * rc:(ci + 1) * rc, :] = out_t.T.astype(o_ref.dtype)


def _mla_t_attention(q, k, vt, wuvt, tq):
    b, h, sq, _ = q.shape
    nk = k.shape[1]
    rc = min(MLA_ITEM_ROWS, tq)
    assert nk % (2 * LANES) == 0 and tq % rc == 0 and sq % tq == 0
    wo = wuvt.shape[0]
    one_buffer = pl.Buffered(1)
    return pl.pallas_call(
        _mla_t_kernel,
        out_shape=jax.ShapeDtypeStruct((b, sq, wo), BF16),
        grid=(b, sq // tq),
        in_specs=[pl.BlockSpec((1, h, tq, MLA_QK_PAD), lambda bi, i: (bi, 0, i, 0)),
                  pl.BlockSpec((1, nk, MLA_QK_PAD), lambda bi, i: (bi, 0, 0), pipeline_mode=one_buffer),
                  pl.BlockSpec((1, MLA_V_ROWS, nk), lambda bi, i: (bi, 0, 0), pipeline_mode=one_buffer),
                  _resident(wuvt.shape)],
        out_specs=pl.BlockSpec((1, tq, wo), lambda bi, i: (bi, i, 0)),
        scratch_shapes=[pltpu.VMEM((nk, rc), F32), pltpu.VMEM((nk, rc), F32),
                        pltpu.VMEM((nk, rc), BF16), pltpu.VMEM((nk, rc), BF16),
                        pltpu.VMEM((8, rc), F32), pltpu.VMEM((8, rc), F32),
                        pltpu.VMEM((tq // rc, h, MLA_KV_RANK, rc), BF16)],
        compiler_params=_cparams(("parallel", "arbitrary")),
        name="mla_attention",
    )(q, k, vt, wuvt)


def _split_heads(t):
    lane = lax.broadcasted_iota(jnp.int32, t.shape, 1)
    lo = lane < HEAD_DIM
    zero = jnp.zeros_like(t)
    return jnp.where(lo, t, zero), jnp.where(lo, zero, t)


def _stack_pair(lo_hi_ctx, lo_hi_lat):
    return jnp.concatenate([lo_hi_ctx[0], lo_hi_ctx[1], lo_hi_lat[0], lo_hi_lat[1]], axis=0)


def _na_kernel(q_ref, k_ref, v_ref, kc_ref, vc_ref, bias_ref, o_ref, *, n_rows):
    step = pl.program_id(2)
    nrb = n_rows // NA_RB
    n_keys = NA_SPAN * GRID_W
    n_ctx = kc_ref.shape[1]
    tq = NA_RB * GRID_W
    kcs = _split_heads(kc_ref[0])
    vcs = _split_heads(vc_ref[0])
    ind = _stack_pair((_pair_indicator(n_ctx, False), _pair_indicator(n_ctx, True)),
                      (_pair_indicator(n_keys, False), _pair_indicator(n_keys, True)))
    for i in range(NA_STEP_BLOCKS):
        rb = step * NA_STEP_BLOCKS + i
        case = jnp.minimum(rb, 1) + jnp.maximum(rb - (nrb - 2), 0)
        base = jnp.clip(rb * NA_RB - NA_KR // 2, 0, n_rows - NA_SPAN)
        start = pl.multiple_of(base * GRID_W, 256)
        q = q_ref[0, i * tq:(i + 1) * tq, :]
        k_st = _stack_pair(kcs, _split_heads(k_ref[0, pl.ds(start, n_keys), :]))
        v_st = _stack_pair(vcs, _split_heads(v_ref[0, pl.ds(start, n_keys), :]))
        s = _dot_nt(q, k_st)
        p_ctx, p_lat = [], []
        for par in range(2):
            t_c = _tiles(s[:, par * n_ctx:(par + 1) * n_ctx])
            off = 2 * n_ctx + par * n_keys
            t_l = _tiles(s[:, off:off + n_keys] + bias_ref[case, par])
            m = _rowmax128(t_c + t_l)
            p_ctx += _exp_tiles(t_c, m)
            p_lat += _exp_tiles(t_l, m)
        o = _dot(jnp.concatenate(p_ctx + p_lat, axis=1), jnp.concatenate([v_st, ind], axis=1))
        o_ref[0, i * tq:(i + 1) * tq, :] = (o[:, :LANES] / o[:, LANES:]).astype(o_ref.dtype)


def _na_attention(q, k, v, kc, vc, bias):
    b, s, w = q.shape
    n_rows = s // GRID_W
    nrb = n_rows // NA_RB
    tq = NA_STEP_BLOCKS * NA_RB * GRID_W
    n_ctx = kc.shape[1]
    return pl.pallas_call(
        functools.partial(_na_kernel, n_rows=n_rows),
        out_shape=jax.ShapeDtypeStruct((b, s, w), BF16),
        grid=(w // LANES, b, nrb // NA_STEP_BLOCKS),
        in_specs=[pl.BlockSpec((1, tq, LANES), lambda j, bi, r: (bi, r, j)),
                  pl.BlockSpec((1, s, LANES), lambda j, bi, r: (bi, 0, j)),
                  pl.BlockSpec((1, s, LANES), lambda j, bi, r: (bi, 0, j)),
                  pl.BlockSpec((1, n_ctx, LANES), lambda j, bi, r: (bi, 0, j)),
                  pl.BlockSpec((1, n_ctx, LANES), lambda j, bi, r: (bi, 0, j)),
                  pl.BlockSpec((3, 2, NA_RB * GRID_W, NA_SPAN * GRID_W), lambda j, bi, r: (0, j, 0, 0))],
        out_specs=pl.BlockSpec((1, tq, LANES), lambda j, bi, r: (bi, r, j)),
        compiler_params=_cparams(("parallel", "parallel", "arbitrary")),
        name="na_attention",
    )(q, k, v, kc, vc, bias)


def _pair_ctx_kernel(q_ref, kc_ref, vc_ref, o_ref):
    q = q_ref[0]
    n_ctx = kc_ref.shape[1]
    kcs = _split_heads(kc_ref[0])
    vcs = _split_heads(vc_ref[0])
    s = _dot_nt(q, jnp.concatenate(kcs, axis=0))
    p = []
    for par in range(2):
        t_c = _tiles(s[:, par * n_ctx:(par + 1) * n_ctx])
        p += _exp_tiles(t_c, _rowmax128(t_c))
    ind = jnp.concatenate([_pair_indicator(n_ctx, False), _pair_indicator(n_ctx, True)], axis=0)
    o = _dot(jnp.concatenate(p, axis=1), jnp.concatenate([jnp.concatenate(vcs, axis=0), ind], axis=1))
    o_ref[0] = (o[:, :LANES] / o[:, LANES:]).astype(o_ref.dtype)


def _pair_ctx_attention(q, kc, vc):
    b, n, w = q.shape
    spec = pl.BlockSpec((1, n, LANES), lambda bi, j: (bi, 0, j))
    return pl.pallas_call(
        _pair_ctx_kernel,
        out_shape=jax.ShapeDtypeStruct((b, n, w), BF16),
        grid=(b, w // LANES),
        in_specs=[spec, spec, spec],
        out_specs=spec,
        compiler_params=_cparams(("parallel", "parallel")),
        name="na_ctx_attention",
    )(q, kc, vc)


def _na_bias_table(rel_bias, n_rows):
    qc = np.arange(GRID_W)
    cstart = np.clip(qc - NA_KC // 2, 0, GRID_W - NA_KC)
    kc = np.arange(GRID_W)
    col_ok = (kc[None, :] >= cstart[:, None]) & (kc[None, :] < cstart[:, None] + NA_KC)
    n_dr, n_dc = 2 * NA_KR - 1, 2 * NA_KC - 1
    nh = rel_bias.shape[0]
    dc = kc[None, :] - qc[:, None] + NA_KC - 1
    sel_col = ((dc[None] == np.arange(n_dc)[:, None, None]) & col_ok[None]).astype(np.float32)
    toe = jnp.einsum('hdu,uqk->hdqk', rel_bias.astype(F32), jnp.asarray(sel_col), precision=lax.Precision.HIGHEST)
    toe = jnp.where(jnp.asarray(col_ok)[None, None], toe * LOG2_E, NEG_INF)
    toe = jnp.concatenate([toe, jnp.full((nh, 1, GRID_W, GRID_W), NEG_INF, F32)], axis=1)
    toe = jnp.concatenate([toe, toe], axis=-1)
    idx = np.full((3, NA_RB, NA_SPAN), n_dr, np.int32)
    nrb = n_rows // NA_RB
    for case, rb in enumerate((0, min(1, nrb - 1), nrb - 1)):
        r0 = rb * NA_RB
        base = int(np.clip(r0 - NA_KR // 2, 0, n_rows - NA_SPAN))
        for a in range(NA_RB):
            r = r0 + a
            rs = int(np.clip(r - NA_KR // 2, 0, n_rows - NA_KR))
            for t in range(NA_SPAN):
                kr = base + t
                if rs <= kr < rs + NA_KR:
                    idx[case, a, t] = kr - r + NA_KR - 1

    def assemble(toe_ref, o_ref):
        lo = lax.broadcasted_iota(jnp.int32, (GRID_W, LANES), 1) < GRID_W
        for case in range(3):
            for a in range(NA_RB):
                for tp in range(NA_SPAN // 2):
                    d0, d1 = int(idx[case, a, 2 * tp]), int(idx[case, a, 2 * tp + 1])
                    tile = toe_ref[0, d0] if d0 == d1 else jnp.where(lo, toe_ref[0, d0], toe_ref[0, d1])
                    o_ref[case, 0, a * GRID_W:(a + 1) * GRID_W, tp * LANES:(tp + 1) * LANES] = tile

    tq, tkeys = NA_RB * GRID_W, NA_SPAN * GRID_W
    return pl.pallas_call(
        assemble,
        out_shape=jax.ShapeDtypeStruct((3, nh, tq, tkeys), F32),
        grid=(nh,),
        in_specs=[pl.BlockSpec((1, n_dr + 1, GRID_W, LANES), lambda h: (h, 0, 0, 0))],
        out_specs=pl.BlockSpec((3, 1, tq, tkeys), lambda h: (0, h, 0, 0)),
        compiler_params=_cparams(("parallel",)),
        name="na_bias_table",
    )(toe)


def _ffn_kernel(*refs, n_attn, tf, final):
    h_ref, mod_ref = refs[0], refs[1]
    attn_refs = refs[2:2 + n_attn]
    wout_refs = refs[2 + n_attn:2 + 2 * n_attn]
    wgu_ref, wd_ref = refs[2 + 2 * n_attn:4 + 2 * n_attn]
    pos = 4 + 2 * n_attn
    fn_ref = refs[pos] if final else None
    pos += 1 if final else 0
    o_ref, a2_ref, f_ref = refs[pos:pos + 3]
    d = D_MODEL
    dff = wd_ref.shape[0]

    proj = _dot(attn_refs[0][0], wout_refs[0][...])
    for t in range(1, n_attn):
        proj = proj + _dot(attn_refs[t][0], wout_refs[t][...])
    h1 = h_ref[0] + mod_ref[0, :, 2 * d:3 * d] * proj
    o_ref[0] = h1
    a2_ref[...] = (_rms(h1) * (1.0 + mod_ref[0, :, 4 * d:5 * d]) + mod_ref[0, :, 3 * d:4 * d]).astype(BF16)

    for j in range(dff // tf):
        a2 = a2_ref[...]
        g = _dot(a2, wgu_ref[:, j * tf:(j + 1) * tf])
        u = _dot(a2, wgu_ref[:, dff + j * tf:dff + (j + 1) * tf])
        f_ref[:, j * tf:(j + 1) * tf] = (_silu(g) * u).astype(BF16)

    out = o_ref[0] + mod_ref[0, :, 5 * d:6 * d] * _dot(f_ref[...], wd_ref[...])
    if final:
        out = _rms(out) * fn_ref[...]
    o_ref[0] = out


def _resident(shape):
    return pl.BlockSpec(shape, lambda *_: (0,) * len(shape), pipeline_mode=pl.Buffered(1))


def _ffn_block(h, mods, grp, attns, wouts, wgu, wd, final_norm, tm, tf):
    b, s, d = h.shape
    dff = wd.shape[0]
    n_attn = len(attns)
    final = final_norm is not None
    in_specs = [pl.BlockSpec((1, tm, d), lambda bi, i: (bi, i, 0)),
                pl.BlockSpec((1, 1, mods.shape[2]), lambda bi, i: (grp(bi), 0, 0))]
    in_specs += [pl.BlockSpec((1, tm, a.shape[2]), lambda bi, i: (bi, i, 0)) for a in attns]
    in_specs += [_resident(w.shape) for w in wouts]
    in_specs += [_resident(wgu.shape), _resident(wd.shape)]
    args = [h, mods, *attns, *wouts, wgu, wd]
    if final:
        in_specs.append(_resident((1, d)))
        args.append(final_norm)
    return pl.pallas_call(
        functools.partial(_ffn_kernel, n_attn=n_attn, tf=tf, final=final),
        out_shape=jax.ShapeDtypeStruct((b, s, d), F32),
        grid=(b, s // tm),
        in_specs=in_specs,
        out_specs=pl.BlockSpec((1, tm, d), lambda bi, i: (bi, i, 0)),
        scratch_shapes=[pltpu.VMEM((tm, d), BF16), pltpu.VMEM((tm, dff), BF16)],
        compiler_params=_cparams(("parallel", "parallel")),
        name="outproj_ffn",
    )(*args)


def _inproj1_kernel(x_ref, mod_ref, win_ref, cos_ref, sin_ref, q_ref, k_ref, v_ref):
    d = D_MODEL
    x = x_ref[0]
    a = (_rms(x) * (1.0 + mod_ref[0, :, d:2 * d]) + mod_ref[0, :, 0:d]).astype(BF16)
    p = _dot(a, win_ref[...])
    cos = cos_ref[...]
    sin = sin_ref[...]
    nq = SWA_HEADS * HEAD_DIM
    nkv = 2 * SWA_KV_HEADS * LANES
    quarter = HEAD_DIM // 4
    for t in range(nq // LANES):
        q = _rope_tile(p[:, t * LANES:(t + 1) * LANES], cos, sin, quarter)
        q_ref[0, :, t * LANES:(t + 1) * LANES] = (q * (HEAD_DIM ** -0.5 * LOG2_E)).astype(BF16)
    for t in range(nkv // LANES):
        k = _rope_tile(p[:, nq + t * LANES:nq + (t + 1) * LANES], cos, sin, quarter)
        k_ref[0, :, t * LANES:(t + 1) * LANES] = k.astype(BF16)
    v_ref[0] = p[:, nq + nkv:nq + 2 * nkv].astype(BF16)


def _inproj1(x, mods, grp, win, cos, sin, tm):
    b, s, d = x.shape
    nq = SWA_HEADS * HEAD_DIM
    nkv = 2 * SWA_KV_HEADS * LANES
    return pl.pallas_call(
        _inproj1_kernel,
        out_shape=(jax.ShapeDtypeStruct((b, s, nq), BF16),
                   jax.ShapeDtypeStruct((b, s, nkv), BF16),
                   jax.ShapeDtypeStruct((b, s, nkv), BF16)),
        grid=(b, s // tm),
        in_specs=[pl.BlockSpec((1, tm, d), lambda bi, i: (bi, i, 0)),
                  pl.BlockSpec((1, 1, mods.shape[2]), lambda bi, i: (grp(bi), 0, 0)),
                  pl.BlockSpec(win.shape, lambda bi, i: (0, 0)),
                  pl.BlockSpec((tm, LANES), lambda bi, i: (i, 0)),
                  pl.BlockSpec((tm, LANES), lambda bi, i: (i, 0))],
        out_specs=(pl.BlockSpec((1, tm, nq), lambda bi, i: (bi, i, 0)),
                   pl.BlockSpec((1, tm, nkv), lambda bi, i: (bi, i, 0)),
                   pl.BlockSpec((1, tm, nkv), lambda bi, i: (bi, i, 0))),
        compiler_params=_cparams(("parallel", "parallel")),
        name="inproj_odd",
    )(x, mods, win, cos, sin)


def _swa_kernel(q_ref, km_ref, k0_ref, kp_ref, vm_ref, v0_ref, vp_ref, kc_ref, vc_ref, sink_ref, o_ref, *, n_steps):
    step = pl.program_id(2)
    blk = SWA_BLOCK
    npair = q_ref.shape[2] // LANES
    kall = jnp.concatenate([km_ref[0], k0_ref[0], kp_ref[0]], axis=0)
    vall = jnp.concatenate([vm_ref[0], v0_ref[0], vp_ref[0]], axis=0)
    kc = kc_ref[0]
    vc = vc_ref[0]
    n_ctx = kc.shape[0]
    kc_st = jnp.concatenate([kc[:, :LANES], kc[:, LANES:]], axis=0)
    vc_st = jnp.concatenate([vc[:, :LANES], vc[:, LANES:]], axis=0)
    ind = _stack_pair((_pair_indicator(n_ctx, False), _pair_indicator(n_ctx, True)),
                      (_pair_indicator(3 * blk, False), _pair_indicator(3 * blk, True)))
    r = lax.broadcasted_iota(jnp.int32, (blk, 3 * blk), 0)
    c = lax.broadcasted_iota(jnp.int32, (blk, 3 * blk), 1)
    cc = c % blk
    lane = lax.broadcasted_iota(jnp.int32, (npair * blk, LANES), 1)
    lo = lane < HEAD_DIM

    def band(left_off, right_off):
        ok = ((c >= blk) & (c < 2 * blk)) | ((c < blk) & (cc >= r + left_off)) | ((c >= 2 * blk) & (cc <= r - right_off))
        neg = jnp.where(ok, 0.0, NEG_INF).astype(F32)
        return jnp.concatenate([neg] * npair, axis=0)

    for t in range(SWA_GROUP):
        left_off = jnp.where(step >= 1, 0, blk) if t == 0 else 0
        right_off = jnp.where(step <= n_steps - 2, 0, blk) if t == SWA_GROUP - 1 else 0
        neg = band(left_off, right_off)
        qb = q_ref[0, t * blk:(t + 1) * blk, :]
        qst = jnp.concatenate([qb[:, i * LANES:(i + 1) * LANES] for i in range(npair)], axis=0)
        kw = kall[t * blk:(t + 3) * blk]
        vw = vall[t * blk:(t + 3) * blk]
        k_st = jnp.concatenate([kc_st, kw[:, :LANES], kw[:, LANES:]], axis=0)
        v_st = jnp.concatenate([vc_st, vw[:, :LANES], vw[:, LANES:]], axis=0)
        s = _dot_nt(qst, k_st)
        p_ctx, p_lat, p_sink = [], [], []
        for par in range(2):
            sink = sink_ref[0, par]
            t_c = _tiles(s[:, par * n_ctx:(par + 1) * n_ctx])
            off = 2 * n_ctx + par * 3 * blk
            t_l = _tiles(s[:, off:off + 3 * blk] + neg)
            m = jnp.maximum(_rowmax128(t_c + t_l), sink)
            p_ctx += _exp_tiles(t_c, m)
            p_lat += _exp_tiles(t_l, m)
            p_sink.append(jnp.exp2(sink - m))
        o = _dot(jnp.concatenate(p_ctx + p_lat, axis=1), jnp.concatenate([v_st, ind], axis=1))
        o = o[:, :LANES] / (o[:, LANES:] + jnp.where(lo, p_sink[0], p_sink[1]))
        for i in range(npair):
            o_ref[0, t * blk:(t + 1) * blk, i * LANES:(i + 1) * LANES] = o[i * blk:(i + 1) * blk].astype(o_ref.dtype)


def _swa_attention(q, kvar, vvar, kc, vc, sink_tab):
    b, s, nq = q.shape
    blk = SWA_BLOCK
    nb = s // blk
    n_steps = nb // SWA_GROUP
    gw = nq // SWA_KV_HEADS
    n_ctx = kc.shape[1]
    prev = lambda bi, g, j: (bi, jnp.maximum(SWA_GROUP * j - 1, 0), g)
    cur = lambda bi, g, j: (bi, j, g)
    nxt = lambda bi, g, j: (bi, jnp.minimum(SWA_GROUP * (j + 1), nb - 1), g)
    edge_spec = lambda f: pl.BlockSpec((1, blk, 2 * LANES), f)
    main_spec = pl.BlockSpec((1, SWA_GROUP * blk, 2 * LANES), cur)
    ctx_spec = pl.BlockSpec((1, n_ctx, 2 * LANES), lambda bi, g, j: (bi, 0, g))
    return pl.pallas_call(
        functools.partial(_swa_kernel, n_steps=n_steps),
        out_shape=jax.ShapeDtypeStruct((b, s, nq), BF16),
        grid=(b, SWA_KV_HEADS, n_steps),
        in_specs=[pl.BlockSpec((1, SWA_GROUP * blk, gw), cur),
                  edge_spec(prev), main_spec, edge_spec(nxt),
                  edge_spec(prev), main_spec, edge_spec(nxt),
                  ctx_spec, ctx_spec,
                  pl.BlockSpec((1, 2, sink_tab.shape[2], LANES), lambda bi, g, j: (g, 0, 0, 0))],
        out_specs=pl.BlockSpec((1, SWA_GROUP * blk, gw), cur),
        compiler_params=_cparams(("parallel", "parallel", "arbitrary")),
        name="swa_attention",
    )(q, kvar, kvar, kvar, vvar, vvar, vvar, kc, vc, sink_tab)


def _rope_tables(s, dim, pad_to, first_lane=0):
    n_rows = s // GRID_W
    half = dim // 2
    inv = ROPE_BASE ** (-jnp.arange(0, half, 2, dtype=F32) / half)
    ar = jnp.arange(n_rows, dtype=jnp.int32).astype(F32)[:, None] * inv
    ac = jnp.arange(GRID_W, dtype=jnp.int32).astype(F32)[:, None] * inv
    expand_r = lambda a: jnp.repeat(a, GRID_W, axis=0)
    expand_c = lambda a: jnp.tile(a, (n_rows, 1))
    cos = jnp.concatenate([expand_r(jnp.cos(ar))] * 2 + [expand_c(jnp.cos(ac))] * 2, axis=1)
    sin = jnp.concatenate([expand_r(-jnp.sin(ar)), expand_r(jnp.sin(ar)),
                           expand_c(-jnp.sin(ac)), expand_c(jnp.sin(ac))], axis=1)
    if pad_to > dim:
        before, after = first_lane, pad_to - dim - first_lane
        cos = jnp.concatenate([jnp.ones((s, before), F32), cos, jnp.ones((s, after), F32)], axis=1)
        sin = jnp.concatenate([jnp.zeros((s, before), F32), sin, jnp.zeros((s, after), F32)], axis=1)
    reps = LANES // cos.shape[1]
    return jnp.tile(cos, (1, reps)), jnp.tile(sin, (1, reps))


def _even_weights(w_in, w_q_up, w_uk, w_uv, w_out):
    d = w_in.shape[0]
    n_lat = MLA_Q_RANK + MLA_KV_RANK
    win = jnp.concatenate([w_in[:, :n_lat], jnp.zeros((d, ROPE_LANE), w_in.dtype), w_in[:, n_lat:MLA_IN],
                           jnp.zeros((d, LANES - ROPE_LANE - MLA_ROPE), w_in.dtype), w_in[:, MLA_IN:]], axis=1)
    wq3 = w_q_up.reshape(MLA_Q_RANK, MLA_HEADS, MLA_NOPE + MLA_ROPE)
    nope = wq3[:, :, :MLA_NOPE].reshape(MLA_Q_RANK, MLA_HEADS * MLA_NOPE)
    rope = jnp.pad(wq3[:, :, MLA_NOPE:], ((0, 0), (0, 0), (ROPE_LANE, LANES - ROPE_LANE - MLA_ROPE)))
    rope = rope.reshape(MLA_Q_RANK, MLA_HEADS * LANES)
    wq = jnp.concatenate([nope, rope], axis=1)
    eye = jnp.eye(MLA_HEADS, dtype=w_uk.dtype)
    wuk = jnp.einsum('hcn,hg->hngc', w_uk, eye).reshape(MLA_HEADS * MLA_NOPE, MLA_HEADS * MLA_KV_RANK)
    wuv = jnp.einsum('hcv,hg->hcgv', w_uv, eye).reshape(MLA_HEADS * MLA_KV_RANK, MLA_HEADS * MLA_V)
    n_mla = MLA_HEADS * MLA_V
    return (win.astype(BF16), wq.astype(BF16), wuk.astype(BF16), wuv.astype(BF16),
            w_out[:n_mla].astype(BF16), w_out[n_mla:].astype(BF16))


def _odd_weights(w_in):
    d = w_in.shape[0]
    nq = SWA_HEADS * HEAD_DIM
    z = jnp.zeros((d, HEAD_DIM), w_in.dtype)

    def variants(off):
        cols = []
        for g in range(SWA_KV_HEADS):
            w = w_in[:, off + g * HEAD_DIM:off + (g + 1) * HEAD_DIM]
            cols += [w, z, z, w]
        return jnp.concatenate(cols, axis=1)

    kcols = variants(nq)
    vcols = variants(nq + SWA_KV_HEADS * HEAD_DIM)
    return jnp.concatenate([w_in[:, :nq], kcols, vcols], axis=1).astype(BF16)


def _sink_table(sinks):
    g = SWA_HEADS // SWA_KV_HEADS
    t = (sinks.astype(F32) * LOG2_E).reshape(SWA_KV_HEADS, g // 2, 2).transpose(0, 2, 1)
    t = jnp.broadcast_to(t[:, :, :, None, None], (SWA_KV_HEADS, 2, g // 2, SWA_BLOCK, LANES))
    return t.reshape(SWA_KV_HEADS, 2, (g // 2) * SWA_BLOCK, LANES)


def kernel(x, c, ctx, c_ctx, mod_w, mod_b, even_w_in, mla_q_norm, mla_kv_norm, mla_w_q_up, mla_w_uk, mla_w_uv,
           na_rel_bias, even_w_out, odd_w_in, swa_sinks, odd_w_out, ffn_w_gate_up, ffn_w_down, final_norm):
    b, s, d = x.shape
    n_ctx = ctx.shape[1]
    n_rows = s // GRID_W
    assert d == D_MODEL and s % (NA_STEP_BLOCKS * NA_RB * GRID_W) == 0 and s % (SWA_GROUP * SWA_BLOCK) == 0
    assert mod_w.shape[0] == 2 and b <= 4

    ctx_grp = 4
    c8 = jnp.zeros((8, d), F32).at[:b].set(c.astype(F32)).at[ctx_grp].set(c_ctx.astype(F32))
    mods = _modulation(c8, mod_w.astype(F32), mod_b.astype(F32))
    mods0 = mods[0].reshape(8, 1, 6 * d)
    mods1 = mods[1].reshape(8, 1, 6 * d)
    lat_grp = lambda bi: bi
    ctx_g = lambda bi: ctx_grp

    tm = min(512, s)
    tf = 256
    wgu = ffn_w_gate_up.astype(BF16)
    wdn = ffn_w_down.astype(BF16)

    win, wq, wuk, wuv, wo_mla, wo_na = _even_weights(even_w_in[0], mla_w_q_up[0], mla_w_uk[0], mla_w_uv[0],
                                                      even_w_out[0])
    qn = mla_q_norm[0].astype(F32).reshape(1, -1)
    kvn = mla_kv_norm[0].astype(F32).reshape(1, -1)
    cos_m, sin_m = _rope_tables(s, MLA_ROPE, LANES, ROPE_LANE)
    one_c = jnp.ones((n_ctx, LANES), F32)
    zero_c = jnp.zeros((n_ctx, LANES), F32)

    q_l, k_l, naq_l, nak_l, nav_l = _inproj0(x, mods0, lat_grp, win, qn, wq, wuk, kvn, cos_m, sin_m, tm)
    q_c, k_c, naq_c, nak_c, nav_c = _inproj0(ctx, mods0, ctx_g, win, qn, wq, wuk, kvn, one_c, zero_c, n_ctx)

    k_all = jnp.concatenate([k_c, k_l], axis=1)
    vt_all = jnp.swapaxes(k_all[:, :, :MLA_V_ROWS], 1, 2)
    wuvt = wuv.T
    o_mla_l = _mla_t_attention(q_l, k_all, vt_all, wuvt, min(1024, s))
    o_mla_c = _mla_t_attention(q_c, k_c, vt_all[:, :, :n_ctx], wuvt, n_ctx)

    bias = _na_bias_table(na_rel_bias[0], n_rows)
    o_na_l = _na_attention(naq_l, nak_l, nav_l, nak_c, nav_c, bias)
    o_na_c = _pair_ctx_attention(naq_c, nak_c, nav_c)

    h_lat = _ffn_block(x, mods0, lat_grp, [o_mla_l, o_na_l], [wo_mla, wo_na], wgu[0], wdn[0], None, tm, tf)
    h_ctx = _ffn_block(ctx, mods0, ctx_g, [o_mla_c, o_na_c], [wo_mla, wo_na], wgu[0], wdn[0], None, n_ctx, tf)

    win1 = _odd_weights(odd_w_in[0])
    cos_s, sin_s = _rope_tables(s, HEAD_DIM, HEAD_DIM)
    q1, k1, v1 = _inproj1(h_lat, mods1, lat_grp, win1, cos_s, sin_s, tm)
    _, k1c, v1c = _inproj1(h_ctx, mods1, ctx_g, win1, one_c, zero_c, n_ctx)
    o_swa = _swa_attention(q1, k1, v1, k1c, v1c, _sink_table(swa_sinks[0]))
    fn = final_norm.astype(F32).reshape(1, d)
    return _ffn_block(h_lat, mods1, lat_grp, [o_swa], [odd_w_out[0].astype(BF16)], wgu[1], wdn[1], fn, tm, tf)
```

```python
import functools

import numpy as np
import jax
import jax.numpy as jnp
from jax import lax
from jax.experimental import pallas as pl
from jax.experimental.pallas import tpu as pltpu

F32 = jnp.float32
BF16 = jnp.bfloat16

D_MODEL = 1024
GRID_W = 64
HEAD_DIM = 64
ROPE_BASE = 10000.0
EPS = 1e-6
NEG_INF = -1e30
LOG2_E = 1.4426950408889634

MLA_HEADS = 8
MLA_NOPE = 64
MLA_ROPE = 32
MLA_V = 64
MLA_Q_RANK = 256
MLA_KV_RANK = 128
MLA_IN = MLA_Q_RANK + MLA_KV_RANK + MLA_ROPE

NA_HEADS = 8
NA_KR = 8
NA_KC = 16
NA_RB = 4
NA_SPAN = 12
NA_STEP_BLOCKS = 8

SWA_HEADS = 16
SWA_KV_HEADS = 2
SWA_WINDOW = 128
SWA_BLOCK = 128
SWA_GROUP = 8

LANES = 128
MLA_QK_PAD = 256
MLA_ITEM_ROWS = 256
MLA_KEY_CHUNK = 768
ONES_LANE = 0
ROPE_LANE = 8
MLA_V_ROWS = 144
VMEM_LIMIT = 56 * 1024 * 1024


def _cparams(sem):
    return pltpu.CompilerParams(dimension_semantics=sem, vmem_limit_bytes=VMEM_LIMIT)


def _dot(a, b):
    return jnp.dot(a, b, preferred_element_type=F32)


def _dot_nt(a, b):
    return lax.dot_general(a, b, (((1,), (1,)), ((), ())), preferred_element_type=F32)


def _rms(x):
    return x * lax.rsqrt(jnp.mean(x * x, axis=-1, keepdims=True) + EPS)


def _silu(x):
    return x * (1.0 / (1.0 + jnp.exp(-x)))


def _rope_tile(x, cos, sin, quarter, first_lane=0):
    lane = lax.broadcasted_iota(jnp.int32, x.shape, 1)
    first = ((lane + (2 * quarter - first_lane)) % (2 * quarter)) < quarter
    swapped = jnp.where(first, pltpu.roll(x, LANES - quarter, 1), pltpu.roll(x, quarter, 1))
    return x * cos + swapped * sin


def _zero_of(x):
    u = lax.bitcast_convert_type(x, jnp.uint32)
    u = lax.shift_right_logical(lax.shift_right_logical(u, jnp.uint32(16)), jnp.uint32(16))
    return lax.bitcast_convert_type(u, F32)


def _tiles(s):
    return [s[:, j * LANES:(j + 1) * LANES] for j in range(s.shape[1] // LANES)]


def _rowmax128(tiles):
    mx = tiles[0]
    for t in tiles[1:]:
        mx = jnp.maximum(mx, t)
    return jnp.broadcast_to(jnp.max(mx, axis=1, keepdims=True), mx.shape)


def _exp_tiles(tiles, m):
    return [jnp.exp2(t - m).astype(BF16) for t in tiles]


def _pair_indicator(n, hi):
    lane = lax.broadcasted_iota(jnp.int32, (n, LANES), 1)
    return jnp.where((lane >= HEAD_DIM) == hi, 1.0, 0.0).astype(BF16)


def _mod_kernel(c_ref, w_ref, b_ref, o_ref):
    a = _silu(c_ref[...])
    o_ref[0] = jnp.dot(a, w_ref[0], precision=lax.Precision.HIGHEST, preferred_element_type=F32) + b_ref[0]


def _modulation(c8, mod_w, mod_b):
    depth, d, n = mod_w.shape
    tn = 1536
    return pl.pallas_call(
        _mod_kernel,
        out_shape=jax.ShapeDtypeStruct((depth, 8, n), F32),
        grid=(depth, n // tn),
        in_specs=[pl.BlockSpec((8, d), lambda l, j: (0, 0)),
                  pl.BlockSpec((1, d, tn), lambda l, j: (l, 0, j)),
                  pl.BlockSpec((1, 1, tn), lambda l, j: (l, 0, j))],
        out_specs=pl.BlockSpec((1, 8, tn), lambda l, j: (l, 0, j)),
        compiler_params=_cparams(("parallel", "parallel")),
        name="modulation",
    )(c8, mod_w, mod_b.reshape(depth, 1, n))


def _inproj0_kernel(x_ref, mod_ref, win_ref, qn_ref, wq_ref, wuk_ref, kvn_ref, cos_ref, sin_ref,
                    qmla_ref, kmla_ref, naq_ref, nak_ref, nav_ref):
    d = D_MODEL
    x = x_ref[0]
    shift = mod_ref[0, :, 0:d]
    scale = mod_ref[0, :, d:2 * d]
    a = (_rms(x) * (1.0 + scale) + shift).astype(BF16)
    p = _dot(a, win_ref[...])
    cos = cos_ref[...]
    sin = sin_ref[...]
    mla_scale = (MLA_NOPE + MLA_ROPE) ** -0.5 * LOG2_E

    cq = (_rms(p[:, 0:MLA_Q_RANK]) * qn_ref[...]).astype(BF16)
    q = _dot(cq, wq_ref[...])
    n_nope = MLA_HEADS * MLA_NOPE
    q_lat = _dot(q[:, 0:n_nope].astype(BF16), wuk_ref[...])
    for h in range(MLA_HEADS):
        qr = _rope_tile(q[:, n_nope + h * LANES:n_nope + (h + 1) * LANES], cos, sin, MLA_ROPE // 4, ROPE_LANE)
        qmla_ref[0, h, :, 0:LANES] = (q_lat[:, h * LANES:(h + 1) * LANES] * mla_scale).astype(BF16)
        qmla_ref[0, h, :, LANES:2 * LANES] = (qr * mla_scale).astype(BF16)

    ckv = _rms(p[:, MLA_Q_RANK:MLA_Q_RANK + MLA_KV_RANK]) * kvn_ref[...]
    kr = _rope_tile(p[:, 3 * LANES:4 * LANES], cos, sin, MLA_ROPE // 4, ROPE_LANE)
    lane = lax.broadcasted_iota(jnp.int32, kr.shape, 1)
    kr = jnp.where(lane == ONES_LANE, 1.0, kr)
    kmla_ref[0, :, 0:LANES] = ckv.astype(BF16)
    kmla_ref[0, :, LANES:2 * LANES] = kr.astype(BF16)

    w = NA_HEADS * HEAD_DIM
    naq_ref[0] = (p[:, 4 * LANES:4 * LANES + w] * (HEAD_DIM ** -0.5 * LOG2_E)).astype(BF16)
    nak_ref[0] = p[:, 4 * LANES + w:4 * LANES + 2 * w].astype(BF16)
    nav_ref[0] = p[:, 4 * LANES + 2 * w:4 * LANES + 3 * w].astype(BF16)


def _inproj0(x, mods, grp, win, qn, wq, wuk, kvn, cos, sin, tm):
    b, s, d = x.shape
    nt = s // tm
    w = NA_HEADS * HEAD_DIM
    const = lambda bi, i: (0, 0)
    return pl.pallas_call(
        _inproj0_kernel,
        out_shape=(jax.ShapeDtypeStruct((b, MLA_HEADS, s, MLA_QK_PAD), BF16),
                   jax.ShapeDtypeStruct((b, s, MLA_QK_PAD), BF16),
                   jax.ShapeDtypeStruct((b, s, w), BF16),
                   jax.ShapeDtypeStruct((b, s, w), BF16),
                   jax.ShapeDtypeStruct((b, s, w), BF16)),
        grid=(b, nt),
        in_specs=[pl.BlockSpec((1, tm, d), lambda bi, i: (bi, i, 0)),
                  pl.BlockSpec((1, 1, mods.shape[2]), lambda bi, i: (grp(bi), 0, 0)),
                  pl.BlockSpec(win.shape, const),
                  pl.BlockSpec(qn.shape, const),
                  pl.BlockSpec(wq.shape, const),
                  pl.BlockSpec(wuk.shape, const),
                  pl.BlockSpec(kvn.shape, const),
                  pl.BlockSpec((tm, LANES), lambda bi, i: (i, 0)),
                  pl.BlockSpec((tm, LANES), lambda bi, i: (i, 0))],
        out_specs=(pl.BlockSpec((1, MLA_HEADS, tm, MLA_QK_PAD), lambda bi, i: (bi, 0, i, 0)),
                   pl.BlockSpec((1, tm, MLA_QK_PAD), lambda bi, i: (bi, i, 0)),
                   pl.BlockSpec((1, tm, w), lambda bi, i: (bi, i, 0)),
                   pl.BlockSpec((1, tm, w), lambda bi, i: (bi, i, 0)),
                   pl.BlockSpec((1, tm, w), lambda bi, i: (bi, i, 0))),
        compiler_params=_cparams(("parallel", "parallel")),
        name="inproj_even",
    )(x, mods, win, qn, wq, wuk, kvn, cos, sin)


def _mla_t_kernel(q_ref, k_ref, vt_ref, wuvt_ref, o_ref, s0_ref, s1_ref, p0_ref, p1_ref, m0_ref, m1_ref, oh_ref):
    n_heads, tq = q_ref.shape[1], q_ref.shape[2]
    nk, rc = s0_ref.shape
    n_items = (tq // rc) * n_heads
    s_refs, p_refs, m_refs = (s0_ref, s1_ref), (p0_ref, p1_ref), (m0_ref, m1_ref)
    tk = MLA_KEY_CHUNK if nk % MLA_KEY_CHUNK == 0 else 2 * LANES
    rg = 128

    def item(i):
        if isinstance(i, int):
            return i % n_heads, i // n_heads
        return lax.rem(i, n_heads), lax.div(i, n_heads)

    def round_(r, par):
        static = isinstance(r, int)
        do_qk = not static or r < n_items
        do_sm = not static or 1 <= r <= n_items
        do_pv = not static or 2 <= r <= n_items + 1
        if do_qk:
            h_q, c_q = item(r)
            row_q = c_q * rc if static else pl.multiple_of(c_q * rc, rc)
            q = q_ref[0, h_q, pl.ds(row_q, rc), :]
        if do_sm:
            m8 = m_refs[1 - par][...]
        if do_pv:
            acc = None
        for c in range(nk // tk):
            keys = slice(c * tk, (c + 1) * tk)
            if do_qk:
                st = _dot_nt(k_ref[0, keys, :], q)
                s_refs[par][keys, :] = st
                cm = jnp.max(st.reshape(tk // 8, 8, rc), axis=0)
                m_refs[par][...] = cm if c == 0 else jnp.maximum(m_refs[par][...], cm)
            if do_sm:
                for g in range(tk // rg):
                    m = m8 + _zero_of(st[g * rg:g * rg + 8, :]) if do_qk else m8
                    blk = slice(c * tk + g * rg, c * tk + (g + 1) * rg)
                    sb = s_refs[1 - par][blk, :].reshape(rg // 8, 8, rc)
                    p_refs[1 - par][blk, :] = jnp.exp2(sb - m[None]).reshape(rg, rc).astype(BF16)
            if do_pv:
                part = _dot(vt_ref[0, :, keys], p_refs[par][keys, :])
                acc = part if acc is None else acc + part
        if do_qk:
            mx = m_refs[par][...]
            m_refs[par][...] = jnp.broadcast_to(jnp.max(mx, axis=0, keepdims=True), mx.shape)
        if do_pv:
            h_v, c_v = item(r - 2)
            l = acc[MLA_KV_RANK + ONES_LANE:MLA_KV_RANK + ONES_LANE + 1, :]
            oh_ref[c_v, h_v] = (acc[0:MLA_KV_RANK, :] / l).astype(BF16)

    round_(0, 0)
    round_(1, 1)
    n_full = max(n_items - 2, 0)

    def pair(i, carry):
        r = 2 + 2 * i
        round_(r, 0)
        round_(r + 1, 1)
        return carry

    lax.fori_loop(0, n_full // 2, pair, 0)
    for r in range(2 + 2 * (n_full // 2), n_items):
        round_(r, r % 2)
    for r in range(max(n_items, 2), n_items + 2):
        round_(r, r % 2)

    for ci in range(tq // rc):
        ot = jnp.concatenate([oh_ref[ci, h] for h in range(n_heads)], axis=0)
        out_t = _dot(wuvt_ref[...], ot)
        o_ref[0, ci * rc:(ci + 1) * rc, :] = out_t.T.astype(o_ref.dtype)


def _mla_t_attention(q, k, vt, wuvt, tq):
    b, h, sq, _ = q.shape
    nk = k.shape[1]
    rc = min(MLA_ITEM_ROWS, tq)
    assert nk % (2 * LANES) == 0 and tq % rc == 0 and sq % tq == 0
    wo = wuvt.shape[0]
    one_buffer = pl.Buffered(1)
    return pl.pallas_call(
        _mla_t_kernel,
        out_shape=jax.ShapeDtypeStruct((b, sq, wo), BF16),
        grid=(b, sq // tq),
        in_specs=[pl.BlockSpec((1, h, tq, MLA_QK_PAD), lambda bi, i: (bi, 0, i, 0)),
                  pl.BlockSpec((1, nk, MLA_QK_PAD), lambda bi, i: (bi, 0, 0), pipeline_mode=one_buffer),
                  pl.BlockSpec((1, MLA_V_ROWS, nk), lambda bi, i: (bi, 0, 0), pipeline_mode=one_buffer),
                  _resident(wuvt.shape)],
        out_specs=pl.BlockSpec((1, tq, wo), lambda bi, i: (bi, i, 0)),
        scratch_shapes=[pltpu.VMEM((nk, rc), F32), pltpu.VMEM((nk, rc), F32),
                        pltpu.VMEM((nk, rc), BF16), pltpu.VMEM((nk, rc), BF16),
                        pltpu.VMEM((8, rc), F32), pltpu.VMEM((8, rc), F32),
                        pltpu.VMEM((tq // rc, h, MLA_KV_RANK, rc), BF16)],
        compiler_params=_cparams(("parallel", "arbitrary")),
        name="mla_attention",
    )(q, k, vt, wuvt)


def _split_heads(t):
    lane = lax.broadcasted_iota(jnp.int32, t.shape, 1)
    lo = lane < HEAD_DIM
    zero = jnp.zeros_like(t)
    return jnp.where(lo, t, zero), jnp.where(lo, zero, t)


def _stack_pair(lo_hi_ctx, lo_hi_lat):
    return jnp.concatenate([lo_hi_ctx[0], lo_hi_ctx[1], lo_hi_lat[0], lo_hi_lat[1]], axis=0)


def _na_kernel(q_ref, k_ref, v_ref, kc_ref, vc_ref, bias_ref, o_ref, *, n_rows):
    step = pl.program_id(2)
    nrb = n_rows // NA_RB
    n_keys = NA_SPAN * GRID_W
    n_ctx = kc_ref.shape[1]
    tq = NA_RB * GRID_W
    kcs = _split_heads(kc_ref[0])
    vcs = _split_heads(vc_ref[0])
    ind = _stack_pair((_pair_indicator(n_ctx, False), _pair_indicator(n_ctx, True)),
                      (_pair_indicator(n_keys, False), _pair_indicator(n_keys, True)))
    for i in range(NA_STEP_BLOCKS):
        rb = step * NA_STEP_BLOCKS + i
        case = jnp.minimum(rb, 1) + jnp.maximum(rb - (nrb - 2), 0)
        base = jnp.clip(rb * NA_RB - NA_KR // 2, 0, n_rows - NA_SPAN)
        start = pl.multiple_of(base * GRID_W, 256)
        q = q_ref[0, i * tq:(i + 1) * tq, :]
        k_st = _stack_pair(kcs, _split_heads(k_ref[0, pl.ds(start, n_keys), :]))
        v_st = _stack_pair(vcs, _split_heads(v_ref[0, pl.ds(start, n_keys), :]))
        s = _dot_nt(q, k_st)
        p_ctx, p_lat = [], []
        for par in range(2):
            t_c = _tiles(s[:, par * n_ctx:(par + 1) * n_ctx])
            off = 2 * n_ctx + par * n_keys
            t_l = _tiles(s[:, off:off + n_keys] + bias_ref[case, par])
            m = _rowmax128(t_c + t_l)
            p_ctx += _exp_tiles(t_c, m)
            p_lat += _exp_tiles(t_l, m)
        o = _dot(jnp.concatenate(p_ctx + p_lat, axis=1), jnp.concatenate([v_st, ind], axis=1))
        o_ref[0, i * tq:(i + 1) * tq, :] = (o[:, :LANES] / o[:, LANES:]).astype(o_ref.dtype)


def _na_attention(q, k, v, kc, vc, bias):
    b, s, w = q.shape
    n_rows = s // GRID_W
    nrb = n_rows // NA_RB
    tq = NA_STEP_BLOCKS * NA_RB * GRID_W
    n_ctx = kc.shape[1]
    return pl.pallas_call(
        functools.partial(_na_kernel, n_rows=n_rows),
        out_shape=jax.ShapeDtypeStruct((b, s, w), BF16),
        grid=(w // LANES, b, nrb // NA_STEP_BLOCKS),
        in_specs=[pl.BlockSpec((1, tq, LANES), lambda j, bi, r: (bi, r, j)),
                  pl.BlockSpec((1, s, LANES), lambda j, bi, r: (bi, 0, j)),
                  pl.BlockSpec((1, s, LANES), lambda j, bi, r: (bi, 0, j)),
                  pl.BlockSpec((1, n_ctx, LANES), lambda j, bi, r: (bi, 0, j)),
                  pl.BlockSpec((1, n_ctx, LANES), lambda j, bi, r: (bi, 0, j)),
                  pl.BlockSpec((3, 2, NA_RB * GRID_W, NA_SPAN * GRID_W), lambda j, bi, r: (0, j, 0, 0))],
        out_specs=pl.BlockSpec((1, tq, LANES), lambda j, bi, r: (bi, r, j)),
        compiler_params=_cparams(("parallel", "parallel", "arbitrary")),
        name="na_attention",
    )(q, k, v, kc, vc, bias)


def _pair_ctx_kernel(q_ref, kc_ref, vc_ref, o_ref):
    q = q_ref[0]
    n_ctx = kc_ref.shape[1]
    kcs = _split_heads(kc_ref[0])
    vcs = _split_heads(vc_ref[0])
    s = _dot_nt(q, jnp.concatenate(kcs, axis=0))
    p = []
    for par in range(2):
        t_c = _tiles(s[:, par * n_ctx:(par + 1) * n_ctx])
        p += _exp_tiles(t_c, _rowmax128(t_c))
    ind = jnp.concatenate([_pair_indicator(n_ctx, False), _pair_indicator(n_ctx, True)], axis=0)
    o = _dot(jnp.concatenate(p, axis=1), jnp.concatenate([jnp.concatenate(vcs, axis=0), ind], axis=1))
    o_ref[0] = (o[:, :LANES] / o[:, LANES:]).astype(o_ref.dtype)


def _pair_ctx_attention(q, kc, vc):
    b, n, w = q.shape
    spec = pl.BlockSpec((1, n, LANES), lambda bi, j: (bi, 0, j))
    return pl.pallas_call(
        _pair_ctx_kernel,
        out_shape=jax.ShapeDtypeStruct((b, n, w), BF16),
        grid=(b, w // LANES),
        in_specs=[spec, spec, spec],
        out_specs=spec,
        compiler_params=_cparams(("parallel", "parallel")),
        name="na_ctx_attention",
    )(q, kc, vc)


def _na_bias_table(rel_bias, n_rows):
    qc = np.arange(GRID_W)
    cstart = np.clip(qc - NA_KC // 2, 0, GRID_W - NA_KC)
    kc = np.arange(GRID_W)
    col_ok = (kc[None, :] >= cstart[:, None]) & (kc[None, :] < cstart[:, None] + NA_KC)
    n_dr, n_dc = 2 * NA_KR - 1, 2 * NA_KC - 1
    nh = rel_bias.shape[0]
    dc = kc[None, :] - qc[:, None] + NA_KC - 1
    sel_col = ((dc[None] == np.arange(n_dc)[:, None, None]) & col_ok[None]).astype(np.float32)
    toe = jnp.einsum('hdu,uqk->hdqk', rel_bias.astype(F32), jnp.asarray(sel_col), precision=lax.Precision.HIGHEST)
    toe = jnp.where(jnp.asarray(col_ok)[None, None], toe * LOG2_E, NEG_INF)
    toe = jnp.concatenate([toe, jnp.full((nh, 1, GRID_W, GRID_W), NEG_INF, F32)], axis=1)
    toe = jnp.concatenate([toe, toe], axis=-1)
    idx = np.full((3, NA_RB, NA_SPAN), n_dr, np.int32)
    nrb = n_rows // NA_RB
    for case, rb in enumerate((0, min(1, nrb - 1), nrb - 1)):
        r0 = rb * NA_RB
        base = int(np.clip(r0 - NA_KR // 2, 0, n_rows - NA_SPAN))
        for a in range(NA_RB):
            r = r0 + a
            rs = int(np.clip(r - NA_KR // 2, 0, n_rows - NA_KR))
            for t in range(NA_SPAN):
                kr = base + t
                if rs <= kr < rs + NA_KR:
                    idx[case, a, t] = kr - r + NA_KR - 1

    def assemble(toe_ref, o_ref):
        lo = lax.broadcasted_iota(jnp.int32, (GRID_W, LANES), 1) < GRID_W
        for case in range(3):
            for a in range(NA_RB):
                for tp in range(NA_SPAN // 2):
                    d0, d1 = int(idx[case, a, 2 * tp]), int(idx[case, a, 2 * tp + 1])
                    tile = toe_ref[0, d0] if d0 == d1 else jnp.where(lo, toe_ref[0, d0], toe_ref[0, d1])
                    o_ref[case, 0, a * GRID_W:(a + 1) * GRID_W, tp * LANES:(tp + 1) * LANES] = tile

    tq, tkeys = NA_RB * GRID_W, NA_SPAN * GRID_W
    return pl.pallas_call(
        assemble,
        out_shape=jax.ShapeDtypeStruct((3, nh, tq, tkeys), F32),
        grid=(nh,),
        in_specs=[pl.BlockSpec((1, n_dr + 1, GRID_W, LANES), lambda h: (h, 0, 0, 0))],
        out_specs=pl.BlockSpec((3, 1, tq, tkeys), lambda h: (0, h, 0, 0)),
        compiler_params=_cparams(("parallel",)),
        name="na_bias_table",
    )(toe)


def _ffn_kernel(*refs, n_attn, tf, final):
    h_ref, mod_ref = refs[0], refs[1]
    attn_refs = refs[2:2 + n_attn]
    wout_refs = refs[2 + n_attn:2 + 2 * n_attn]
    wgu_ref, wd_ref = refs[2 + 2 * n_attn:4 + 2 * n_attn]
    pos = 4 + 2 * n_attn
    fn_ref = refs[pos] if final else None
    pos += 1 if final else 0
    o_ref, a2_ref, f_ref = refs[pos:pos + 3]
    d = D_MODEL
    dff = wd_ref.shape[0]

    proj = _dot(attn_refs[0][0], wout_refs[0][...])
    for t in range(1, n_attn):
        proj = proj + _dot(attn_refs[t][0], wout_refs[t][...])
    h1 = h_ref[0] + mod_ref[0, :, 2 * d:3 * d] * proj
    o_ref[0] = h1
    a2_ref[...] = (_rms(h1) * (1.0 + mod_ref[0, :, 4 * d:5 * d]) + mod_ref[0, :, 3 * d:4 * d]).astype(BF16)

    for j in range(dff // tf):
        a2 = a2_ref[...]
        g = _dot(a2, wgu_ref[:, j * tf:(j + 1) * tf])
        u = _dot(a2, wgu_ref[:, dff + j * tf:dff + (j + 1) * tf])
        f_ref[:, j * tf:(j + 1) * tf] = (_silu(g) * u).astype(BF16)

    out = o_ref[0] + mod_ref[0, :, 5 * d:6 * d] * _dot(f_ref[...], wd_ref[...])
    if final:
        out = _rms(out) * fn_ref[...]
    o_ref[0] = out


def _resident(shape):
    return pl.BlockSpec(shape, lambda *_: (0,) * len(shape), pipeline_mode=pl.Buffered(1))


def _ffn_block(h, mods, grp, attns, wouts, wgu, wd, final_norm, tm, tf):
    b, s, d = h.shape
    dff = wd.shape[0]
    n_attn = len(attns)
    final = final_norm is not None
    in_specs = [pl.BlockSpec((1, tm, d), lambda bi, i: (bi, i, 0)),
                pl.BlockSpec((1, 1, mods.shape[2]), lambda bi, i: (grp(bi), 0, 0))]
    in_specs += [pl.BlockSpec((1, tm, a.shape[2]), lambda bi, i: (bi, i, 0)) for a in attns]
    in_specs += [_resident(w.shape) for w in wouts]
    in_specs += [_resident(wgu.shape), _resident(wd.shape)]
    args = [h, mods, *attns, *wouts, wgu, wd]
    if final:
        in_specs.append(_resident((1, d)))
        args.append(final_norm)
    return pl.pallas_call(
        functools.partial(_ffn_kernel, n_attn=n_attn, tf=tf, final=final),
        out_shape=jax.ShapeDtypeStruct((b, s, d), F32),
        grid=(b, s // tm),
        in_specs=in_specs,
        out_specs=pl.BlockSpec((1, tm, d), lambda bi, i: (bi, i, 0)),
        scratch_shapes=[pltpu.VMEM((tm, d), BF16), pltpu.VMEM((tm, dff), BF16)],
        compiler_params=_cparams(("parallel", "parallel")),
        name="outproj_ffn",
    )(*args)


def _inproj1_kernel(x_ref, mod_ref, win_ref, cos_ref, sin_ref, q_ref, k_ref, v_ref):
    d = D_MODEL
    x = x_ref[0]
    a = (_rms(x) * (1.0 + mod_ref[0, :, d:2 * d]) + mod_ref[0, :, 0:d]).astype(BF16)
    p = _dot(a, win_ref[...])
    cos = cos_ref[...]
    sin = sin_ref[...]
    nq = SWA_HEADS * HEAD_DIM
    nkv = 2 * SWA_KV_HEADS * LANES
    quarter = HEAD_DIM // 4
    for t in range(nq // LANES):
        q = _rope_tile(p[:, t * LANES:(t + 1) * LANES], cos, sin, quarter)
        q_ref[0, :, t * LANES:(t + 1) * LANES] = (q * (HEAD_DIM ** -0.5 * LOG2_E)).astype(BF16)
    for t in range(nkv // LANES):
        k = _rope_tile(p[:, nq + t * LANES:nq + (t + 1) * LANES], cos, sin, quarter)
        k_ref[0, :, t * LANES:(t + 1) * LANES] = k.astype(BF16)
    v_ref[0] = p[:, nq + nkv:nq + 2 * nkv].astype(BF16)


def _inproj1(x, mods, grp, win, cos, sin, tm):
    b, s, d = x.shape
    nq = SWA_HEADS * HEAD_DIM
    nkv = 2 * SWA_KV_HEADS * LANES
    return pl.pallas_call(
        _inproj1_kernel,
        out_shape=(jax.ShapeDtypeStruct((b, s, nq), BF16),
                   jax.ShapeDtypeStruct((b, s, nkv), BF16),
                   jax.ShapeDtypeStruct((b, s, nkv), BF16)),
        grid=(b, s // tm),
        in_specs=[pl.BlockSpec((1, tm, d), lambda bi, i: (bi, i, 0)),
                  pl.BlockSpec((1, 1, mods.shape[2]), lambda bi, i: (grp(bi), 0, 0)),
                  pl.BlockSpec(win.shape, lambda bi, i: (0, 0)),
                  pl.BlockSpec((tm, LANES), lambda bi, i: (i, 0)),
                  pl.BlockSpec((tm, LANES), lambda bi, i: (i, 0))],
        out_specs=(pl.BlockSpec((1, tm, nq), lambda bi, i: (bi, i, 0)),
                   pl.BlockSpec((1, tm, nkv), lambda bi, i: (bi, i, 0)),
                   pl.BlockSpec((1, tm, nkv), lambda bi, i: (bi, i, 0))),
        compiler_params=_cparams(("parallel", "parallel")),
        name="inproj_odd",
    )(x, mods, win, cos, sin)


def _swa_kernel(q_ref, km_ref, k0_ref, kp_ref, vm_ref, v0_ref, vp_ref, kc_ref, vc_ref, sink_ref, o_ref, *, n_steps):
    step = pl.program_id(2)
    blk = SWA_BLOCK
    npair = q_ref.shape[2] // LANES
    kall = jnp.concatenate([km_ref[0], k0_ref[0], kp_ref[0]], axis=0)
    vall = jnp.concatenate([vm_ref[0], v0_ref[0], vp_ref[0]], axis=0)
    kc = kc_ref[0]
    vc = vc_ref[0]
    n_ctx = kc.shape[0]
    kc_st = jnp.concatenate([kc[:, :LANES], kc[:, LANES:]], axis=0)
    vc_st = jnp.concatenate([vc[:, :LANES], vc[:, LANES:]], axis=0)
    ind = _stack_pair((_pair_indicator(n_ctx, False), _pair_indicator(n_ctx, True)),
                      (_pair_indicator(3 * blk, False), _pair_indicator(3 * blk, True)))
    r = lax.broadcasted_iota(jnp.int32, (blk, 3 * blk), 0)
    c = lax.broadcasted_iota(jnp.int32, (blk, 3 * blk), 1)
    cc = c % blk
    lane = lax.broadcasted_iota(jnp.int32, (npair * blk, LANES), 1)
    lo = lane < HEAD_DIM

    def band(left_off, right_off):
        ok = ((c >= blk) & (c < 2 * blk)) | ((c < blk) & (cc >= r + left_off)) | ((c >= 2 * blk) & (cc <= r - right_off))
        neg = jnp.where(ok, 0.0, NEG_INF).astype(F32)
        return jnp.concatenate([neg] * npair, axis=0)

    for t in range(SWA_GROUP):
        left_off = jnp.where(step >= 1, 0, blk) if t == 0 else 0
        right_off = jnp.where(step <= n_steps - 2, 0, blk) if t == SWA_GROUP - 1 else 0
        neg = band(left_off, right_off)
        qb = q_ref[0, t * blk:(t + 1) * blk, :]
        qst = jnp.concatenate([qb[:, i * LANES:(i + 1) * LANES] for i in range(npair)], axis=0)
        kw = kall[t * blk:(t + 3) * blk]
        vw = vall[t * blk:(t + 3) * blk]
        k_st = jnp.concatenate([kc_st, kw[:, :LANES], kw[:, LANES:]], axis=0)
        v_st = jnp.concatenate([vc_st, vw[:, :LANES], vw[:, LANES:]], axis=0)
        s = _dot_nt(qst, k_st)
        p_ctx, p_lat, p_sink = [], [], []
        for par in range(2):
            sink = sink_ref[0, par]
            t_c = _tiles(s[:, par * n_ctx:(par + 1) * n_ctx])
            off = 2 * n_ctx + par * 3 * blk
            t_l = _tiles(s[:, off:off + 3 * blk] + neg)
            m = jnp.maximum(_rowmax128(t_c + t_l), sink)
            p_ctx += _exp_tiles(t_c, m)
            p_lat += _exp_tiles(t_l, m)
            p_sink.append(jnp.exp2(sink - m))
        o = _dot(jnp.concatenate(p_ctx + p_lat, axis=1), jnp.concatenate([v_st, ind], axis=1))
        o = o[:, :LANES] / (o[:, LANES:] + jnp.where(lo, p_sink[0], p_sink[1]))
        for i in range(npair):
            o_ref[0, t * blk:(t + 1) * blk, i * LANES:(i + 1) * LANES] = o[i * blk:(i + 1) * blk].astype(o_ref.dtype)


def _swa_attention(q, kvar, vvar, kc, vc, sink_tab):
    b, s, nq = q.shape
    blk = SWA_BLOCK
    nb = s // blk
    n_steps = nb // SWA_GROUP
    gw = nq // SWA_KV_HEADS
    n_ctx = kc.shape[1]
    prev = lambda bi, g, j: (bi, jnp.maximum(SWA_GROUP * j - 1, 0), g)
    cur = lambda bi, g, j: (bi, j, g)
    nxt = lambda bi, g, j: (bi, jnp.minimum(SWA_GROUP * (j + 1), nb - 1), g)
    edge_spec = lambda f: pl.BlockSpec((1, blk, 2 * LANES), f)
    main_spec = pl.BlockSpec((1, SWA_GROUP * blk, 2 * LANES), cur)
    ctx_spec = pl.BlockSpec((1, n_ctx, 2 * LANES), lambda bi, g, j: (bi, 0, g))
    return pl.pallas_call(
        functools.partial(_swa_kernel, n_steps=n_steps),
        out_shape=jax.ShapeDtypeStruct((b, s, nq), BF16),
        grid=(b, SWA_KV_HEADS, n_steps),
        in_specs=[pl.BlockSpec((1, SWA_GROUP * blk, gw), cur),
                  edge_spec(prev), main_spec, edge_spec(nxt),
                  edge_spec(prev), main_spec, edge_spec(nxt),
                  ctx_spec, ctx_spec,
                  pl.BlockSpec((1, 2, sink_tab.shape[2], LANES), lambda bi, g, j: (g, 0, 0, 0))],
        out_specs=pl.BlockSpec((1, SWA_GROUP * blk, gw), cur),
        compiler_params=_cparams(("parallel", "parallel", "arbitrary")),
        name="swa_attention",
    )(q, kvar, kvar, kvar, vvar, vvar, vvar, kc, vc, sink_tab)


def _rope_tables(s, dim, pad_to, first_lane=0):
    n_rows = s // GRID_W
    half = dim // 2
    inv = ROPE_BASE ** (-jnp.arange(0, half, 2, dtype=F32) / half)
    ar = jnp.arange(n_rows, dtype=jnp.int32).astype(F32)[:, None] * inv
    ac = jnp.arange(GRID_W, dtype=jnp.int32).astype(F32)[:, None] * inv
    expand_r = lambda a: jnp.repeat(a, GRID_W, axis=0)
    expand_c = lambda a: jnp.tile(a, (n_rows, 1))
    cos = jnp.concatenate([expand_r(jnp.cos(ar))] * 2 + [expand_c(jnp.cos(ac))] * 2, axis=1)
    sin = jnp.concatenate([expand_r(-jnp.sin(ar)), expand_r(jnp.sin(ar)),
                           expand_c(-jnp.sin(ac)), expand_c(jnp.sin(ac))], axis=1)
    if pad_to > dim:
        before, after = first_lane, pad_to - dim - first_lane
        cos = jnp.concatenate([jnp.ones((s, before), F32), cos, jnp.ones((s, after), F32)], axis=1)
        sin = jnp.concatenate([jnp.zeros((s, before), F32), sin, jnp.zeros((s, after), F32)], axis=1)
    reps = LANES // cos.shape[1]
    return jnp.tile(cos, (1, reps)), jnp.tile(sin, (1, reps))


def _even_weights(w_in, w_q_up, w_uk, w_uv, w_out):
    d = w_in.shape[0]
    n_lat = MLA_Q_RANK + MLA_KV_RANK
    win = jnp.concatenate([w_in[:, :n_lat], jnp.zeros((d, ROPE_LANE), w_in.dtype), w_in[:, n_lat:MLA_IN],
                           jnp.zeros((d, LANES - ROPE_LANE - MLA_ROPE), w_in.dtype), w_in[:, MLA_IN:]], axis=1)
    wq3 = w_q_up.reshape(MLA_Q_RANK, MLA_HEADS, MLA_NOPE + MLA_ROPE)
    nope = wq3[:, :, :MLA_NOPE].reshape(MLA_Q_RANK, MLA_HEADS * MLA_NOPE)
    rope = jnp.pad(wq3[:, :, MLA_NOPE:], ((0, 0), (0, 0), (ROPE_LANE, LANES - ROPE_LANE - MLA_ROPE)))
    rope = rope.reshape(MLA_Q_RANK, MLA_HEADS * LANES)
    wq = jnp.concatenate([nope, rope], axis=1)
    eye = jnp.eye(MLA_HEADS, dtype=w_uk.dtype)
    wuk = jnp.einsum('hcn,hg->hngc', w_uk, eye).reshape(MLA_HEADS * MLA_NOPE, MLA_HEADS * MLA_KV_RANK)
    wuv = jnp.einsum('hcv,hg->hcgv', w_uv, eye).reshape(MLA_HEADS * MLA_KV_RANK, MLA_HEADS * MLA_V)
    n_mla = MLA_HEADS * MLA_V
    return (win.astype(BF16), wq.astype(BF16), wuk.astype(BF16), wuv.astype(BF16),
            w_out[:n_mla].astype(BF16), w_out[n_mla:].astype(BF16))


def _odd_weights(w_in):
    d = w_in.shape[0]
    nq = SWA_HEADS * HEAD_DIM
    z = jnp.zeros((d, HEAD_DIM), w_in.dtype)

    def variants(off):
        cols = []
        for g in range(SWA_KV_HEADS):
            w = w_in[:, off + g * HEAD_DIM:off + (g + 1) * HEAD_DIM]
            cols += [w, z, z, w]
        return jnp.concatenate(cols, axis=1)

    kcols = variants(nq)
    vcols = variants(nq + SWA_KV_HEADS * HEAD_DIM)
    return jnp.concatenate([w_in[:, :nq], kcols, vcols], axis=1).astype(BF16)


def _sink_table(sinks):
    g = SWA_HEADS // SWA_KV_HEADS
    t = (sinks.astype(F32) * LOG2_E).reshape(SWA_KV_HEADS, g // 2, 2).transpose(0, 2, 1)
    t = jnp.broadcast_to(t[:, :, :, None, None], (SWA_KV_HEADS, 2, g // 2, SWA_BLOCK, LANES))
    return t.reshape(SWA_KV_HEADS, 2, (g // 2) * SWA_BLOCK, LANES)


def kernel(x, c, ctx, c_ctx, mod_w, mod_b, even_w_in, mla_q_norm, mla_kv_norm, mla_w_q_up, mla_w_uk, mla_w_uv,
           na_rel_bias, even_w_out, odd_w_in, swa_sinks, odd_w_out, ffn_w_gate_up, ffn_w_down, final_norm):
    b, s, d = x.shape
    n_ctx = ctx.shape[1]
    n_rows = s // GRID_W
    assert d == D_MODEL and s % (NA_STEP_BLOCKS * NA_RB * GRID_W) == 0 and s % (SWA_GROUP * SWA_BLOCK) == 0
    assert mod_w.shape[0] == 2 and b <= 4

    ctx_grp = 4
    c8 = jnp.zeros((8, d), F32).at[:b].set(c.astype(F32)).at[ctx_grp].set(c_ctx.astype(F32))
    mods = _modulation(c8, mod_w.astype(F32), mod_b.astype(F32))
    mods0 = mods[0].reshape(8, 1, 6 * d)
    mods1 = mods[1].reshape(8, 1, 6 * d)
    lat_grp = lambda bi: bi
    ctx_g = lambda bi: ctx_grp

    tm = min(512, s)
    tf = 256
    wgu = ffn_w_gate_up.astype(BF16)
    wdn = ffn_w_down.astype(BF16)

    win, wq, wuk, wuv, wo_mla, wo_na = _even_weights(even_w_in[0], mla_w_q_up[0], mla_w_uk[0], mla_w_uv[0],
                                                      even_w_out[0])
    qn = mla_q_norm[0].astype(F32).reshape(1, -1)
    kvn = mla_kv_norm[0].astype(F32).reshape(1, -1)
    cos_m, sin_m = _rope_tables(s, MLA_ROPE, LANES, ROPE_LANE)
    one_c = jnp.ones((n_ctx, LANES), F32)
    zero_c = jnp.zeros((n_ctx, LANES), F32)

    q_l, k_l, naq_l, nak_l, nav_l = _inproj0(x, mods0, lat_grp, win, qn, wq, wuk, kvn, cos_m, sin_m, tm)
    q_c, k_c, naq_c, nak_c, nav_c = _inproj0(ctx, mods0, ctx_g, win, qn, wq, wuk, kvn, one_c, zero_c, n_ctx)

    k_all = jnp.concatenate([k_c, k_l], axis=1)
    vt_all = jnp.swapaxes(k_all[:, :, :MLA_V_ROWS], 1, 2)
    wuvt = wuv.T
    o_mla_l = _mla_t_attention(q_l, k_all, vt_all, wuvt, min(1024, s))
    o_mla_c = _mla_t_attention(q_c, k_c, vt_all[:, :, :n_ctx], wuvt, n_ctx)

    bias = _na_bias_table(na_rel_bias[0], n_rows)
    o_na_l = _na_attention(naq_l, nak_l, nav_l, nak_c, nav_c, bias)
    o_na_c = _pair_ctx_attention(naq_c, nak_c, nav_c)

    h_lat = _ffn_block(x, mods0, lat_grp, [o_mla_l, o_na_l], [wo_mla, wo_na], wgu[0], wdn[0], None, tm, tf)
    h_ctx = _ffn_block(ctx, mods0, ctx_g, [o_mla_c, o_na_c], [wo_mla, wo_na], wgu[0], wdn[0], None, n_ctx, tf)

    win1 = _odd_weights(odd_w_in[0])
    cos_s, sin_s = _rope_tables(s, HEAD_DIM, HEAD_DIM)
    q1, k1, v1 = _inproj1(h_lat, mods1, lat_grp, win1, cos_s, sin_s, tm)
    _, k1c, v1c = _inproj1(h_ctx, mods1, ctx_g, win1, one_c, zero_c, n_ctx)
    o_swa = _swa_attention(q1, k1, v1, k1c, v1c, _sink_table(swa_sinks[0]))
    fn = final_norm.astype(F32).reshape(1, d)
    return _ffn_block(h_lat, mods1, lat_grp, [o_swa], [odd_w_out[0].astype(BF16)], wgu[1], wdn[1], fn, tm, tf)
```

```python
import functools

import numpy as np
import jax
import jax.numpy as jnp
from jax import lax
from jax.experimental import pallas as pl
from jax.experimental.pallas import tpu as pltpu

F32 = jnp.float32
BF16 = jnp.bfloat16

D_MODEL = 1024
GRID_W = 64
HEAD_DIM = 64
ROPE_BASE = 10000.0
EPS = 1e-6
NEG_INF = -1e30
LOG2_E = 1.4426950408889634

MLA_HEADS = 8
MLA_NOPE = 64
MLA_ROPE = 32
MLA_V = 64
MLA_Q_RANK = 256
MLA_KV_RANK = 128
MLA_IN = MLA_Q_RANK + MLA_KV_RANK + MLA_ROPE

NA_HEADS = 8
NA_KR = 8
NA_KC = 16
NA_RB = 4
NA_SPAN = 12
NA_STEP_BLOCKS = 16

SWA_HEADS = 16
SWA_KV_HEADS = 2
SWA_WINDOW = 128
SWA_BLOCK = 128
SWA_GROUP = 16

LANES = 128
MLA_QK_PAD = 256
MLA_ITEM_ROWS = 256
MLA_KEY_CHUNK = 768
ONES_LANE = 0
ROPE_LANE = 8
MLA_V_ROWS = 144
VMEM_LIMIT = 56 * 1024 * 1024


def _cparams(sem):
    return pltpu.CompilerParams(dimension_semantics=sem, vmem_limit_bytes=VMEM_LIMIT)


def _dot(a, b):
    return jnp.dot(a, b, preferred_element_type=F32)


def _dot_nt(a, b):
    return lax.dot_general(a, b, (((1,), (1,)), ((), ())), preferred_element_type=F32)


def _rms(x):
    return x * lax.rsqrt(jnp.mean(x * x, axis=-1, keepdims=True) + EPS)


def _silu(x):
    return x * (1.0 / (1.0 + jnp.exp(-x)))


def _rope_tile(x, cos, sin, quarter, first_lane=0):
    lane = lax.broadcasted_iota(jnp.int32, x.shape, 1)
    first = ((lane + (2 * quarter - first_lane)) % (2 * quarter)) < quarter
    swapped = jnp.where(first, pltpu.roll(x, LANES - quarter, 1), pltpu.roll(x, quarter, 1))
    return x * cos + swapped * sin


def _zero_of(x):
    u = lax.bitcast_convert_type(x, jnp.uint32)
    u = lax.shift_right_logical(lax.shift_right_logical(u, jnp.uint32(16)), jnp.uint32(16))
    return lax.bitcast_convert_type(u, F32)


def _tiles(s):
    return [s[:, j * LANES:(j + 1) * LANES] for j in range(s.shape[1] // LANES)]


def _rowmax128(tiles):
    mx = tiles[0]
    for t in tiles[1:]:
        mx = jnp.maximum(mx, t)
    return jnp.broadcast_to(jnp.max(mx, axis=1, keepdims=True), mx.shape)


def _exp_tiles(tiles, m):
    return [jnp.exp2(t - m).astype(BF16) for t in tiles]


def _pair_indicator(n, hi):
    lane = lax.broadcasted_iota(jnp.int32, (n, LANES), 1)
    return jnp.where((lane >= HEAD_DIM) == hi, 1.0, 0.0).astype(BF16)


def _mod_kernel(c_ref, w_ref, b_ref, o_ref):
    a = _silu(c_ref[...])
    o_ref[0] = jnp.dot(a, w_ref[0], precision=lax.Precision.HIGHEST, preferred_element_type=F32) + b_ref[0]


def _modulation(c8, mod_w, mod_b):
    depth, d, n = mod_w.shape
    tn = 1536
    return pl.pallas_call(
        _mod_kernel,
        out_shape=jax.ShapeDtypeStruct((depth, 8, n), F32),
        grid=(depth, n // tn),
        in_specs=[pl.BlockSpec((8, d), lambda l, j: (0, 0)),
                  pl.BlockSpec((1, d, tn), lambda l, j: (l, 0, j)),
                  pl.BlockSpec((1, 1, tn), lambda l, j: (l, 0, j))],
        out_specs=pl.BlockSpec((1, 8, tn), lambda l, j: (l, 0, j)),
        compiler_params=_cparams(("parallel", "parallel")),
        name="modulation",
    )(c8, mod_w, mod_b.reshape(depth, 1, n))


def _inproj0_kernel(x_ref, mod_ref, win_ref, qn_ref, wq_ref, wuk_ref, kvn_ref, cos_ref, sin_ref,
                    qmla_ref, kmla_ref, naq_ref, nak_ref, nav_ref):
    d = D_MODEL
    x = x_ref[0]
    shift = mod_ref[0, :, 0:d]
    scale = mod_ref[0, :, d:2 * d]
    a = (_rms(x) * (1.0 + scale) + shift).astype(BF16)
    p = _dot(a, win_ref[...])
    cos = cos_ref[...]
    sin = sin_ref[...]
    mla_scale = (MLA_NOPE + MLA_ROPE) ** -0.5 * LOG2_E

    cq = (_rms(p[:, 0:MLA_Q_RANK]) * qn_ref[...]).astype(BF16)
    q = _dot(cq, wq_ref[...])
    n_nope = MLA_HEADS * MLA_NOPE
    q_lat = _dot(q[:, 0:n_nope].astype(BF16), wuk_ref[...])
    for h in range(MLA_HEADS):
        qr = _rope_tile(q[:, n_nope + h * LANES:n_nope + (h + 1) * LANES], cos, sin, MLA_ROPE // 4, ROPE_LANE)
        qmla_ref[0, h, :, 0:LANES] = (q_lat[:, h * LANES:(h + 1) * LANES] * mla_scale).astype(BF16)
        qmla_ref[0, h, :, LANES:2 * LANES] = (qr * mla_scale).astype(BF16)

    ckv = _rms(p[:, MLA_Q_RANK:MLA_Q_RANK + MLA_KV_RANK]) * kvn_ref[...]
    kr = _rope_tile(p[:, 3 * LANES:4 * LANES], cos, sin, MLA_ROPE // 4, ROPE_LANE)
    lane = lax.broadcasted_iota(jnp.int32, kr.shape, 1)
    kr = jnp.where(lane == ONES_LANE, 1.0, kr)
    kmla_ref[0, :, 0:LANES] = ckv.astype(BF16)
    kmla_ref[0, :, LANES:2 * LANES] = kr.astype(BF16)

    w = NA_HEADS * HEAD_DIM
    naq_ref[0] = (p[:, 4 * LANES:4 * LANES + w] * (HEAD_DIM ** -0.5 * LOG2_E)).astype(BF16)
    nak_ref[0] = p[:, 4 * LANES + w:4 * LANES + 2 * w].astype(BF16)
    nav_ref[0] = p[:, 4 * LANES + 2 * w:4 * LANES + 3 * w].astype(BF16)


def _inproj0(x, mods, grp, win, qn, wq, wuk, kvn, cos, sin, tm):
    b, s, d = x.shape
    nt = s // tm
    w = NA_HEADS * HEAD_DIM
    const = lambda bi, i: (0, 0)
    return pl.pallas_call(
        _inproj0_kernel,
        out_shape=(jax.ShapeDtypeStruct((b, MLA_HEADS, s, MLA_QK_PAD), BF16),
                   jax.ShapeDtypeStruct((b, s, MLA_QK_PAD), BF16),
                   jax.ShapeDtypeStruct((b, s, w), BF16),
                   jax.ShapeDtypeStruct((b, s, w), BF16),
                   jax.ShapeDtypeStruct((b, s, w), BF16)),
        grid=(b, nt),
        in_specs=[pl.BlockSpec((1, tm, d), lambda bi, i: (bi, i, 0)),
                  pl.BlockSpec((1, 1, mods.shape[2]), lambda bi, i: (grp(bi), 0, 0)),
                  pl.BlockSpec(win.shape, const),
                  pl.BlockSpec(qn.shape, const),
                  pl.BlockSpec(wq.shape, const),
                  pl.BlockSpec(wuk.shape, const),
                  pl.BlockSpec(kvn.shape, const),
                  pl.BlockSpec((tm, LANES), lambda bi, i: (i, 0)),
                  pl.BlockSpec((tm, LANES), lambda bi, i: (i, 0))],
        out_specs=(pl.BlockSpec((1, MLA_HEADS, tm, MLA_QK_PAD), lambda bi, i: (bi, 0, i, 0)),
                   pl.BlockSpec((1, tm, MLA_QK_PAD), lambda bi, i: (bi, i, 0)),
                   pl.BlockSpec((1, tm, w), lambda bi, i: (bi, i, 0)),
                   pl.BlockSpec((1, tm, w), lambda bi, i: (bi, i, 0)),
                   pl.BlockSpec((1, tm, w), lambda bi, i: (bi, i, 0))),
        compiler_params=_cparams(("parallel", "parallel")),
        name="inproj_even",
    )(x, mods, win, qn, wq, wuk, kvn, cos, sin)


def _mla_t_kernel(q_ref, k_ref, vt_ref, wuvt_ref, o_ref, s0_ref, s1_ref, p0_ref, p1_ref, m0_ref, m1_ref, oh_ref):
    n_heads, tq = q_ref.shape[1], q_ref.shape[2]
    nk, rc = s0_ref.shape
    n_items = (tq // rc) * n_heads
    s_refs, p_refs, m_refs = (s0_ref, s1_ref), (p0_ref, p1_ref), (m0_ref, m1_ref)
    tk = MLA_KEY_CHUNK if nk % MLA_KEY_CHUNK == 0 else 2 * LANES
    rg = 128

    def item(i):
        if isinstance(i, int):
            return i % n_heads, i // n_heads
        return lax.rem(i, n_heads), lax.div(i, n_heads)

    def round_(r, par):
        static = isinstance(r, int)
        do_qk = not static or r < n_items
        do_sm = not static or 1 <= r <= n_items
        do_pv = not static or 2 <= r <= n_items + 1
        if do_qk:
            h_q, c_q = item(r)
            row_q = c_q * rc if static else pl.multiple_of(c_q * rc, rc)
            q = q_ref[0, h_q, pl.ds(row_q, rc), :]
        if do_sm:
            m8 = m_refs[1 - par][...]
        if do_pv:
            acc = None
        for c in range(nk // tk):
            keys = slice(c * tk, (c + 1) * tk)
            if do_qk:
                st = _dot_nt(k_ref[0, keys, :], q)
                s_refs[par][keys, :] = st
                cm = jnp.max(st.reshape(tk // 8, 8, rc), axis=0)
                m_refs[par][...] = cm if c == 0 else jnp.maximum(m_refs[par][...], cm)
            if do_sm:
                for g in range(tk // rg):
                    m = m8 + _zero_of(st[g * rg:g * rg + 8, :]) if do_qk else m8
                    blk = slice(c * tk + g * rg, c * tk + (g + 1) * rg)
                    sb = s_refs[1 - par][blk, :].reshape(rg // 8, 8, rc)
                    p_refs[1 - par][blk, :] = jnp.exp2(sb - m[None]).reshape(rg, rc).astype(BF16)
            if do_pv:
                part = _dot(vt_ref[0, :, keys], p_refs[par][keys, :])
                acc = part if acc is None else acc + part
        if do_qk:
            mx = m_refs[par][...]
            m_refs[par][...] = jnp.broadcast_to(jnp.max(mx, axis=0, keepdims=True), mx.shape)
        if do_pv:
            h_v, c_v = item(r - 2)
            l = acc[MLA_KV_RANK + ONES_LANE:MLA_KV_RANK + ONES_LANE + 1, :]
            oh_ref[c_v, h_v] = (acc[0:MLA_KV_RANK, :] / l).astype(BF16)

    round_(0, 0)
    round_(1, 1)
    n_full = max(n_items - 2, 0)

    def pair(i, carry):
        r = 2 + 2 * i
        round_(r, 0)
        round_(r + 1, 1)
        return carry

    lax.fori_loop(0, n_full // 2, pair, 0)
    for r in range(2 + 2 * (n_full // 2), n_items):
        round_(r, r % 2)
    for r in range(max(n_items, 2), n_items + 2):
        round_(r, r % 2)

    for ci in range(tq // rc):
        ot = jnp.concatenate([oh_ref[ci, h] for h in range(n_heads)], axis=0)
        out_t = _dot(wuvt_ref[...], ot)
        o_ref[0, ci * rc:(ci + 1) * rc, :] = out_t.T.astype(o_ref.dtype)


def _mla_t_attention(q, k, vt, wuvt, tq):
    b, h, sq, _ = q.shape
    nk = k.shape[1]
    rc = min(MLA_ITEM_ROWS, tq)
    assert nk % (2 * LANES) == 0 and tq % rc == 0 and sq % tq == 0
    wo = wuvt.shape[0]
    one_buffer = pl.Buffered(1)
    return pl.pallas_call(
        _mla_t_kernel,
        out_shape=jax.ShapeDtypeStruct((b, sq, wo), BF16),
        grid=(b, sq // tq),
        in_specs=[pl.BlockSpec((1, h, tq, MLA_QK_PAD), lambda bi, i: (bi, 0, i, 0)),
                  pl.BlockSpec((1, nk, MLA_QK_PAD), lambda bi, i: (bi, 0, 0), pipeline_mode=one_buffer),
                  pl.BlockSpec((1, MLA_V_ROWS, nk), lambda bi, i: (bi, 0, 0), pipeline_mode=one_buffer),
                  _resident(wuvt.shape)],
        out_specs=pl.BlockSpec((1, tq, wo), lambda bi, i: (bi, i, 0)),
        scratch_shapes=[pltpu.VMEM((nk, rc), F32), pltpu.VMEM((nk, rc), F32),
                        pltpu.VMEM((nk, rc), BF16), pltpu.VMEM((nk, rc), BF16),
                        pltpu.VMEM((8, rc), F32), pltpu.VMEM((8, rc), F32),
                        pltpu.VMEM((tq // rc, h, MLA_KV_RANK, rc), BF16)],
        compiler_params=_cparams(("parallel", "arbitrary")),
        name="mla_attention",
    )(q, k, vt, wuvt)


def _split_heads(t):
    lane = lax.broadcasted_iota(jnp.int32, t.shape, 1)
    lo = lane < HEAD_DIM
    zero = jnp.zeros_like(t)
    return jnp.where(lo, t, zero), jnp.where(lo, zero, t)


def _stack_pair(lo_hi_ctx, lo_hi_lat):
    return jnp.concatenate([lo_hi_ctx[0], lo_hi_ctx[1], lo_hi_lat[0], lo_hi_lat[1]], axis=0)


def _na_kernel(q_ref, k_ref, v_ref, kc_ref, vc_ref, bias_ref, o_ref, *, n_rows):
    step = pl.program_id(2)
    nrb = n_rows // NA_RB
    n_keys = NA_SPAN * GRID_W
    n_ctx = kc_ref.shape[1]
    tq = NA_RB * GRID_W
    kcs = _split_heads(kc_ref[0])
    vcs = _split_heads(vc_ref[0])
    ind = _stack_pair((_pair_indicator(n_ctx, False), _pair_indicator(n_ctx, True)),
                      (_pair_indicator(n_keys, False), _pair_indicator(n_keys, True)))
    for i in range(NA_STEP_BLOCKS):
        rb = step * NA_STEP_BLOCKS + i
        case = jnp.minimum(rb, 1) + jnp.maximum(rb - (nrb - 2), 0)
        base = jnp.clip(rb * NA_RB - NA_KR // 2, 0, n_rows - NA_SPAN)
        start = pl.multiple_of(base * GRID_W, 256)
        q = q_ref[0, i * tq:(i + 1) * tq, :]
        k_st = _stack_pair(kcs, _split_heads(k_ref[0, pl.ds(start, n_keys), :]))
        v_st = _stack_pair(vcs, _split_heads(v_ref[0, pl.ds(start, n_keys), :]))
        s = _dot_nt(q, k_st)
        p_ctx, p_lat = [], []
        for par in range(2):
            t_c = _tiles(s[:, par * n_ctx:(par + 1) * n_ctx])
            off = 2 * n_ctx + par * n_keys
            t_l = _tiles(s[:, off:off + n_keys] + bias_ref[case, par])
            m = _rowmax128(t_c + t_l)
            p_ctx += _exp_tiles(t_c, m)
            p_lat += _exp_tiles(t_l, m)
        o = _dot(jnp.concatenate(p_ctx + p_lat, axis=1), jnp.concatenate([v_st, ind], axis=1))
        o_ref[0, i * tq:(i + 1) * tq, :] = (o[:, :LANES] / o[:, LANES:]).astype(o_ref.dtype)


def _na_attention(q, k, v, kc, vc, bias):
    b, s, w = q.shape
    n_rows = s // GRID_W
    nrb = n_rows // NA_RB
    tq = NA_STEP_BLOCKS * NA_RB * GRID_W
    n_ctx = kc.shape[1]
    return pl.pallas_call(
        functools.partial(_na_kernel, n_rows=n_rows),
        out_shape=jax.ShapeDtypeStruct((b, s, w), BF16),
        grid=(w // LANES, b, nrb // NA_STEP_BLOCKS),
        in_specs=[pl.BlockSpec((1, tq, LANES), lambda j, bi, r: (bi, r, j)),
                  pl.BlockSpec((1, s, LANES), lambda j, bi, r: (bi, 0, j)),
                  pl.BlockSpec((1, s, LANES), lambda j, bi, r: (bi, 0, j)),
                  pl.BlockSpec((1, n_ctx, LANES), lambda j, bi, r: (bi, 0, j)),
                  pl.BlockSpec((1, n_ctx, LANES), lambda j, bi, r: (bi, 0, j)),
                  pl.BlockSpec((3, 2, NA_RB * GRID_W, NA_SPAN * GRID_W), lambda j, bi, r: (0, j, 0, 0))],
        out_specs=pl.BlockSpec((1, tq, LANES), lambda j, bi, r: (bi, r, j)),
        compiler_params=_cparams(("parallel", "parallel", "arbitrary")),
        name="na_attention",
    )(q, k, v, kc, vc, bias)


def _pair_ctx_kernel(q_ref, kc_ref, vc_ref, o_ref):
    q = q_ref[0]
    n_ctx = kc_ref.shape[1]
    kcs = _split_heads(kc_ref[0])
    vcs = _split_heads(vc_ref[0])
    s = _dot_nt(q, jnp.concatenate(kcs, axis=0))
    p = []
    for par in range(2):
        t_c = _tiles(s[:, par * n_ctx:(par + 1) * n_ctx])
        p += _exp_tiles(t_c, _rowmax128(t_c))
    ind = jnp.concatenate([_pair_indicator(n_ctx, False), _pair_indicator(n_ctx, True)], axis=0)
    o = _dot(jnp.concatenate(p, axis=1), jnp.concatenate([jnp.concatenate(vcs, axis=0), ind], axis=1))
    o_ref[0] = (o[:, :LANES] / o[:, LANES:]).astype(o_ref.dtype)


def _pair_ctx_attention(q, kc, vc):
    b, n, w = q.shape
    spec = pl.BlockSpec((1, n, LANES), lambda bi, j: (bi, 0, j))
    return pl.pallas_call(
        _pair_ctx_kernel,
        out_shape=jax.ShapeDtypeStruct((b, n, w), BF16),
        grid=(b, w // LANES),
        in_specs=[spec, spec, spec],
        out_specs=spec,
        compiler_params=_cparams(("parallel", "parallel")),
        name="na_ctx_attention",
    )(q, kc, vc)


def _na_bias_table(rel_bias, n_rows):
    qc = np.arange(GRID_W)
    cstart = np.clip(qc - NA_KC // 2, 0, GRID_W - NA_KC)
    kc = np.arange(GRID_W)
    col_ok = (kc[None, :] >= cstart[:, None]) & (kc[None, :] < cstart[:, None] + NA_KC)
    n_dr, n_dc = 2 * NA_KR - 1, 2 * NA_KC - 1
    nh = rel_bias.shape[0]
    dc = kc[None, :] - qc[:, None] + NA_KC - 1
    sel_col = ((dc[None] == np.arange(n_dc)[:, None, None]) & col_ok[None]).astype(np.float32)
    toe = jnp.einsum('hdu,uqk->hdqk', rel_bias.astype(F32), jnp.asarray(sel_col), precision=lax.Precision.HIGHEST)
    toe = jnp.where(jnp.asarray(col_ok)[None, None], toe * LOG2_E, NEG_INF)
    toe = jnp.concatenate([toe, jnp.full((nh, 1, GRID_W, GRID_W), NEG_INF, F32)], axis=1)
    toe = jnp.concatenate([toe, toe], axis=-1)
    idx = np.full((3, NA_RB, NA_SPAN), n_dr, np.int32)
    nrb = n_rows // NA_RB
    for case, rb in enumerate((0, min(1, nrb - 1), nrb - 1)):
        r0 = rb * NA_RB
        base = int(np.clip(r0 - NA_KR // 2, 0, n_rows - NA_SPAN))
        for a in range(NA_RB):
            r = r0 + a
            rs = int(np.clip(r - NA_KR // 2, 0, n_rows - NA_KR))
            for t in range(NA_SPAN):
                kr = base + t
                if rs <= kr < rs + NA_KR:
                    idx[case, a, t] = kr - r + NA_KR - 1

    def assemble(toe_ref, o_ref):
        lo = lax.broadcasted_iota(jnp.int32, (GRID_W, LANES), 1) < GRID_W
        for case in range(3):
            for a in range(NA_RB):
                for tp in range(NA_SPAN // 2):
                    d0, d1 = int(idx[case, a, 2 * tp]), int(idx[case, a, 2 * tp + 1])
                    tile = toe_ref[0, d0] if d0 == d1 else jnp.where(lo, toe_ref[0, d0], toe_ref[0, d1])
                    o_ref[case, 0, a * GRID_W:(a + 1) * GRID_W, tp * LANES:(tp + 1) * LANES] = tile

    tq, tkeys = NA_RB * GRID_W, NA_SPAN * GRID_W
    return pl.pallas_call(
        assemble,
        out_shape=jax.ShapeDtypeStruct((3, nh, tq, tkeys), F32),
        grid=(nh,),
        in_specs=[pl.BlockSpec((1, n_dr + 1, GRID_W, LANES), lambda h: (h, 0, 0, 0))],
        out_specs=pl.BlockSpec((3, 1, tq, tkeys), lambda h: (0, h, 0, 0)),
        compiler_params=_cparams(("parallel",)),
        name="na_bias_table",
    )(toe)


def _ffn_kernel(*refs, n_attn, tf, final):
    h_ref, mod_ref = refs[0], refs[1]
    attn_refs = refs[2:2 + n_attn]
    wout_refs = refs[2 + n_attn:2 + 2 * n_attn]
    wgu_ref, wd_ref = refs[2 + 2 * n_attn:4 + 2 * n_attn]
    pos = 4 + 2 * n_attn
    fn_ref = refs[pos] if final else None
    pos += 1 if final else 0
    o_ref, a2_ref, f_ref = refs[pos:pos + 3]
    d = D_MODEL
    dff = wd_ref.shape[0]

    proj = _dot(attn_refs[0][0], wout_refs[0][...])
    for t in range(1, n_attn):
        proj = proj + _dot(attn_refs[t][0], wout_refs[t][...])
    h1 = h_ref[0] + mod_ref[0, :, 2 * d:3 * d] * proj
    o_ref[0] = h1
    a2_ref[...] = (_rms(h1) * (1.0 + mod_ref[0, :, 4 * d:5 * d]) + mod_ref[0, :, 3 * d:4 * d]).astype(BF16)

    for j in range(dff // tf):
        a2 = a2_ref[...]
        g = _dot(a2, wgu_ref[:, j * tf:(j + 1) * tf])
        u = _dot(a2, wgu_ref[:, dff + j * tf:dff + (j + 1) * tf])
        f_ref[:, j * tf:(j + 1) * tf] = (_silu(g) * u).astype(BF16)

    out = o_ref[0] + mod_ref[0, :, 5 * d:6 * d] * _dot(f_ref[...], wd_ref[...])
    if final:
        out = _rms(out) * fn_ref[...]
    o_ref[0] = out


def _resident(shape):
    return pl.BlockSpec(shape, lambda *_: (0,) * len(shape), pipeline_mode=pl.Buffered(1))


def _ffn_block(h, mods, grp, attns, wouts, wgu, wd, final_norm, tm, tf):
    b, s, d = h.shape
    dff = wd.shape[0]
    n_attn = len(attns)
    final = final_norm is not None
    in_specs = [pl.BlockSpec((1, tm, d), lambda bi, i: (bi, i, 0)),
                pl.BlockSpec((1, 1, mods.shape[2]), lambda bi, i: (grp(bi), 0, 0))]
    in_specs += [pl.BlockSpec((1, tm, a.shape[2]), lambda bi, i: (bi, i, 0)) for a in attns]
    in_specs += [_resident(w.shape) for w in wouts]
    in_specs += [_resident(wgu.shape), _resident(wd.shape)]
    args = [h, mods, *attns, *wouts, wgu, wd]
    if final:
        in_specs.append(_resident((1, d)))
        args.append(final_norm)
    return pl.pallas_call(
        functools.partial(_ffn_kernel, n_attn=n_attn, tf=tf, final=final),
        out_shape=jax.ShapeDtypeStruct((b, s, d), F32),
        grid=(b, s // tm),
        in_specs=in_specs,
        out_specs=pl.BlockSpec((1, tm, d), lambda bi, i: (bi, i, 0)),
        scratch_shapes=[pltpu.VMEM((tm, d), BF16), pltpu.VMEM((tm, dff), BF16)],
        compiler_params=_cparams(("parallel", "parallel")),
        name="outproj_ffn",
    )(*args)


def _inproj1_kernel(x_ref, mod_ref, win_ref, cos_ref, sin_ref, q_ref, k_ref, v_ref):
    d = D_MODEL
    x = x_ref[0]
    a = (_rms(x) * (1.0 + mod_ref[0, :, d:2 * d]) + mod_ref[0, :, 0:d]).astype(BF16)
    p = _dot(a, win_ref[...])
    cos = cos_ref[...]
    sin = sin_ref[...]
    nq = SWA_HEADS * HEAD_DIM
    nkv = 2 * SWA_KV_HEADS * LANES
    quarter = HEAD_DIM // 4
    for t in range(nq // LANES):
        q = _rope_tile(p[:, t * LANES:(t + 1) * LANES], cos, sin, quarter)
        q_ref[0, :, t * LANES:(t + 1) * LANES] = (q * (HEAD_DIM ** -0.5 * LOG2_E)).astype(BF16)
    for t in range(nkv // LANES):
        k = _rope_tile(p[:, nq + t * LANES:nq + (t + 1) * LANES], cos, sin, quarter)
        k_ref[0, :, t * LANES:(t + 1) * LANES] = k.astype(BF16)
    v_ref[0] = p[:, nq + nkv:nq + 2 * nkv].astype(BF16)


def _inproj1(x, mods, grp, win, cos, sin, tm):
    b, s, d = x.shape
    nq = SWA_HEADS * HEAD_DIM
    nkv = 2 * SWA_KV_HEADS * LANES
    return pl.pallas_call(
        _inproj1_kernel,
        out_shape=(jax.ShapeDtypeStruct((b, s, nq), BF16),
                   jax.ShapeDtypeStruct((b, s, nkv), BF16),
                   jax.ShapeDtypeStruct((b, s, nkv), BF16)),
        grid=(b, s // tm),
        in_specs=[pl.BlockSpec((1, tm, d), lambda bi, i: (bi, i, 0)),
                  pl.BlockSpec((1, 1, mods.shape[2]), lambda bi, i: (grp(bi), 0, 0)),
                  pl.BlockSpec(win.shape, lambda bi, i: (0, 0)),
                  pl.BlockSpec((tm, LANES), lambda bi, i: (i, 0)),
                  pl.BlockSpec((tm, LANES), lambda bi, i: (i, 0))],
        out_specs=(pl.BlockSpec((1, tm, nq), lambda bi, i: (bi, i, 0)),
                   pl.BlockSpec((1, tm, nkv), lambda bi, i: (bi, i, 0)),
                   pl.BlockSpec((1, tm, nkv), lambda bi, i: (bi, i, 0))),
        compiler_params=_cparams(("parallel", "parallel")),
        name="inproj_odd",
    )(x, mods, win, cos, sin)


def _swa_kernel(q_ref, km_ref, k0_ref, kp_ref, vm_ref, v0_ref, vp_ref, kc_ref, vc_ref, sink_ref, o_ref, *, n_steps):
    step = pl.program_id(2)
    blk = SWA_BLOCK
    npair = q_ref.shape[2] // LANES
    kall = jnp.concatenate([km_ref[0], k0_ref[0], kp_ref[0]], axis=0)
    vall = jnp.concatenate([vm_ref[0], v0_ref[0], vp_ref[0]], axis=0)
    kc = kc_ref[0]
    vc = vc_ref[0]
    n_ctx = kc.shape[0]
    kc_st = jnp.concatenate([kc[:, :LANES], kc[:, LANES:]], axis=0)
    vc_st = jnp.concatenate([vc[:, :LANES], vc[:, LANES:]], axis=0)
    ind = _stack_pair((_pair_indicator(n_ctx, False), _pair_indicator(n_ctx, True)),
                      (_pair_indicator(3 * blk, False), _pair_indicator(3 * blk, True)))
    r = lax.broadcasted_iota(jnp.int32, (blk, 3 * blk), 0)
    c = lax.broadcasted_iota(jnp.int32, (blk, 3 * blk), 1)
    cc = c % blk
    lane = lax.broadcasted_iota(jnp.int32, (npair * blk, LANES), 1)
    lo = lane < HEAD_DIM

    def band(left_off, right_off):
        ok = ((c >= blk) & (c < 2 * blk)) | ((c < blk) & (cc >= r + left_off)) | ((c >= 2 * blk) & (cc <= r - right_off))
        neg = jnp.where(ok, 0.0, NEG_INF).astype(F32)
        return jnp.concatenate([neg] * npair, axis=0)

    for t in range(SWA_GROUP):
        left_off = jnp.where(step >= 1, 0, blk) if t == 0 else 0
        right_off = jnp.where(step <= n_steps - 2, 0, blk) if t == SWA_GROUP - 1 else 0
        neg = band(left_off, right_off)
        qb = q_ref[0, t * blk:(t + 1) * blk, :]
        qst = jnp.concatenate([qb[:, i * LANES:(i + 1) * LANES] for i in range(npair)], axis=0)
        kw = kall[t * blk:(t + 3) * blk]
        vw = vall[t * blk:(t + 3) * blk]
        k_st = jnp.concatenate([kc_st, kw[:, :LANES], kw[:, LANES:]], axis=0)
        v_st = jnp.concatenate([vc_st, vw[:, :LANES], vw[:, LANES:]], axis=0)
        s = _dot_nt(qst, k_st)
        p_ctx, p_lat, p_sink = [], [], []
        for par in range(2):
            sink = sink_ref[0, par]
            t_c = _tiles(s[:, par * n_ctx:(par + 1) * n_ctx])
            off = 2 * n_ctx + par * 3 * blk
            t_l = _tiles(s[:, off:off + 3 * blk] + neg)
            m = jnp.maximum(_rowmax128(t_c + t_l), sink)
            p_ctx += _exp_tiles(t_c, m)
            p_lat += _exp_tiles(t_l, m)
            p_sink.append(jnp.exp2(sink - m))
        o = _dot(jnp.concatenate(p_ctx + p_lat, axis=1), jnp.concatenate([v_st, ind], axis=1))
        o = o[:, :LANES] / (o[:, LANES:] + jnp.where(lo, p_sink[0], p_sink[1]))
        for i in range(npair):
            o_ref[0, t * blk:(t + 1) * blk, i * LANES:(i + 1) * LANES] = o[i * blk:(i + 1) * blk].astype(o_ref.dtype)


def _swa_attention(q, kvar, vvar, kc, vc, sink_tab):
    b, s, nq = q.shape
    blk = SWA_BLOCK
    nb = s // blk
    n_steps = nb // SWA_GROUP
    gw = nq // SWA_KV_HEADS
    n_ctx = kc.shape[1]
    prev = lambda bi, g, j: (bi, jnp.maximum(SWA_GROUP * j - 1, 0), g)
    cur = lambda bi, g, j: (bi, j, g)
    nxt = lambda bi, g, j: (bi, jnp.minimum(SWA_GROUP * (j + 1), nb - 1), g)
    edge_spec = lambda f: pl.BlockSpec((1, blk, 2 * LANES), f)
    main_spec = pl.BlockSpec((1, SWA_GROUP * blk, 2 * LANES), cur)
    ctx_spec = pl.BlockSpec((1, n_ctx, 2 * LANES), lambda bi, g, j: (bi, 0, g))
    return pl.pallas_call(
        functools.partial(_swa_kernel, n_steps=n_steps),
        out_shape=jax.ShapeDtypeStruct((b, s, nq), BF16),
        grid=(b, SWA_KV_HEADS, n_steps),
        in_specs=[pl.BlockSpec((1, SWA_GROUP * blk, gw), cur),
                  edge_spec(prev), main_spec, edge_spec(nxt),
                  edge_spec(prev), main_spec, edge_spec(nxt),
                  ctx_spec, ctx_spec,
                  pl.BlockSpec((1, 2, sink_tab.shape[2], LANES), lambda bi, g, j: (g, 0, 0, 0))],
        out_specs=pl.BlockSpec((1, SWA_GROUP * blk, gw), cur),
        compiler_params=_cparams(("parallel", "parallel", "arbitrary")),
        name="swa_attention",
    )(q, kvar, kvar, kvar, vvar, vvar, vvar, kc, vc, sink_tab)


def _rope_tables(s, dim, pad_to, first_lane=0):
    n_rows = s // GRID_W
    half = dim // 2
    inv = ROPE_BASE ** (-jnp.arange(0, half, 2, dtype=F32) / half)
    ar = jnp.arange(n_rows, dtype=jnp.int32).astype(F32)[:, None] * inv
    ac = jnp.arange(GRID_W, dtype=jnp.int32).astype(F32)[:, None] * inv
    expand_r = lambda a: jnp.repeat(a, GRID_W, axis=0)
    expand_c = lambda a: jnp.tile(a, (n_rows, 1))
    cos = jnp.concatenate([expand_r(jnp.cos(ar))] * 2 + [expand_c(jnp.cos(ac))] * 2, axis=1)
    sin = jnp.concatenate([expand_r(-jnp.sin(ar)), expand_r(jnp.sin(ar)),
                           expand_c(-jnp.sin(ac)), expand_c(jnp.sin(ac))], axis=1)
    if pad_to > dim:
        before, after = first_lane, pad_to - dim - first_lane
        cos = jnp.concatenate([jnp.ones((s, before), F32), cos, jnp.ones((s, after), F32)], axis=1)
        sin = jnp.concatenate([jnp.zeros((s, before), F32), sin, jnp.zeros((s, after), F32)], axis=1)
    reps = LANES // cos.shape[1]
    return jnp.tile(cos, (1, reps)), jnp.tile(sin, (1, reps))


def _even_weights(w_in, w_q_up, w_uk, w_uv, w_out):
    d = w_in.shape[0]
    n_lat = MLA_Q_RANK + MLA_KV_RANK
    win = jnp.concatenate([w_in[:, :n_lat], jnp.zeros((d, ROPE_LANE), w_in.dtype), w_in[:, n_lat:MLA_IN],
                           jnp.zeros((d, LANES - ROPE_LANE - MLA_ROPE), w_in.dtype), w_in[:, MLA_IN:]], axis=1)
    wq3 = w_q_up.reshape(MLA_Q_RANK, MLA_HEADS, MLA_NOPE + MLA_ROPE)
    nope = wq3[:, :, :MLA_NOPE].reshape(MLA_Q_RANK, MLA_HEADS * MLA_NOPE)
    rope = jnp.pad(wq3[:, :, MLA_NOPE:], ((0, 0), (0, 0), (ROPE_LANE, LANES - ROPE_LANE - MLA_ROPE)))
    rope = rope.reshape(MLA_Q_RANK, MLA_HEADS * LANES)
    wq = jnp.concatenate([nope, rope], axis=1)
    eye = jnp.eye(MLA_HEADS, dtype=w_uk.dtype)
    wuk = jnp.einsum('hcn,hg->hngc', w_uk, eye).reshape(MLA_HEADS * MLA_NOPE, MLA_HEADS * MLA_KV_RANK)
    wuv = jnp.einsum('hcv,hg->hcgv', w_uv, eye).reshape(MLA_HEADS * MLA_KV_RANK, MLA_HEADS * MLA_V)
    n_mla = MLA_HEADS * MLA_V
    return (win.astype(BF16), wq.astype(BF16), wuk.astype(BF16), wuv.astype(BF16),
            w_out[:n_mla].astype(BF16), w_out[n_mla:].astype(BF16))


def _odd_weights(w_in):
    d = w_in.shape[0]
    nq = SWA_HEADS * HEAD_DIM
    z = jnp.zeros((d, HEAD_DIM), w_in.dtype)

    def variants(off):
        cols = []
        for g in range(SWA_KV_HEADS):
            w = w_in[:, off + g * HEAD_DIM:off + (g + 1) * HEAD_DIM]
            cols += [w, z, z, w]
        return jnp.concatenate(cols, axis=1)

    kcols = variants(nq)
    vcols = variants(nq + SWA_KV_HEADS * HEAD_DIM)
    return jnp.concatenate([w_in[:, :nq], kcols, vcols], axis=1).astype(BF16)


def _sink_table(sinks):
    g = SWA_HEADS // SWA_KV_HEADS
    t = (sinks.astype(F32) * LOG2_E).reshape(SWA_KV_HEADS, g // 2, 2).transpose(0, 2, 1)
    t = jnp.broadcast_to(t[:, :, :, None, None], (SWA_KV_HEADS, 2, g // 2, SWA_BLOCK, LANES))
    return t.reshape(SWA_KV_HEADS, 2, (g // 2) * SWA_BLOCK, LANES)


def kernel(x, c, ctx, c_ctx, mod_w, mod_b, even_w_in, mla_q_norm, mla_kv_norm, mla_w_q_up, mla_w_uk, mla_w_uv,
           na_rel_bias, even_w_out, odd_w_in, swa_sinks, odd_w_out, ffn_w_gate_up, ffn_w_down, final_norm):
    b, s, d = x.shape
    n_ctx = ctx.shape[1]
    n_rows = s // GRID_W
    assert d == D_MODEL and s % (NA_STEP_BLOCKS * NA_RB * GRID_W) == 0 and s % (SWA_GROUP * SWA_BLOCK) == 0
    assert mod_w.shape[0] == 2 and b <= 4

    ctx_grp = 4
    c8 = jnp.zeros((8, d), F32).at[:b].set(c.astype(F32)).at[ctx_grp].set(c_ctx.astype(F32))
    mods = _modulation(c8, mod_w.astype(F32), mod_b.astype(F32))
    mods0 = mods[0].reshape(8, 1, 6 * d)
    mods1 = mods[1].reshape(8, 1, 6 * d)
    lat_grp = lambda bi: bi
    ctx_g = lambda bi: ctx_grp

    tm = min(512, s)
    tf = 256
    wgu = ffn_w_gate_up.astype(BF16)
    wdn = ffn_w_down.astype(BF16)

    win, wq, wuk, wuv, wo_mla, wo_na = _even_weights(even_w_in[0], mla_w_q_up[0], mla_w_uk[0], mla_w_uv[0],
                                                      even_w_out[0])
    qn = mla_q_norm[0].astype(F32).reshape(1, -1)
    kvn = mla_kv_norm[0].astype(F32).reshape(1, -1)
    cos_m, sin_m = _rope_tables(s, MLA_ROPE, LANES, ROPE_LANE)
    one_c = jnp.ones((n_ctx, LANES), F32)
    zero_c = jnp.zeros((n_ctx, LANES), F32)

    q_l, k_l, naq_l, nak_l, nav_l = _inproj0(x, mods0, lat_grp, win, qn, wq, wuk, kvn, cos_m, sin_m, tm)
    q_c, k_c, naq_c, nak_c, nav_c = _inproj0(ctx, mods0, ctx_g, win, qn, wq, wuk, kvn, one_c, zero_c, n_ctx)

    k_all = jnp.concatenate([k_c, k_l], axis=1)
    vt_all = jnp.swapaxes(k_all[:, :, :MLA_V_ROWS], 1, 2)
    wuvt = wuv.T
    o_mla_l = _mla_t_attention(q_l, k_all, vt_all, wuvt, min(1024, s))
    o_mla_c = _mla_t_attention(q_c, k_c, vt_all[:, :, :n_ctx], wuvt, n_ctx)

    bias = _na_bias_table(na_rel_bias[0], n_rows)
    o_na_l = _na_attention(naq_l, nak_l, nav_l, nak_c, nav_c, bias)
    o_na_c = _pair_ctx_attention(naq_c, nak_c, nav_c)

    h_lat = _ffn_block(x, mods0, lat_grp, [o_mla_l, o_na_l], [wo_mla, wo_na], wgu[0], wdn[0], None, tm, tf)
    h_ctx = _ffn_block(ctx, mods0, ctx_g, [o_mla_c, o_na_c], [wo_mla, wo_na], wgu[0], wdn[0], None, n_ctx, tf)

    win1 = _odd_weights(odd_w_in[0])
    cos_s, sin_s = _rope_tables(s, HEAD_DIM, HEAD_DIM)
    q1, k1, v1 = _inproj1(h_lat, mods1, lat_grp, win1, cos_s, sin_s, tm)
    _, k1c, v1c = _inproj1(h_ctx, mods1, ctx_g, win1, one_c, zero_c, n_ctx)
    o_swa = _swa_attention(q1, k1, v1, k1c, v1c, _sink_table(swa_sinks[0]))
    fn = final_norm.astype(F32).reshape(1, d)
    return _ffn_block(h_lat, mods1, lat_grp, [o_swa], [odd_w_out[0].astype(BF16)], wgu[1], wdn[1], fn, tm, tf)
```

```python
import functools

import numpy as np
import jax
import jax.numpy as jnp
from jax import lax
from jax.experimental import pallas as pl
from jax.experimental.pallas import tpu as pltpu

F32 = jnp.float32
BF16 = jnp.bfloat16

D_MODEL = 1024
GRID_W = 64
HEAD_DIM = 64
ROPE_BASE = 10000.0
EPS = 1e-6
NEG_INF = -1e30
LOG2_E = 1.4426950408889634

MLA_HEADS = 8
MLA_NOPE = 64
MLA_ROPE = 32
MLA_V = 64
MLA_Q_RANK = 256
MLA_KV_RANK = 128
MLA_IN = MLA_Q_RANK + MLA_KV_RANK + MLA_ROPE

NA_HEADS = 8
NA_KR = 8
NA_KC = 16
NA_RB = 4
NA_SPAN = 12
NA_STEP_BLOCKS = 16

SWA_HEADS = 16
SWA_KV_HEADS = 2
SWA_WINDOW = 128
SWA_BLOCK = 128
SWA_GROUP = 16

LANES = 128
MLA_QK_PAD = 256
MLA_ITEM_ROWS = 256
MLA_KEY_CHUNK = 768
ONES_LANE = 0
ROPE_LANE = 8
MLA_V_ROWS = 144
VMEM_LIMIT = 56 * 1024 * 1024


def _cparams(sem):
    return pltpu.CompilerParams(dimension_semantics=sem, vmem_limit_bytes=VMEM_LIMIT)


def _dot(a, b):
    return jnp.dot(a, b, preferred_element_type=F32)


def _dot_nt(a, b):
    return lax.dot_general(a, b, (((1,), (1,)), ((), ())), preferred_element_type=F32)


def _rms(x):
    return x * lax.rsqrt(jnp.mean(x * x, axis=-1, keepdims=True) + EPS)


def _silu(x):
    return x * (1.0 / (1.0 + jnp.exp(-x)))


def _rope_tile(x, cos, sin, quarter, first_lane=0):
    lane = lax.broadcasted_iota(jnp.int32, x.shape, 1)
    first = ((lane + (2 * quarter - first_lane)) % (2 * quarter)) < quarter
    swapped = jnp.where(first, pltpu.roll(x, LANES - quarter, 1), pltpu.roll(x, quarter, 1))
    return x * cos + swapped * sin


def _zero_of(x):
    u = lax.bitcast_convert_type(x, jnp.uint32)
    u = lax.shift_right_logical(lax.shift_right_logical(u, jnp.uint32(16)), jnp.uint32(16))
    return lax.bitcast_convert_type(u, F32)


def _tiles(s):
    return [s[:, j * LANES:(j + 1) * LANES] for j in range(s.shape[1] // LANES)]


def _rowmax128(tiles):
    mx = tiles[0]
    for t in tiles[1:]:
        mx = jnp.maximum(mx, t)
    return jnp.broadcast_to(jnp.max(mx, axis=1, keepdims=True), mx.shape)


def _exp_tiles(tiles, m):
    return [jnp.exp2(t - m).astype(BF16) for t in tiles]


def _pair_indicator(n, hi):
    lane = lax.broadcasted_iota(jnp.int32, (n, LANES), 1)
    return jnp.where((lane >= HEAD_DIM) == hi, 1.0, 0.0).astype(BF16)


def _mod_kernel(c_ref, w_ref, b_ref, o_ref):
    a = _silu(c_ref[...])
    o_ref[0] = jnp.dot(a, w_ref[0], precision=lax.Precision.HIGHEST, preferred_element_type=F32) + b_ref[0]


def _modulation(c8, mod_w, mod_b):
    depth, d, n = mod_w.shape
    tn = 1536
    return pl.pallas_call(
        _mod_kernel,
        out_shape=jax.ShapeDtypeStruct((depth, 8, n), F32),
        grid=(depth, n // tn),
        in_specs=[pl.BlockSpec((8, d), lambda l, j: (0, 0)),
                  pl.BlockSpec((1, d, tn), lambda l, j: (l, 0, j)),
                  pl.BlockSpec((1, 1, tn), lambda l, j: (l, 0, j))],
        out_specs=pl.BlockSpec((1, 8, tn), lambda l, j: (l, 0, j)),
        compiler_params=_cparams(("parallel", "parallel")),
        name="modulation",
    )(c8, mod_w, mod_b.reshape(depth, 1, n))


def _inproj0_kernel(x_ref, mod_ref, win_ref, qn_ref, wq_ref, wuk_ref, kvn_ref, cos_ref, sin_ref,
                    qmla_ref, kmla_ref, naq_ref, nak_ref, nav_ref):
    d = D_MODEL
    x = x_ref[0]
    shift = mod_ref[0, :, 0:d]
    scale = mod_ref[0, :, d:2 * d]
    a = (_rms(x) * (1.0 + scale) + shift).astype(BF16)
    p = _dot(a, win_ref[...])
    cos = cos_ref[...]
    sin = sin_ref[...]
    mla_scale = (MLA_NOPE + MLA_ROPE) ** -0.5 * LOG2_E

    cq = (_rms(p[:, 0:MLA_Q_RANK]) * qn_ref[...]).astype(BF16)
    q = _dot(cq, wq_ref[...])
    n_nope = MLA_HEADS * MLA_NOPE
    q_lat = _dot(q[:, 0:n_nope].astype(BF16), wuk_ref[...])
    for h in range(MLA_HEADS):
        qr = _rope_tile(q[:, n_nope + h * LANES:n_nope + (h + 1) * LANES], cos, sin, MLA_ROPE // 4, ROPE_LANE)
        qmla_ref[0, h, :, 0:LANES] = (q_lat[:, h * LANES:(h + 1) * LANES] * mla_scale).astype(BF16)
        qmla_ref[0, h, :, LANES:2 * LANES] = (qr * mla_scale).astype(BF16)

    ckv = _rms(p[:, MLA_Q_RANK:MLA_Q_RANK + MLA_KV_RANK]) * kvn_ref[...]
    kr = _rope_tile(p[:, 3 * LANES:4 * LANES], cos, sin, MLA_ROPE // 4, ROPE_LANE)
    lane = lax.broadcasted_iota(jnp.int32, kr.shape, 1)
    kr = jnp.where(lane == ONES_LANE, 1.0, kr)
    kmla_ref[0, :, 0:LANES] = ckv.astype(BF16)
    kmla_ref[0, :, LANES:2 * LANES] = kr.astype(BF16)

    w = NA_HEADS * HEAD_DIM
    naq_ref[0] = (p[:, 4 * LANES:4 * LANES + w] * (HEAD_DIM ** -0.5 * LOG2_E)).astype(BF16)
    nak_ref[0] = p[:, 4 * LANES + w:4 * LANES + 2 * w].astype(BF16)
    nav_ref[0] = p[:, 4 * LANES + 2 * w:4 * LANES + 3 * w].astype(BF16)


def _inproj0(x, mods, grp, win, qn, wq, wuk, kvn, cos, sin, tm):
    b, s, d = x.shape
    nt = s // tm
    w = NA_HEADS * HEAD_DIM
    const = lambda bi, i: (0, 0)
    return pl.pallas_call(
        _inproj0_kernel,
        out_shape=(jax.ShapeDtypeStruct((b, MLA_HEADS, s, MLA_QK_PAD), BF16),
                   jax.ShapeDtypeStruct((b, s, MLA_QK_PAD), BF16),
                   jax.ShapeDtypeStruct((b, s, w), BF16),
                   jax.ShapeDtypeStruct((b, s, w), BF16),
                   jax.ShapeDtypeStruct((b, s, w), BF16)),
        grid=(b, nt),
        in_specs=[pl.BlockSpec((1, tm, d), lambda bi, i: (bi, i, 0)),
                  pl.BlockSpec((1, 1, mods.shape[2]), lambda bi, i: (grp(bi), 0, 0)),
                  pl.BlockSpec(win.shape, const),
                  pl.BlockSpec(qn.shape, const),
                  pl.BlockSpec(wq.shape, const),
                  pl.BlockSpec(wuk.shape, const),
                  pl.BlockSpec(kvn.shape, const),
                  pl.BlockSpec((tm, LANES), lambda bi, i: (i, 0)),
                  pl.BlockSpec((tm, LANES), lambda bi, i: (i, 0))],
        out_specs=(pl.BlockSpec((1, MLA_HEADS, tm, MLA_QK_PAD), lambda bi, i: (bi, 0, i, 0)),
                   pl.BlockSpec((1, tm, MLA_QK_PAD), lambda bi, i: (bi, i, 0)),
                   pl.BlockSpec((1, tm, w), lambda bi, i: (bi, i, 0)),
                   pl.BlockSpec((1, tm, w), lambda bi, i: (bi, i, 0)),
                   pl.BlockSpec((1, tm, w), lambda bi, i: (bi, i, 0))),
        compiler_params=_cparams(("parallel", "parallel")),
        name="inproj_even",
    )(x, mods, win, qn, wq, wuk, kvn, cos, sin)


def _mla_t_kernel(q_ref, k_ref, vt_ref, wuvt_ref, o_ref, s0_ref, s1_ref, p0_ref, p1_ref, m0_ref, m1_ref, oh_ref):
    n_heads, tq = q_ref.shape[1], q_ref.shape[2]
    nk, rc = s0_ref.shape
    n_items = (tq // rc) * n_heads
    s_refs, p_refs, m_refs = (s0_ref, s1_ref), (p0_ref, p1_ref), (m0_ref, m1_ref)
    tk = MLA_KEY_CHUNK if nk % MLA_KEY_CHUNK == 0 else 2 * LANES
    rg = 128

    def item(i):
        if isinstance(i, int):
            return i % n_heads, i // n_heads
        return lax.rem(i, n_heads), lax.div(i, n_heads)

    def round_(r, par):
        static = isinstance(r, int)
        do_qk = not static or r < n_items
        do_sm = not static or 1 <= r <= n_items
        do_pv = not static or 2 <= r <= n_items + 1
        if do_qk:
            h_q, c_q = item(r)
            row_q = c_q * rc if static else pl.multiple_of(c_q * rc, rc)
            q = q_ref[0, h_q, pl.ds(row_q, rc), :]
        if do_sm:
            m8 = m_refs[1 - par][...]
        if do_pv:
            acc = None
        for c in range(nk // tk):
            keys = slice(c * tk, (c + 1) * tk)
            if do_qk:
                st = _dot_nt(k_ref[0, keys, :], q)
                s_refs[par][keys, :] = st
                cm = jnp.max(st.reshape(tk // 8, 8, rc), axis=0)
                m_refs[par][...] = cm if c == 0 else jnp.maximum(m_refs[par][...], cm)
            if do_sm:
                for g in range(tk // rg):
                    m = m8 + _zero_of(st[g * rg:g * rg + 8, :]) if do_qk else m8
                    blk = slice(c * tk + g * rg, c * tk + (g + 1) * rg)
                    sb = s_refs[1 - par][blk, :].reshape(rg // 8, 8, rc)
                    p_refs[1 - par][blk, :] = jnp.exp2(sb - m[None]).reshape(rg, rc).astype(BF16)
            if do_pv:
                part = _dot(vt_ref[0, :, keys], p_refs[par][keys, :])
                acc = part if acc is None else acc + part
        if do_qk:
            mx = m_refs[par][...]
            m_refs[par][...] = jnp.broadcast_to(jnp.max(mx, axis=0, keepdims=True), mx.shape)
        if do_pv:
            h_v, c_v = item(r - 2)
            l = acc[MLA_KV_RANK + ONES_LANE:MLA_KV_RANK + ONES_LANE + 1, :]
            oh_ref[c_v, h_v] = (acc[0:MLA_KV_RANK, :] / l).astype(BF16)

    round_(0, 0)
    round_(1, 1)
    n_full = max(n_items - 2, 0)

    def pair(i, carry):
        r = 2 + 2 * i
        round_(r, 0)
        round_(r + 1, 1)
        return carry

    lax.fori_loop(0, n_full // 2, pair, 0)
    for r in range(2 + 2 * (n_full // 2), n_items):
        round_(r, r % 2)
    for r in range(max(n_items, 2), n_items + 2):
        round_(r, r % 2)

    for ci in range(tq // rc):
        ot = jnp.concatenate([oh_ref[ci, h] for h in range(n_heads)], axis=0)
        out_t = _dot(wuvt_ref[...], ot)
        o_ref[0, ci * rc:(ci + 1) * rc, :] = out_t.T.astype(o_ref.dtype)


def _mla_t_attention(q, k, vt, wuvt, tq):
    b, h, sq, _ = q.shape
    nk = k.shape[1]
    rc = min(MLA_ITEM_ROWS, tq)
    assert nk % (2 * LANES) == 0 and tq % rc == 0 and sq % tq == 0
    wo = wuvt.shape[0]
    one_buffer = pl.Buffered(1)
    return pl.pallas_call(
        _mla_t_kernel,
        out_shape=jax.ShapeDtypeStruct((b, sq, wo), BF16),
        grid=(b, sq // tq),
        in_specs=[pl.BlockSpec((1, h, tq, MLA_QK_PAD), lambda bi, i: (bi, 0, i, 0)),
                  pl.BlockSpec((1, nk, MLA_QK_PAD), lambda bi, i: (bi, 0, 0), pipeline_mode=one_buffer),
                  pl.BlockSpec((1, MLA_V_ROWS, nk), lambda bi, i: (bi, 0, 0), pipeline_mode=one_buffer),
                  _resident(wuvt.shape)],
        out_specs=pl.BlockSpec((1, tq, wo), lambda bi, i: (bi, i, 0)),
        scratch_shapes=[pltpu.VMEM((nk, rc), F32), pltpu.VMEM((nk, rc), F32),
                        pltpu.VMEM((nk, rc), BF16), pltpu.VMEM((nk, rc), BF16),
                        pltpu.VMEM((8, rc), F32), pltpu.VMEM((8, rc), F32),
                        pltpu.VMEM((tq // rc, h, MLA_KV_RANK, rc), BF16)],
        compiler_params=_cparams(("parallel", "arbitrary")),
        name="mla_attention",
    )(q, k, vt, wuvt)


def _split_heads(t):
    lane = lax.broadcasted_iota(jnp.int32, t.shape, 1)
    lo = lane < HEAD_DIM
    zero = jnp.zeros_like(t)
    return jnp.where(lo, t, zero), jnp.where(lo, zero, t)


def _stack_pair(lo_hi_ctx, lo_hi_lat):
    return jnp.concatenate([lo_hi_ctx[0], lo_hi_ctx[1], lo_hi_lat[0], lo_hi_lat[1]], axis=0)


def _na_kernel(q_ref, k_ref, v_ref, kc_ref, vc_ref, bias_ref, o_ref, *, n_rows):
    step = pl.program_id(2)
    nrb = n_rows // NA_RB
    n_keys = NA_SPAN * GRID_W
    n_ctx = kc_ref.shape[1]
    tq = NA_RB * GRID_W
    kcs = _split_heads(kc_ref[0])
    vcs = _split_heads(vc_ref[0])
    ind = _stack_pair((_pair_indicator(n_ctx, False), _pair_indicator(n_ctx, True)),
                      (_pair_indicator(n_keys, False), _pair_indicator(n_keys, True)))
    for i in range(NA_STEP_BLOCKS):
        rb = step * NA_STEP_BLOCKS + i
        case = jnp.minimum(rb, 1) + jnp.maximum(rb - (nrb - 2), 0)
        base = jnp.clip(rb * NA_RB - NA_KR // 2, 0, n_rows - NA_SPAN)
        start = pl.multiple_of(base * GRID_W, 256)
        q = q_ref[0, i * tq:(i + 1) * tq, :]
        k_st = _stack_pair(kcs, _split_heads(k_ref[0, pl.ds(start, n_keys), :]))
        v_st = _stack_pair(vcs, _split_heads(v_ref[0, pl.ds(start, n_keys), :]))
        s = _dot_nt(q, k_st)
        p_ctx, p_lat = [], []
        for par in range(2):
            t_c = _tiles(s[:, par * n_ctx:(par + 1) * n_ctx])
            off = 2 * n_ctx + par * n_keys
            t_l = _tiles(s[:, off:off + n_keys] + bias_ref[case, par])
            m = _rowmax128(t_c + t_l)
            p_ctx += _exp_tiles(t_c, m)
            p_lat += _exp_tiles(t_l, m)
        o = _dot(jnp.concatenate(p_ctx + p_lat, axis=1), jnp.concatenate([v_st, ind], axis=1))
        o_ref[0, i * tq:(i + 1) * tq, :] = (o[:, :LANES] / o[:, LANES:]).astype(o_ref.dtype)


def _na_attention(q, k, v, kc, vc, bias):
    b, s, w = q.shape
    n_rows = s // GRID_W
    nrb = n_rows // NA_RB
    tq = NA_STEP_BLOCKS * NA_RB * GRID_W
    n_ctx = kc.shape[1]
    return pl.pallas_call(
        functools.partial(_na_kernel, n_rows=n_rows),
        out_shape=jax.ShapeDtypeStruct((b, s, w), BF16),
        grid=(w // LANES, b, nrb // NA_STEP_BLOCKS),
        in_specs=[pl.BlockSpec((1, tq, LANES), lambda j, bi, r: (bi, r, j)),
                  pl.BlockSpec((1, s, LANES), lambda j, bi, r: (bi, 0, j)),
                  pl.BlockSpec((1, s, LANES), lambda j, bi, r: (bi, 0, j)),
                  pl.BlockSpec((1, n_ctx, LANES), lambda j, bi, r: (bi, 0, j)),
                  pl.BlockSpec((1, n_ctx, LANES), lambda j, bi, r: (bi, 0, j)),
                  pl.BlockSpec((3, 2, NA_RB * GRID_W, NA_SPAN * GRID_W), lambda j, bi, r: (0, j, 0, 0))],
        out_specs=pl.BlockSpec((1, tq, LANES), lambda j, bi, r: (bi, r, j)),
        compiler_params=_cparams(("parallel", "parallel", "arbitrary")),
        name="na_attention",
    )(q, k, v, kc, vc, bias)


def _pair_ctx_kernel(q_ref, kc_ref, vc_ref, o_ref):
    q = q_ref[0]
    n_ctx = kc_ref.shape[1]
    kcs = _split_heads(kc_ref[0])
    vcs = _split_heads(vc_ref[0])
    s = _dot_nt(q, jnp.concatenate(kcs, axis=0))
    p = []
    for par in range(2):
        t_c = _tiles(s[:, par * n_ctx:(par + 1) * n_ctx])
        p += _exp_tiles(t_c, _rowmax128(t_c))
    ind = jnp.concatenate([_pair_indicator(n_ctx, False), _pair_indicator(n_ctx, True)], axis=0)
    o = _dot(jnp.concatenate(p, axis=1), jnp.concatenate([jnp.concatenate(vcs, axis=0), ind], axis=1))
    o_ref[0] = (o[:, :LANES] / o[:, LANES:]).astype(o_ref.dtype)


def _pair_ctx_attention(q, kc, vc):
    b, n, w = q.shape
    spec = pl.BlockSpec((1, n, LANES), lambda bi, j: (bi, 0, j))
    return pl.pallas_call(
        _pair_ctx_kernel,
        out_shape=jax.ShapeDtypeStruct((b, n, w), BF16),
        grid=(b, w // LANES),
        in_specs=[spec, spec, spec],
        out_specs=spec,
        compiler_params=_cparams(("parallel", "parallel")),
        name="na_ctx_attention",
    )(q, kc, vc)


def _na_bias_table(rel_bias, n_rows):
    qc = np.arange(GRID_W)
    cstart = np.clip(qc - NA_KC // 2, 0, GRID_W - NA_KC)
    kc = np.arange(GRID_W)
    col_ok = (kc[None, :] >= cstart[:, None]) & (kc[None, :] < cstart[:, None] + NA_KC)
    n_dr, n_dc = 2 * NA_KR - 1, 2 * NA_KC - 1
    nh = rel_bias.shape[0]
    dc = kc[None, :] - qc[:, None] + NA_KC - 1
    sel_col = ((dc[None] == np.arange(n_dc)[:, None, None]) & col_ok[None]).astype(np.float32)
    toe = jnp.einsum('hdu,uqk->hdqk', rel_bias.astype(F32), jnp.asarray(sel_col), precision=lax.Precision.HIGHEST)
    toe = jnp.where(jnp.asarray(col_ok)[None, None], toe * LOG2_E, NEG_INF)
    toe = jnp.concatenate([toe, jnp.full((nh, 1, GRID_W, GRID_W), NEG_INF, F32)], axis=1)
    toe = jnp.concatenate([toe, toe], axis=-1)
    idx = np.full((3, NA_RB, NA_SPAN), n_dr, np.int32)
    nrb = n_rows // NA_RB
    for case, rb in enumerate((0, min(1, nrb - 1), nrb - 1)):
        r0 = rb * NA_RB
        base = int(np.clip(r0 - NA_KR // 2, 0, n_rows - NA_SPAN))
        for a in range(NA_RB):
            r = r0 + a
            rs = int(np.clip(r - NA_KR // 2, 0, n_rows - NA_KR))
            for t in range(NA_SPAN):
                kr = base + t
                if rs <= kr < rs + NA_KR:
                    idx[case, a, t] = kr - r + NA_KR - 1

    def assemble(toe_ref, o_ref):
        lo = lax.broadcasted_iota(jnp.int32, (GRID_W, LANES), 1) < GRID_W
        for case in range(3):
            for a in range(NA_RB):
                for tp in range(NA_SPAN // 2):
                    d0, d1 = int(idx[case, a, 2 * tp]), int(idx[case, a, 2 * tp + 1])
                    tile = toe_ref[0, d0] if d0 == d1 else jnp.where(lo, toe_ref[0, d0], toe_ref[0, d1])
                    o_ref[case, 0, a * GRID_W:(a + 1) * GRID_W, tp * LANES:(tp + 1) * LANES] = tile

    tq, tkeys = NA_RB * GRID_W, NA_SPAN * GRID_W
    return pl.pallas_call(
        assemble,
        out_shape=jax.ShapeDtypeStruct((3, nh, tq, tkeys), F32),
        grid=(nh,),
        in_specs=[pl.BlockSpec((1, n_dr + 1, GRID_W, LANES), lambda h: (h, 0, 0, 0))],
        out_specs=pl.BlockSpec((3, 1, tq, tkeys), lambda h: (0, h, 0, 0)),
        compiler_params=_cparams(("parallel",)),
        name="na_bias_table",
    )(toe)


def _ffn_kernel(*refs, n_attn, tf, final):
    h_ref, mod_ref = refs[0], refs[1]
    attn_refs = refs[2:2 + n_attn]
    wout_refs = refs[2 + n_attn:2 + 2 * n_attn]
    wgu_ref, wd_ref = refs[2 + 2 * n_attn:4 + 2 * n_attn]
    pos = 4 + 2 * n_attn
    fn_ref = refs[pos] if final else None
    pos += 1 if final else 0
    o_ref, a2_ref, f_ref = refs[pos:pos + 3]
    d = D_MODEL
    dff = wd_ref.shape[1]

    proj = _dot(attn_refs[0][0], wout_refs[0][...])
    for t in range(1, n_attn):
        proj = proj + _dot(attn_refs[t][0], wout_refs[t][...])
    h1 = h_ref[0] + mod_ref[0, :, 2 * d:3 * d] * proj
    o_ref[0] = h1
    a2_ref[...] = (_rms(h1) * (1.0 + mod_ref[0, :, 4 * d:5 * d]) + mod_ref[0, :, 3 * d:4 * d]).astype(BF16)

    for j in range(dff // tf):
        a2 = a2_ref[...]
        g = _dot(a2, wgu_ref[0, :, j * tf:(j + 1) * tf])
        u = _dot(a2, wgu_ref[0, :, dff + j * tf:dff + (j + 1) * tf])
        f_ref[:, j * tf:(j + 1) * tf] = (_silu(g) * u).astype(BF16)

    out = o_ref[0] + mod_ref[0, :, 5 * d:6 * d] * _dot(f_ref[...], wd_ref[0])
    if final:
        out = _rms(out) * fn_ref[...]
    o_ref[0] = out


def _resident(shape):
    return pl.BlockSpec(shape, lambda *_: (0,) * len(shape), pipeline_mode=pl.Buffered(1))


def _ffn_block(h, mods, grp, attns, wouts, wgu, wd, layer, final_norm, tm, tf):
    b, s, d = h.shape
    dff = wd.shape[1]
    n_attn = len(attns)
    final = final_norm is not None
    in_specs = [pl.BlockSpec((1, tm, d), lambda bi, i: (bi, i, 0)),
                pl.BlockSpec((1, 1, mods.shape[2]), lambda bi, i: (grp(bi), 0, 0))]
    in_specs += [pl.BlockSpec((1, tm, a.shape[2]), lambda bi, i: (bi, i, 0)) for a in attns]
    in_specs += [_resident(w.shape) for w in wouts]
    in_specs += [pl.BlockSpec((1,) + w.shape[1:], lambda *_: (layer, 0, 0), pipeline_mode=pl.Buffered(1))
                 for w in (wgu, wd)]
    args = [h, mods, *attns, *wouts, wgu, wd]
    if final:
        in_specs.append(_resident((1, d)))
        args.append(final_norm)
    return pl.pallas_call(
        functools.partial(_ffn_kernel, n_attn=n_attn, tf=tf, final=final),
        out_shape=jax.ShapeDtypeStruct((b, s, d), F32),
        grid=(b, s // tm),
        in_specs=in_specs,
        out_specs=pl.BlockSpec((1, tm, d), lambda bi, i: (bi, i, 0)),
        scratch_shapes=[pltpu.VMEM((tm, d), BF16), pltpu.VMEM((tm, dff), BF16)],
        compiler_params=_cparams(("parallel", "parallel")),
        name="outproj_ffn",
    )(*args)


def _inproj1_kernel(x_ref, mod_ref, win_ref, cos_ref, sin_ref, q_ref, k_ref, v_ref):
    d = D_MODEL
    x = x_ref[0]
    a = (_rms(x) * (1.0 + mod_ref[0, :, d:2 * d]) + mod_ref[0, :, 0:d]).astype(BF16)
    p = _dot(a, win_ref[...])
    cos = cos_ref[...]
    sin = sin_ref[...]
    nq = SWA_HEADS * HEAD_DIM
    nkv = 2 * SWA_KV_HEADS * LANES
    quarter = HEAD_DIM // 4
    for t in range(nq // LANES):
        q = _rope_tile(p[:, t * LANES:(t + 1) * LANES], cos, sin, quarter)
        q_ref[0, :, t * LANES:(t + 1) * LANES] = (q * (HEAD_DIM ** -0.5 * LOG2_E)).astype(BF16)
    for t in range(nkv // LANES):
        k = _rope_tile(p[:, nq + t * LANES:nq + (t + 1) * LANES], cos, sin, quarter)
        k_ref[0, :, t * LANES:(t + 1) * LANES] = k.astype(BF16)
    v_ref[0] = p[:, nq + nkv:nq + 2 * nkv].astype(BF16)


def _inproj1(x, mods, grp, win, cos, sin, tm):
    b, s, d = x.shape
    nq = SWA_HEADS * HEAD_DIM
    nkv = 2 * SWA_KV_HEADS * LANES
    return pl.pallas_call(
        _inproj1_kernel,
        out_shape=(jax.ShapeDtypeStruct((b, s, nq), BF16),
                   jax.ShapeDtypeStruct((b, s, nkv), BF16),
                   jax.ShapeDtypeStruct((b, s, nkv), BF16)),
        grid=(b, s // tm),
        in_specs=[pl.BlockSpec((1, tm, d), lambda bi, i: (bi, i, 0)),
                  pl.BlockSpec((1, 1, mods.shape[2]), lambda bi, i: (grp(bi), 0, 0)),
                  pl.BlockSpec(win.shape, lambda bi, i: (0, 0)),
                  pl.BlockSpec((tm, LANES), lambda bi, i: (i, 0)),
                  pl.BlockSpec((tm, LANES), lambda bi, i: (i, 0))],
        out_specs=(pl.BlockSpec((1, tm, nq), lambda bi, i: (bi, i, 0)),
                   pl.BlockSpec((1, tm, nkv), lambda bi, i: (bi, i, 0)),
                   pl.BlockSpec((1, tm, nkv), lambda bi, i: (bi, i, 0))),
        compiler_params=_cparams(("parallel", "parallel")),
        name="inproj_odd",
    )(x, mods, win, cos, sin)


def _swa_kernel(q_ref, km_ref, k0_ref, kp_ref, vm_ref, v0_ref, vp_ref, kc_ref, vc_ref, sink_ref, o_ref, *, n_steps):
    step = pl.program_id(2)
    blk = SWA_BLOCK
    npair = q_ref.shape[2] // LANES
    kall = jnp.concatenate([km_ref[0], k0_ref[0], kp_ref[0]], axis=0)
    vall = jnp.concatenate([vm_ref[0], v0_ref[0], vp_ref[0]], axis=0)
    kc = kc_ref[0]
    vc = vc_ref[0]
    n_ctx = kc.shape[0]
    kc_st = jnp.concatenate([kc[:, :LANES], kc[:, LANES:]], axis=0)
    vc_st = jnp.concatenate([vc[:, :LANES], vc[:, LANES:]], axis=0)
    ind = _stack_pair((_pair_indicator(n_ctx, False), _pair_indicator(n_ctx, True)),
                      (_pair_indicator(3 * blk, False), _pair_indicator(3 * blk, True)))
    r = lax.broadcasted_iota(jnp.int32, (blk, 3 * blk), 0)
    c = lax.broadcasted_iota(jnp.int32, (blk, 3 * blk), 1)
    cc = c % blk
    lane = lax.broadcasted_iota(jnp.int32, (npair * blk, LANES), 1)
    lo = lane < HEAD_DIM

    def band(left_off, right_off):
        ok = ((c >= blk) & (c < 2 * blk)) | ((c < blk) & (cc >= r + left_off)) | ((c >= 2 * blk) & (cc <= r - right_off))
        neg = jnp.where(ok, 0.0, NEG_INF).astype(F32)
        return jnp.concatenate([neg] * npair, axis=0)

    for t in range(SWA_GROUP):
        left_off = jnp.where(step >= 1, 0, blk) if t == 0 else 0
        right_off = jnp.where(step <= n_steps - 2, 0, blk) if t == SWA_GROUP - 1 else 0
        neg = band(left_off, right_off)
        qb = q_ref[0, t * blk:(t + 1) * blk, :]
        qst = jnp.concatenate([qb[:, i * LANES:(i + 1) * LANES] for i in range(npair)], axis=0)
        kw = kall[t * blk:(t + 3) * blk]
        vw = vall[t * blk:(t + 3) * blk]
        k_st = jnp.concatenate([kc_st, kw[:, :LANES], kw[:, LANES:]], axis=0)
        v_st = jnp.concatenate([vc_st, vw[:, :LANES], vw[:, LANES:]], axis=0)
        s = _dot_nt(qst, k_st)
        p_ctx, p_lat, p_sink = [], [], []
        for par in range(2):
            sink = sink_ref[0, par]
            t_c = _tiles(s[:, par * n_ctx:(par + 1) * n_ctx])
            off = 2 * n_ctx + par * 3 * blk
            t_l = _tiles(s[:, off:off + 3 * blk] + neg)
            m = jnp.maximum(_rowmax128(t_c + t_l), sink)
            p_ctx += _exp_tiles(t_c, m)
            p_lat += _exp_tiles(t_l, m)
            p_sink.append(jnp.exp2(sink - m))
        o = _dot(jnp.concatenate(p_ctx + p_lat, axis=1), jnp.concatenate([v_st, ind], axis=1))
        o = o[:, :LANES] / (o[:, LANES:] + jnp.where(lo, p_sink[0], p_sink[1]))
        for i in range(npair):
            o_ref[0, t * blk:(t + 1) * blk, i * LANES:(i + 1) * LANES] = o[i * blk:(i + 1) * blk].astype(o_ref.dtype)


def _swa_attention(q, kvar, vvar, kc, vc, sink_tab):
    b, s, nq = q.shape
    blk = SWA_BLOCK
    nb = s // blk
    n_steps = nb // SWA_GROUP
    gw = nq // SWA_KV_HEADS
    n_ctx = kc.shape[1]
    prev = lambda bi, g, j: (bi, jnp.maximum(SWA_GROUP * j - 1, 0), g)
    cur = lambda bi, g, j: (bi, j, g)
    nxt = lambda bi, g, j: (bi, jnp.minimum(SWA_GROUP * (j + 1), nb - 1), g)
    edge_spec = lambda f: pl.BlockSpec((1, blk, 2 * LANES), f)
    main_spec = pl.BlockSpec((1, SWA_GROUP * blk, 2 * LANES), cur)
    ctx_spec = pl.BlockSpec((1, n_ctx, 2 * LANES), lambda bi, g, j: (bi, 0, g))
    return pl.pallas_call(
        functools.partial(_swa_kernel, n_steps=n_steps),
        out_shape=jax.ShapeDtypeStruct((b, s, nq), BF16),
        grid=(b, SWA_KV_HEADS, n_steps),
        in_specs=[pl.BlockSpec((1, SWA_GROUP * blk, gw), cur),
                  edge_spec(prev), main_spec, edge_spec(nxt),
                  edge_spec(prev), main_spec, edge_spec(nxt),
                  ctx_spec, ctx_spec,
                  pl.BlockSpec((1, 2, sink_tab.shape[2], LANES), lambda bi, g, j: (g, 0, 0, 0))],
        out_specs=pl.BlockSpec((1, SWA_GROUP * blk, gw), cur),
        compiler_params=_cparams(("parallel", "parallel", "arbitrary")),
        name="swa_attention",
    )(q, kvar, kvar, kvar, vvar, vvar, vvar, kc, vc, sink_tab)


def _rope_tables(s, dim, pad_to, first_lane=0):
    n_rows = s // GRID_W
    half = dim // 2
    inv = ROPE_BASE ** (-jnp.arange(0, half, 2, dtype=F32) / half)
    ar = jnp.arange(n_rows, dtype=jnp.int32).astype(F32)[:, None] * inv
    ac = jnp.arange(GRID_W, dtype=jnp.int32).astype(F32)[:, None] * inv
    expand_r = lambda a: jnp.repeat(a, GRID_W, axis=0)
    expand_c = lambda a: jnp.tile(a, (n_rows, 1))
    cos = jnp.concatenate([expand_r(jnp.cos(ar))] * 2 + [expand_c(jnp.cos(ac))] * 2, axis=1)
    sin = jnp.concatenate([expand_r(-jnp.sin(ar)), expand_r(jnp.sin(ar)),
                           expand_c(-jnp.sin(ac)), expand_c(jnp.sin(ac))], axis=1)
    if pad_to > dim:
        before, after = first_lane, pad_to - dim - first_lane
        cos = jnp.concatenate([jnp.ones((s, before), F32), cos, jnp.ones((s, after), F32)], axis=1)
        sin = jnp.concatenate([jnp.zeros((s, before), F32), sin, jnp.zeros((s, after), F32)], axis=1)
    reps = LANES // cos.shape[1]
    return jnp.tile(cos, (1, reps)), jnp.tile(sin, (1, reps))


def _even_weights(w_in, w_q_up, w_uk, w_uv, w_out):
    d = w_in.shape[0]
    n_lat = MLA_Q_RANK + MLA_KV_RANK
    win = jnp.concatenate([w_in[:, :n_lat], jnp.zeros((d, ROPE_LANE), w_in.dtype), w_in[:, n_lat:MLA_IN],
                           jnp.zeros((d, LANES - ROPE_LANE - MLA_ROPE), w_in.dtype), w_in[:, MLA_IN:]], axis=1)
    wq3 = w_q_up.reshape(MLA_Q_RANK, MLA_HEADS, MLA_NOPE + MLA_ROPE)
    nope = wq3[:, :, :MLA_NOPE].reshape(MLA_Q_RANK, MLA_HEADS * MLA_NOPE)
    rope = jnp.pad(wq3[:, :, MLA_NOPE:], ((0, 0), (0, 0), (ROPE_LANE, LANES - ROPE_LANE - MLA_ROPE)))
    rope = rope.reshape(MLA_Q_RANK, MLA_HEADS * LANES)
    wq = jnp.concatenate([nope, rope], axis=1)
    eye = jnp.eye(MLA_HEADS, dtype=w_uk.dtype)
    wuk = jnp.einsum('hcn,hg->hngc', w_uk, eye).reshape(MLA_HEADS * MLA_NOPE, MLA_HEADS * MLA_KV_RANK)
    wuv = jnp.einsum('hcv,hg->hcgv', w_uv, eye).reshape(MLA_HEADS * MLA_KV_RANK, MLA_HEADS * MLA_V)
    n_mla = MLA_HEADS * MLA_V
    return (win.astype(BF16), wq.astype(BF16), wuk.astype(BF16), wuv.astype(BF16),
            w_out[:n_mla].astype(BF16), w_out[n_mla:].astype(BF16))


def _odd_weights(w_in):
    d = w_in.shape[0]
    nq = SWA_HEADS * HEAD_DIM
    z = jnp.zeros((d, HEAD_DIM), w_in.dtype)

    def variants(off):
        cols = []
        for g in range(SWA_KV_HEADS):
            w = w_in[:, off + g * HEAD_DIM:off + (g + 1) * HEAD_DIM]
            cols += [w, z, z, w]
        return jnp.concatenate(cols, axis=1)

    kcols = variants(nq)
    vcols = variants(nq + SWA_KV_HEADS * HEAD_DIM)
    return jnp.concatenate([w_in[:, :nq], kcols, vcols], axis=1).astype(BF16)


def _sink_table(sinks):
    g = SWA_HEADS // SWA_KV_HEADS
    t = (sinks.astype(F32) * LOG2_E).reshape(SWA_KV_HEADS, g // 2, 2).transpose(0, 2, 1)
    t = jnp.broadcast_to(t[:, :, :, None, None], (SWA_KV_HEADS, 2, g // 2, SWA_BLOCK, LANES))
    return t.reshape(SWA_KV_HEADS, 2, (g // 2) * SWA_BLOCK, LANES)


def kernel(x, c, ctx, c_ctx, mod_w, mod_b, even_w_in, mla_q_norm, mla_kv_norm, mla_w_q_up, mla_w_uk, mla_w_uv,
           na_rel_bias, even_w_out, odd_w_in, swa_sinks, odd_w_out, ffn_w_gate_up, ffn_w_down, final_norm):
    b, s, d = x.shape
    n_ctx = ctx.shape[1]
    n_rows = s // GRID_W
    assert d == D_MODEL and s % (NA_STEP_BLOCKS * NA_RB * GRID_W) == 0 and s % (SWA_GROUP * SWA_BLOCK) == 0
    assert mod_w.shape[0] == 2 and b <= 4

    ctx_grp = 4
    c8 = jnp.zeros((8, d), F32).at[:b].set(c.astype(F32)).at[ctx_grp].set(c_ctx.astype(F32))
    mods = _modulation(c8, mod_w.astype(F32), mod_b.astype(F32))
    mods0 = mods[0].reshape(8, 1, 6 * d)
    mods1 = mods[1].reshape(8, 1, 6 * d)
    lat_grp = lambda bi: bi
    ctx_g = lambda bi: ctx_grp

    tm = min(512, s)
    tf = 256
    wgu = ffn_w_gate_up.astype(BF16)
    wdn = ffn_w_down.astype(BF16)

    win, wq, wuk, wuv, wo_mla, wo_na = _even_weights(even_w_in[0], mla_w_q_up[0], mla_w_uk[0], mla_w_uv[0],
                                                      even_w_out[0])
    qn = mla_q_norm[0].astype(F32).reshape(1, -1)
    kvn = mla_kv_norm[0].astype(F32).reshape(1, -1)
    cos_m, sin_m = _rope_tables(s, MLA_ROPE, LANES, ROPE_LANE)
    one_c = jnp.ones((n_ctx, LANES), F32)
    zero_c = jnp.zeros((n_ctx, LANES), F32)

    q_l, k_l, naq_l, nak_l, nav_l = _inproj0(x, mods0, lat_grp, win, qn, wq, wuk, kvn, cos_m, sin_m, tm)
    q_c, k_c, naq_c, nak_c, nav_c = _inproj0(ctx, mods0, ctx_g, win, qn, wq, wuk, kvn, one_c, zero_c, n_ctx)

    k_all = jnp.concatenate([k_c, k_l], axis=1)
    vt_all = jnp.swapaxes(k_all[:, :, :MLA_V_ROWS], 1, 2)
    wuvt = wuv.T
    o_mla_l = _mla_t_attention(q_l, k_all, vt_all, wuvt, min(1024, s))
    o_mla_c = _mla_t_attention(q_c, k_c, vt_all[:, :, :n_ctx], wuvt, n_ctx)

    bias = _na_bias_table(na_rel_bias[0], n_rows)
    o_na_l = _na_attention(naq_l, nak_l, nav_l, nak_c, nav_c, bias)
    o_na_c = _pair_ctx_attention(naq_c, nak_c, nav_c)

    h_lat = _ffn_block(x, mods0, lat_grp, [o_mla_l, o_na_l], [wo_mla, wo_na], wgu, wdn, 0, None, tm, tf)
    h_ctx = _ffn_block(ctx, mods0, ctx_g, [o_mla_c, o_na_c], [wo_mla, wo_na], wgu, wdn, 0, None, n_ctx, tf)

    win1 = _odd_weights(odd_w_in[0])
    cos_s, sin_s = _rope_tables(s, HEAD_DIM, HEAD_DIM)
    q1, k1, v1 = _inproj1(h_lat, mods1, lat_grp, win1, cos_s, sin_s, tm)
    _, k1c, v1c = _inproj1(h_ctx, mods1, ctx_g, win1, one_c, zero_c, n_ctx)
    o_swa = _swa_attention(q1, k1, v1, k1c, v1c, _sink_table(swa_sinks[0]))
    fn = final_norm.astype(F32).reshape(1, d)
    return _ffn_block(h_lat, mods1, lat_grp, [o_swa], [odd_w_out[0].astype(BF16)], wgu, wdn, 1, fn, tm, tf)
```

```python
import functools

import numpy as np
import jax
import jax.numpy as jnp
from jax import lax
from jax.experimental import pallas as pl
from jax.experimental.pallas import tpu as pltpu

F32 = jnp.float32
BF16 = jnp.bfloat16

D_MODEL = 1024
GRID_W = 64
HEAD_DIM = 64
ROPE_BASE = 10000.0
EPS = 1e-6
NEG_INF = -1e30
LOG2_E = 1.4426950408889634

MLA_HEADS = 8
MLA_NOPE = 64
MLA_ROPE = 32
MLA_V = 64
MLA_Q_RANK = 256
MLA_KV_RANK = 128
MLA_IN = MLA_Q_RANK + MLA_KV_RANK + MLA_ROPE

NA_HEADS = 8
NA_KR = 8
NA_KC = 16
NA_RB = 4
NA_SPAN = 12
NA_STEP_BLOCKS = 16

SWA_HEADS = 16
SWA_KV_HEADS = 2
SWA_WINDOW = 128
SWA_BLOCK = 128
SWA_GROUP = 16

LANES = 128
MLA_QK_PAD = 256
MLA_ITEM_ROWS = 256
MLA_KEY_CHUNK = 768
ONES_LANE = 0
ROPE_LANE = 8
MLA_V_ROWS = 144
VMEM_LIMIT = 56 * 1024 * 1024


def _cparams(sem):
    return pltpu.CompilerParams(dimension_semantics=sem, vmem_limit_bytes=VMEM_LIMIT)


def _dot(a, b):
    return jnp.dot(a, b, preferred_element_type=F32)


def _dot_nt(a, b):
    return lax.dot_general(a, b, (((1,), (1,)), ((), ())), preferred_element_type=F32)


def _rms(x):
    return x * lax.rsqrt(jnp.mean(x * x, axis=-1, keepdims=True) + EPS)


def _silu(x):
    return x * (1.0 / (1.0 + jnp.exp(-x)))


def _rope_tile(x, cos, sin, quarter, first_lane=0):
    lane = lax.broadcasted_iota(jnp.int32, x.shape, 1)
    first = ((lane + (2 * quarter - first_lane)) % (2 * quarter)) < quarter
    swapped = jnp.where(first, pltpu.roll(x, LANES - quarter, 1), pltpu.roll(x, quarter, 1))
    return x * cos + swapped * sin


def _zero_of(x):
    u = lax.bitcast_convert_type(x, jnp.uint32)
    u = lax.shift_right_logical(lax.shift_right_logical(u, jnp.uint32(16)), jnp.uint32(16))
    return lax.bitcast_convert_type(u, F32)


def _tiles(s):
    return [s[:, j * LANES:(j + 1) * LANES] for j in range(s.shape[1] // LANES)]


def _rowmax128(tiles):
    mx = tiles[0]
    for t in tiles[1:]:
        mx = jnp.maximum(mx, t)
    return jnp.broadcast_to(jnp.max(mx, axis=1, keepdims=True), mx.shape)


def _exp_tiles(tiles, m):
    return [jnp.exp2(t - m).astype(BF16) for t in tiles]


def _pair_indicator(n, hi):
    lane = lax.broadcasted_iota(jnp.int32, (n, LANES), 1)
    return jnp.where((lane >= HEAD_DIM) == hi, 1.0, 0.0).astype(BF16)


def _mod_kernel(c_ref, w_ref, b_ref, o_ref):
    a = _silu(c_ref[...])
    o_ref[0] = jnp.dot(a, w_ref[0], precision=lax.Precision.HIGHEST, preferred_element_type=F32) + b_ref[0]


def _modulation(c8, mod_w, mod_b):
    depth, d, n = mod_w.shape
    tn = 1536
    return pl.pallas_call(
        _mod_kernel,
        out_shape=jax.ShapeDtypeStruct((depth, 8, n), F32),
        grid=(depth, n // tn),
        in_specs=[pl.BlockSpec((8, d), lambda l, j: (0, 0)),
                  pl.BlockSpec((1, d, tn), lambda l, j: (l, 0, j)),
                  pl.BlockSpec((1, 1, tn), lambda l, j: (l, 0, j))],
        out_specs=pl.BlockSpec((1, 8, tn), lambda l, j: (l, 0, j)),
        compiler_params=_cparams(("parallel", "parallel")),
        name="modulation",
    )(c8, mod_w, mod_b.reshape(depth, 1, n))


def _inproj0_kernel(x_ref, mod_ref, win_ref, qn_ref, wq_ref, wuk_ref, kvn_ref, cos_ref, sin_ref,
                    qmla_ref, kmla_ref, naq_ref, nak_ref, nav_ref):
    d = D_MODEL
    x = x_ref[0]
    shift = mod_ref[0, :, 0:d]
    scale = mod_ref[0, :, d:2 * d]
    a = (_rms(x) * (1.0 + scale) + shift).astype(BF16)
    p = _dot(a, win_ref[...])
    cos = cos_ref[...]
    sin = sin_ref[...]
    mla_scale = (MLA_NOPE + MLA_ROPE) ** -0.5 * LOG2_E

    cq = (_rms(p[:, 0:MLA_Q_RANK]) * qn_ref[...]).astype(BF16)
    q = _dot(cq, wq_ref[...])
    n_nope = MLA_HEADS * MLA_NOPE
    q_lat = _dot(q[:, 0:n_nope].astype(BF16), wuk_ref[...])
    for h in range(MLA_HEADS):
        qr = _rope_tile(q[:, n_nope + h * LANES:n_nope + (h + 1) * LANES], cos, sin, MLA_ROPE // 4, ROPE_LANE)
        qmla_ref[0, h, :, 0:LANES] = (q_lat[:, h * LANES:(h + 1) * LANES] * mla_scale).astype(BF16)
        qmla_ref[0, h, :, LANES:2 * LANES] = (qr * mla_scale).astype(BF16)

    ckv = _rms(p[:, MLA_Q_RANK:MLA_Q_RANK + MLA_KV_RANK]) * kvn_ref[...]
    kr = _rope_tile(p[:, 3 * LANES:4 * LANES], cos, sin, MLA_ROPE // 4, ROPE_LANE)
    lane = lax.broadcasted_iota(jnp.int32, kr.shape, 1)
    kr = jnp.where(lane == ONES_LANE, 1.0, kr)
    kmla_ref[0, :, 0:LANES] = ckv.astype(BF16)
    kmla_ref[0, :, LANES:2 * LANES] = kr.astype(BF16)

    w = NA_HEADS * HEAD_DIM
    naq_ref[0] = (p[:, 4 * LANES:4 * LANES + w] * (HEAD_DIM ** -0.5 * LOG2_E)).astype(BF16)
    nak_ref[0] = p[:, 4 * LANES + w:4 * LANES + 2 * w].astype(BF16)
    nav_ref[0] = p[:, 4 * LANES + 2 * w:4 * LANES + 3 * w].astype(BF16)


def _inproj0(x, mods, grp, win, qn, wq, wuk, kvn, cos, sin, tm):
    b, s, d = x.shape
    nt = s // tm
    w = NA_HEADS * HEAD_DIM
    const = lambda bi, i: (0, 0)
    return pl.pallas_call(
        _inproj0_kernel,
        out_shape=(jax.ShapeDtypeStruct((b, MLA_HEADS, s, MLA_QK_PAD), BF16),
                   jax.ShapeDtypeStruct((b, s, MLA_QK_PAD), BF16),
                   jax.ShapeDtypeStruct((b, s, w), BF16),
                   jax.ShapeDtypeStruct((b, s, w), BF16),
                   jax.ShapeDtypeStruct((b, s, w), BF16)),
        grid=(b, nt),
        in_specs=[pl.BlockSpec((1, tm, d), lambda bi, i: (bi, i, 0)),
                  pl.BlockSpec((1, 1, mods.shape[2]), lambda bi, i: (grp(bi), 0, 0)),
                  pl.BlockSpec(win.shape, const),
                  pl.BlockSpec(qn.shape, const),
                  pl.BlockSpec(wq.shape, const),
                  pl.BlockSpec(wuk.shape, const),
                  pl.BlockSpec(kvn.shape, const),
                  pl.BlockSpec((tm, LANES), lambda bi, i: (i, 0)),
                  pl.BlockSpec((tm, LANES), lambda bi, i: (i, 0))],
        out_specs=(pl.BlockSpec((1, MLA_HEADS, tm, MLA_QK_PAD), lambda bi, i: (bi, 0, i, 0)),
                   pl.BlockSpec((1, tm, MLA_QK_PAD), lambda bi, i: (bi, i, 0)),
                   pl.BlockSpec((1, tm, w), lambda bi, i: (bi, i, 0)),
                   pl.BlockSpec((1, tm, w), lambda bi, i: (bi, i, 0)),
                   pl.BlockSpec((1, tm, w), lambda bi, i: (bi, i, 0))),
        compiler_params=_cparams(("parallel", "parallel")),
        name="inproj_even",
    )(x, mods, win, qn, wq, wuk, kvn, cos, sin)


def _mla_t_kernel(q_ref, k_ref, vt_ref, wuvt_ref, o_ref, s0_ref, s1_ref, p0_ref, p1_ref, m0_ref, m1_ref, oh_ref):
    n_heads, tq = q_ref.shape[1], q_ref.shape[2]
    nk, rc = s0_ref.shape
    n_items = (tq // rc) * n_heads
    s_refs, p_refs, m_refs = (s0_ref, s1_ref), (p0_ref, p1_ref), (m0_ref, m1_ref)
    tk = MLA_KEY_CHUNK if nk % MLA_KEY_CHUNK == 0 else 2 * LANES
    rg = 128

    def item(i):
        if isinstance(i, int):
            return i % n_heads, i // n_heads
        return lax.rem(i, n_heads), lax.div(i, n_heads)

    def round_(r, par):
        static = isinstance(r, int)
        do_qk = not static or r < n_items
        do_sm = not static or 1 <= r <= n_items
        do_pv = not static or 2 <= r <= n_items + 1
        if do_qk:
            h_q, c_q = item(r)
            row_q = c_q * rc if static else pl.multiple_of(c_q * rc, rc)
            q = q_ref[0, h_q, pl.ds(row_q, rc), :]
        if do_sm:
            m8 = m_refs[1 - par][...]
        if do_pv:
            acc = None
        for c in range(nk // tk):
            keys = slice(c * tk, (c + 1) * tk)
            if do_qk:
                st = _dot_nt(k_ref[0, keys, :], q)
                s_refs[par][keys, :] = st
                cm = jnp.max(st.reshape(tk // 8, 8, rc), axis=0)
                m_refs[par][...] = cm if c == 0 else jnp.maximum(m_refs[par][...], cm)
            if do_sm:
                for g in range(tk // rg):
                    m = m8 + _zero_of(st[g * rg:g * rg + 8, :]) if do_qk else m8
                    blk = slice(c * tk + g * rg, c * tk + (g + 1) * rg)
                    sb = s_refs[1 - par][blk, :].reshape(rg // 8, 8, rc)
                    p_refs[1 - par][blk, :] = jnp.exp2(sb - m[None]).reshape(rg, rc).astype(BF16)
            if do_pv:
                part = _dot(vt_ref[0, :, keys], p_refs[par][keys, :])
                acc = part if acc is None else acc + part
        if do_qk:
            mx = m_refs[par][...]
            m_refs[par][...] = jnp.broadcast_to(jnp.max(mx, axis=0, keepdims=True), mx.shape)
        if do_pv:
            h_v, c_v = item(r - 2)
            l = acc[MLA_KV_RANK + ONES_LANE:MLA_KV_RANK + ONES_LANE + 1, :]
            oh_ref[c_v, h_v] = (acc[0:MLA_KV_RANK, :] / l).astype(BF16)

    round_(0, 0)
    round_(1, 1)
    n_full = max(n_items - 2, 0)

    def pair(i, carry):
        r = 2 + 2 * i
        round_(r, 0)
        round_(r + 1, 1)
        return carry

    lax.fori_loop(0, n_full // 2, pair, 0)
    for r in range(2 + 2 * (n_full // 2), n_items):
        round_(r, r % 2)
    for r in range(max(n_items, 2), n_items + 2):
        round_(r, r % 2)

    for ci in range(tq // rc):
        out_t = jnp.concatenate([_dot(wuvt_ref[h], oh_ref[ci, h]) for h in range(n_heads)], axis=0)
        o_ref[0, ci * rc:(ci + 1) * rc, :] = out_t.T.astype(o_ref.dtype)


def _mla_t_attention(q, k, vt, wuvt, tq):
    b, h, sq, _ = q.shape
    nk = k.shape[1]
    rc = min(MLA_ITEM_ROWS, tq)
    assert nk % (2 * LANES) == 0 and tq % rc == 0 and sq % tq == 0
    wo = wuvt.shape[0] * wuvt.shape[1]
    one_buffer = pl.Buffered(1)
    return pl.pallas_call(
        _mla_t_kernel,
        out_shape=jax.ShapeDtypeStruct((b, sq, wo), BF16),
        grid=(b, sq // tq),
        in_specs=[pl.BlockSpec((1, h, tq, MLA_QK_PAD), lambda bi, i: (bi, 0, i, 0)),
                  pl.BlockSpec((1, nk, MLA_QK_PAD), lambda bi, i: (bi, 0, 0), pipeline_mode=one_buffer),
                  pl.BlockSpec((1, MLA_V_ROWS, nk), lambda bi, i: (bi, 0, 0), pipeline_mode=one_buffer),
                  _resident(wuvt.shape)],
        out_specs=pl.BlockSpec((1, tq, wo), lambda bi, i: (bi, i, 0)),
        scratch_shapes=[pltpu.VMEM((nk, rc), F32), pltpu.VMEM((nk, rc), F32),
                        pltpu.VMEM((nk, rc), BF16), pltpu.VMEM((nk, rc), BF16),
                        pltpu.VMEM((8, rc), F32), pltpu.VMEM((8, rc), F32),
                        pltpu.VMEM((tq // rc, h, MLA_KV_RANK, rc), BF16)],
        compiler_params=_cparams(("parallel", "arbitrary")),
        name="mla_attention",
    )(q, k, vt, wuvt)


def _split_heads(t):
    lane = lax.broadcasted_iota(jnp.int32, t.shape, 1)
    lo = lane < HEAD_DIM
    zero = jnp.zeros_like(t)
    return jnp.where(lo, t, zero), jnp.where(lo, zero, t)


def _stack_pair(lo_hi_ctx, lo_hi_lat):
    return jnp.concatenate([lo_hi_ctx[0], lo_hi_ctx[1], lo_hi_lat[0], lo_hi_lat[1]], axis=0)


def _na_kernel(q_ref, k_ref, v_ref, kc_ref, vc_ref, bias_ref, o_ref, *, n_rows):
    step = pl.program_id(2)
    nrb = n_rows // NA_RB
    n_keys = NA_SPAN * GRID_W
    n_ctx = kc_ref.shape[1]
    tq = NA_RB * GRID_W
    kcs = _split_heads(kc_ref[0])
    vcs = _split_heads(vc_ref[0])
    ind = _stack_pair((_pair_indicator(n_ctx, False), _pair_indicator(n_ctx, True)),
                      (_pair_indicator(n_keys, False), _pair_indicator(n_keys, True)))
    for i in range(NA_STEP_BLOCKS):
        rb = step * NA_STEP_BLOCKS + i
        case = jnp.minimum(rb, 1) + jnp.maximum(rb - (nrb - 2), 0)
        base = jnp.clip(rb * NA_RB - NA_KR // 2, 0, n_rows - NA_SPAN)
        start = pl.multiple_of(base * GRID_W, 256)
        q = q_ref[0, i * tq:(i + 1) * tq, :]
        k_st = _stack_pair(kcs, _split_heads(k_ref[0, pl.ds(start, n_keys), :]))
        v_st = _stack_pair(vcs, _split_heads(v_ref[0, pl.ds(start, n_keys), :]))
        s = _dot_nt(q, k_st)
        p_ctx, p_lat = [], []
        for par in range(2):
            t_c = _tiles(s[:, par * n_ctx:(par + 1) * n_ctx])
            off = 2 * n_ctx + par * n_keys
            t_l = _tiles(s[:, off:off + n_keys] + bias_ref[case, par])
            m = _rowmax128(t_c + t_l)
            p_ctx += _exp_tiles(t_c, m)
            p_lat += _exp_tiles(t_l, m)
        o = _dot(jnp.concatenate(p_ctx + p_lat, axis=1), jnp.concatenate([v_st, ind], axis=1))
        o_ref[0, i * tq:(i + 1) * tq, :] = (o[:, :LANES] / o[:, LANES:]).astype(o_ref.dtype)


def _na_attention(q, k, v, kc, vc, bias):
    b, s, w = q.shape
    n_rows = s // GRID_W
    nrb = n_rows // NA_RB
    tq = NA_STEP_BLOCKS * NA_RB * GRID_W
    n_ctx = kc.shape[1]
    return pl.pallas_call(
        functools.partial(_na_kernel, n_rows=n_rows),
        out_shape=jax.ShapeDtypeStruct((b, s, w), BF16),
        grid=(w // LANES, b, nrb // NA_STEP_BLOCKS),
        in_specs=[pl.BlockSpec((1, tq, LANES), lambda j, bi, r: (bi, r, j)),
                  pl.BlockSpec((1, s, LANES), lambda j, bi, r: (bi, 0, j)),
                  pl.BlockSpec((1, s, LANES), lambda j, bi, r: (bi, 0, j)),
                  pl.BlockSpec((1, n_ctx, LANES), lambda j, bi, r: (bi, 0, j)),
                  pl.BlockSpec((1, n_ctx, LANES), lambda j, bi, r: (bi, 0, j)),
                  pl.BlockSpec((3, 2, NA_RB * GRID_W, NA_SPAN * GRID_W), lambda j, bi, r: (0, j, 0, 0))],
        out_specs=pl.BlockSpec((1, tq, LANES), lambda j, bi, r: (bi, r, j)),
        compiler_params=_cparams(("parallel", "parallel", "arbitrary")),
        name="na_attention",
    )(q, k, v, kc, vc, bias)


def _pair_ctx_kernel(q_ref, kc_ref, vc_ref, o_ref):
    q = q_ref[0]
    n_ctx = kc_ref.shape[1]
    kcs = _split_heads(kc_ref[0])
    vcs = _split_heads(vc_ref[0])
    s = _dot_nt(q, jnp.concatenate(kcs, axis=0))
    p = []
    for par in range(2):
        t_c = _tiles(s[:, par * n_ctx:(par + 1) * n_ctx])
        p += _exp_tiles(t_c, _rowmax128(t_c))
    ind = jnp.concatenate([_pair_indicator(n_ctx, False), _pair_indicator(n_ctx, True)], axis=0)
    o = _dot(jnp.concatenate(p, axis=1), jnp.concatenate([jnp.concatenate(vcs, axis=0), ind], axis=1))
    o_ref[0] = (o[:, :LANES] / o[:, LANES:]).astype(o_ref.dtype)


def _pair_ctx_attention(q, kc, vc):
    b, n, w = q.shape
    spec = pl.BlockSpec((1, n, LANES), lambda bi, j: (bi, 0, j))
    return pl.pallas_call(
        _pair_ctx_kernel,
        out_shape=jax.ShapeDtypeStruct((b, n, w), BF16),
        grid=(b, w // LANES),
        in_specs=[spec, spec, spec],
        out_specs=spec,
        compiler_params=_cparams(("parallel", "parallel")),
        name="na_ctx_attention",
    )(q, kc, vc)


def _na_bias_table(rel_bias, n_rows):
    qc = np.arange(GRID_W)
    cstart = np.clip(qc - NA_KC // 2, 0, GRID_W - NA_KC)
    kc = np.arange(GRID_W)
    col_ok = (kc[None, :] >= cstart[:, None]) & (kc[None, :] < cstart[:, None] + NA_KC)
    n_dr, n_dc = 2 * NA_KR - 1, 2 * NA_KC - 1
    nh = rel_bias.shape[0]
    dc = kc[None, :] - qc[:, None] + NA_KC - 1
    sel_col = ((dc[None] == np.arange(n_dc)[:, None, None]) & col_ok[None]).astype(np.float32)
    toe = jnp.einsum('hdu,uqk->hdqk', rel_bias.astype(F32), jnp.asarray(sel_col), precision=lax.Precision.HIGHEST)
    toe = jnp.where(jnp.asarray(col_ok)[None, None], toe * LOG2_E, NEG_INF)
    toe = jnp.concatenate([toe, jnp.full((nh, 1, GRID_W, GRID_W), NEG_INF, F32)], axis=1)
    toe = jnp.concatenate([toe, toe], axis=-1)
    idx = np.full((3, NA_RB, NA_SPAN), n_dr, np.int32)
    nrb = n_rows // NA_RB
    for case, rb in enumerate((0, min(1, nrb - 1), nrb - 1)):
        r0 = rb * NA_RB
        base = int(np.clip(r0 - NA_KR // 2, 0, n_rows - NA_SPAN))
        for a in range(NA_RB):
            r = r0 + a
            rs = int(np.clip(r - NA_KR // 2, 0, n_rows - NA_KR))
            for t in range(NA_SPAN):
                kr = base + t
                if rs <= kr < rs + NA_KR:
                    idx[case, a, t] = kr - r + NA_KR - 1

    def assemble(toe_ref, o_ref):
        lo = lax.broadcasted_iota(jnp.int32, (GRID_W, LANES), 1) < GRID_W
        for case in range(3):
            for a in range(NA_RB):
                for tp in range(NA_SPAN // 2):
                    d0, d1 = int(idx[case, a, 2 * tp]), int(idx[case, a, 2 * tp + 1])
                    tile = toe_ref[0, d0] if d0 == d1 else jnp.where(lo, toe_ref[0, d0], toe_ref[0, d1])
                    o_ref[case, 0, a * GRID_W:(a + 1) * GRID_W, tp * LANES:(tp + 1) * LANES] = tile

    tq, tkeys = NA_RB * GRID_W, NA_SPAN * GRID_W
    return pl.pallas_call(
        assemble,
        out_shape=jax.ShapeDtypeStruct((3, nh, tq, tkeys), F32),
        grid=(nh,),
        in_specs=[pl.BlockSpec((1, n_dr + 1, GRID_W, LANES), lambda h: (h, 0, 0, 0))],
        out_specs=pl.BlockSpec((3, 1, tq, tkeys), lambda h: (0, h, 0, 0)),
        compiler_params=_cparams(("parallel",)),
        name="na_bias_table",
    )(toe)


def _ffn_kernel(*refs, n_attn, tf, final):
    h_ref, mod_ref = refs[0], refs[1]
    attn_refs = refs[2:2 + n_attn]
    wout_refs = refs[2 + n_attn:2 + 2 * n_attn]
    wgu_ref, wd_ref = refs[2 + 2 * n_attn:4 + 2 * n_attn]
    pos = 4 + 2 * n_attn
    fn_ref = refs[pos] if final else None
    pos += 1 if final else 0
    o_ref, a2_ref, f_ref = refs[pos:pos + 3]
    d = D_MODEL
    dff = wd_ref.shape[1]

    proj = _dot(attn_refs[0][0], wout_refs[0][...])
    for t in range(1, n_attn):
        proj = proj + _dot(attn_refs[t][0], wout_refs[t][...])
    h1 = h_ref[0] + mod_ref[0, :, 2 * d:3 * d] * proj
    o_ref[0] = h1
    a2_ref[...] = (_rms(h1) * (1.0 + mod_ref[0, :, 4 * d:5 * d]) + mod_ref[0, :, 3 * d:4 * d]).astype(BF16)

    for j in range(dff // tf):
        a2 = a2_ref[...]
        g = _dot(a2, wgu_ref[0, :, j * tf:(j + 1) * tf])
        u = _dot(a2, wgu_ref[0, :, dff + j * tf:dff + (j + 1) * tf])
        f_ref[:, j * tf:(j + 1) * tf] = (_silu(g) * u).astype(BF16)

    out = o_ref[0] + mod_ref[0, :, 5 * d:6 * d] * _dot(f_ref[...], wd_ref[0])
    if final:
        out = _rms(out) * fn_ref[...]
    o_ref[0] = out


def _resident(shape):
    return pl.BlockSpec(shape, lambda *_: (0,) * len(shape), pipeline_mode=pl.Buffered(1))


def _ffn_block(h, mods, grp, attns, wouts, wgu, wd, layer, final_norm, tm, tf):
    b, s, d = h.shape
    dff = wd.shape[1]
    n_attn = len(attns)
    final = final_norm is not None
    in_specs = [pl.BlockSpec((1, tm, d), lambda bi, i: (bi, i, 0)),
                pl.BlockSpec((1, 1, mods.shape[2]), lambda bi, i: (grp(bi), 0, 0))]
    in_specs += [pl.BlockSpec((1, tm, a.shape[2]), lambda bi, i: (bi, i, 0)) for a in attns]
    in_specs += [_resident(w.shape) for w in wouts]
    in_specs += [pl.BlockSpec((1,) + w.shape[1:], lambda *_: (layer, 0, 0), pipeline_mode=pl.Buffered(1))
                 for w in (wgu, wd)]
    args = [h, mods, *attns, *wouts, wgu, wd]
    if final:
        in_specs.append(_resident((1, d)))
        args.append(final_norm)
    return pl.pallas_call(
        functools.partial(_ffn_kernel, n_attn=n_attn, tf=tf, final=final),
        out_shape=jax.ShapeDtypeStruct((b, s, d), F32),
        grid=(b, s // tm),
        in_specs=in_specs,
        out_specs=pl.BlockSpec((1, tm, d), lambda bi, i: (bi, i, 0)),
        scratch_shapes=[pltpu.VMEM((tm, d), BF16), pltpu.VMEM((tm, dff), BF16)],
        compiler_params=_cparams(("parallel", "parallel")),
        name="outproj_ffn",
    )(*args)


def _inproj1_kernel(x_ref, mod_ref, win_ref, cos_ref, sin_ref, q_ref, k_ref, v_ref):
    d = D_MODEL
    x = x_ref[0]
    a = (_rms(x) * (1.0 + mod_ref[0, :, d:2 * d]) + mod_ref[0, :, 0:d]).astype(BF16)
    p = _dot(a, win_ref[...])
    cos = cos_ref[...]
    sin = sin_ref[...]
    nq = SWA_HEADS * HEAD_DIM
    nkv = 2 * SWA_KV_HEADS * LANES
    quarter = HEAD_DIM // 4
    for t in range(nq // LANES):
        q = _rope_tile(p[:, t * LANES:(t + 1) * LANES], cos, sin, quarter)
        q_ref[0, :, t * LANES:(t + 1) * LANES] = (q * (HEAD_DIM ** -0.5 * LOG2_E)).astype(BF16)
    for t in range(nkv // LANES):
        k = _rope_tile(p[:, nq + t * LANES:nq + (t + 1) * LANES], cos, sin, quarter)
        k_ref[0, :, t * LANES:(t + 1) * LANES] = k.astype(BF16)
    v_ref[0] = p[:, nq + nkv:nq + 2 * nkv].astype(BF16)


def _inproj1(x, mods, grp, win, cos, sin, tm):
    b, s, d = x.shape
    nq = SWA_HEADS * HEAD_DIM
    nkv = 2 * SWA_KV_HEADS * LANES
    return pl.pallas_call(
        _inproj1_kernel,
        out_shape=(jax.ShapeDtypeStruct((b, s, nq), BF16),
                   jax.ShapeDtypeStruct((b, s, nkv), BF16),
                   jax.ShapeDtypeStruct((b, s, nkv), BF16)),
        grid=(b, s // tm),
        in_specs=[pl.BlockSpec((1, tm, d), lambda bi, i: (bi, i, 0)),
                  pl.BlockSpec((1, 1, mods.shape[2]), lambda bi, i: (grp(bi), 0, 0)),
                  pl.BlockSpec(win.shape, lambda bi, i: (0, 0)),
                  pl.BlockSpec((tm, LANES), lambda bi, i: (i, 0)),
                  pl.BlockSpec((tm, LANES), lambda bi, i: (i, 0))],
        out_specs=(pl.BlockSpec((1, tm, nq), lambda bi, i: (bi, i, 0)),
                   pl.BlockSpec((1, tm, nkv), lambda bi, i: (bi, i, 0)),
                   pl.BlockSpec((1, tm, nkv), lambda bi, i: (bi, i, 0))),
        compiler_params=_cparams(("parallel", "parallel")),
        name="inproj_odd",
    )(x, mods, win, cos, sin)


def _swa_kernel(q_ref, km_ref, k0_ref, kp_ref, vm_ref, v0_ref, vp_ref, kc_ref, vc_ref, sink_ref, o_ref, *, n_steps):
    step = pl.program_id(2)
    blk = SWA_BLOCK
    npair = q_ref.shape[2] // LANES
    kall = jnp.concatenate([km_ref[0], k0_ref[0], kp_ref[0]], axis=0)
    vall = jnp.concatenate([vm_ref[0], v0_ref[0], vp_ref[0]], axis=0)
    kc = kc_ref[0]
    vc = vc_ref[0]
    n_ctx = kc.shape[0]
    kc_st = jnp.concatenate([kc[:, :LANES], kc[:, LANES:]], axis=0)
    vc_st = jnp.concatenate([vc[:, :LANES], vc[:, LANES:]], axis=0)
    ind = _stack_pair((_pair_indicator(n_ctx, False), _pair_indicator(n_ctx, True)),
                      (_pair_indicator(3 * blk, False), _pair_indicator(3 * blk, True)))
    r = lax.broadcasted_iota(jnp.int32, (blk, 3 * blk), 0)
    c = lax.broadcasted_iota(jnp.int32, (blk, 3 * blk), 1)
    cc = c % blk
    lane = lax.broadcasted_iota(jnp.int32, (npair * blk, LANES), 1)
    lo = lane < HEAD_DIM

    def band(left_off, right_off):
        ok = ((c >= blk) & (c < 2 * blk)) | ((c < blk) & (cc >= r + left_off)) | ((c >= 2 * blk) & (cc <= r - right_off))
        neg = jnp.where(ok, 0.0, NEG_INF).astype(F32)
        return jnp.concatenate([neg] * npair, axis=0)

    for t in range(SWA_GROUP):
        left_off = jnp.where(step >= 1, 0, blk) if t == 0 else 0
        right_off = jnp.where(step <= n_steps - 2, 0, blk) if t == SWA_GROUP - 1 else 0
        neg = band(left_off, right_off)
        qb = q_ref[0, t * blk:(t + 1) * blk, :]
        qst = jnp.concatenate([qb[:, i * LANES:(i + 1) * LANES] for i in range(npair)], axis=0)
        kw = kall[t * blk:(t + 3) * blk]
        vw = vall[t * blk:(t + 3) * blk]
        k_st = jnp.concatenate([kc_st, kw[:, :LANES], kw[:, LANES:]], axis=0)
        v_st = jnp.concatenate([vc_st, vw[:, :LANES], vw[:, LANES:]], axis=0)
        s = _dot_nt(qst, k_st)
        p_ctx, p_lat, p_sink = [], [], []
        for par in range(2):
            sink = sink_ref[0, par]
            t_c = _tiles(s[:, par * n_ctx:(par + 1) * n_ctx])
            off = 2 * n_ctx + par * 3 * blk
            t_l = _tiles(s[:, off:off + 3 * blk] + neg)
            m = jnp.maximum(_rowmax128(t_c + t_l), sink)
            p_ctx += _exp_tiles(t_c, m)
            p_lat += _exp_tiles(t_l, m)
            p_sink.append(jnp.exp2(sink - m))
        o = _dot(jnp.concatenate(p_ctx + p_lat, axis=1), jnp.concatenate([v_st, ind], axis=1))
        o = o[:, :LANES] / (o[:, LANES:] + jnp.where(lo, p_sink[0], p_sink[1]))
        for i in range(npair):
            o_ref[0, t * blk:(t + 1) * blk, i * LANES:(i + 1) * LANES] = o[i * blk:(i + 1) * blk].astype(o_ref.dtype)


def _swa_attention(q, kvar, vvar, kc, vc, sink_tab):
    b, s, nq = q.shape
    blk = SWA_BLOCK
    nb = s // blk
    n_steps = nb // SWA_GROUP
    gw = nq // SWA_KV_HEADS
    n_ctx = kc.shape[1]
    prev = lambda bi, g, j: (bi, jnp.maximum(SWA_GROUP * j - 1, 0), g)
    cur = lambda bi, g, j: (bi, j, g)
    nxt = lambda bi, g, j: (bi, jnp.minimum(SWA_GROUP * (j + 1), nb - 1), g)
    edge_spec = lambda f: pl.BlockSpec((1, blk, 2 * LANES), f)
    main_spec = pl.BlockSpec((1, SWA_GROUP * blk, 2 * LANES), cur)
    ctx_spec = pl.BlockSpec((1, n_ctx, 2 * LANES), lambda bi, g, j: (bi, 0, g))
    return pl.pallas_call(
        functools.partial(_swa_kernel, n_steps=n_steps),
        out_shape=jax.ShapeDtypeStruct((b, s, nq), BF16),
        grid=(b, SWA_KV_HEADS, n_steps),
        in_specs=[pl.BlockSpec((1, SWA_GROUP * blk, gw), cur),
                  edge_spec(prev), main_spec, edge_spec(nxt),
                  edge_spec(prev), main_spec, edge_spec(nxt),
                  ctx_spec, ctx_spec,
                  pl.BlockSpec((1, 2, sink_tab.shape[2], LANES), lambda bi, g, j: (g, 0, 0, 0))],
        out_specs=pl.BlockSpec((1, SWA_GROUP * blk, gw), cur),
        compiler_params=_cparams(("parallel", "parallel", "arbitrary")),
        name="swa_attention",
    )(q, kvar, kvar, kvar, vvar, vvar, vvar, kc, vc, sink_tab)


def _rope_tables(s, dim, pad_to, first_lane=0):
    n_rows = s // GRID_W
    half = dim // 2
    inv = ROPE_BASE ** (-jnp.arange(0, half, 2, dtype=F32) / half)
    ar = jnp.arange(n_rows, dtype=jnp.int32).astype(F32)[:, None] * inv
    ac = jnp.arange(GRID_W, dtype=jnp.int32).astype(F32)[:, None] * inv
    expand_r = lambda a: jnp.repeat(a, GRID_W, axis=0)
    expand_c = lambda a: jnp.tile(a, (n_rows, 1))
    cos = jnp.concatenate([expand_r(jnp.cos(ar))] * 2 + [expand_c(jnp.cos(ac))] * 2, axis=1)
    sin = jnp.concatenate([expand_r(-jnp.sin(ar)), expand_r(jnp.sin(ar)),
                           expand_c(-jnp.sin(ac)), expand_c(jnp.sin(ac))], axis=1)
    if pad_to > dim:
        before, after = first_lane, pad_to - dim - first_lane
        cos = jnp.concatenate([jnp.ones((s, before), F32), cos, jnp.ones((s, after), F32)], axis=1)
        sin = jnp.concatenate([jnp.zeros((s, before), F32), sin, jnp.zeros((s, after), F32)], axis=1)
    reps = LANES // cos.shape[1]
    return jnp.tile(cos, (1, reps)), jnp.tile(sin, (1, reps))


def _even_weights(w_in, w_q_up, w_uk, w_uv, w_out):
    d = w_in.shape[0]
    n_lat = MLA_Q_RANK + MLA_KV_RANK
    win = jnp.concatenate([w_in[:, :n_lat], jnp.zeros((d, ROPE_LANE), w_in.dtype), w_in[:, n_lat:MLA_IN],
                           jnp.zeros((d, LANES - ROPE_LANE - MLA_ROPE), w_in.dtype), w_in[:, MLA_IN:]], axis=1)
    wq3 = w_q_up.reshape(MLA_Q_RANK, MLA_HEADS, MLA_NOPE + MLA_ROPE)
    nope = wq3[:, :, :MLA_NOPE].reshape(MLA_Q_RANK, MLA_HEADS * MLA_NOPE)
    rope = jnp.pad(wq3[:, :, MLA_NOPE:], ((0, 0), (0, 0), (ROPE_LANE, LANES - ROPE_LANE - MLA_ROPE)))
    rope = rope.reshape(MLA_Q_RANK, MLA_HEADS * LANES)
    wq = jnp.concatenate([nope, rope], axis=1)
    eye = jnp.eye(MLA_HEADS, dtype=w_uk.dtype)
    wuk = jnp.einsum('hcn,hg->hngc', w_uk, eye).reshape(MLA_HEADS * MLA_NOPE, MLA_HEADS * MLA_KV_RANK)
    wuvt = jnp.swapaxes(w_uv, 1, 2)
    n_mla = MLA_HEADS * MLA_V
    return (win.astype(BF16), wq.astype(BF16), wuk.astype(BF16), wuvt.astype(BF16),
            w_out[:n_mla].astype(BF16), w_out[n_mla:].astype(BF16))


def _odd_weights(w_in):
    d = w_in.shape[0]
    nq = SWA_HEADS * HEAD_DIM
    z = jnp.zeros((d, HEAD_DIM), w_in.dtype)

    def variants(off):
        cols = []
        for g in range(SWA_KV_HEADS):
            w = w_in[:, off + g * HEAD_DIM:off + (g + 1) * HEAD_DIM]
            cols += [w, z, z, w]
        return jnp.concatenate(cols, axis=1)

    kcols = variants(nq)
    vcols = variants(nq + SWA_KV_HEADS * HEAD_DIM)
    return jnp.concatenate([w_in[:, :nq], kcols, vcols], axis=1).astype(BF16)


def _sink_table(sinks):
    g = SWA_HEADS // SWA_KV_HEADS
    t = (sinks.astype(F32) * LOG2_E).reshape(SWA_KV_HEADS, g // 2, 2).transpose(0, 2, 1)
    t = jnp.broadcast_to(t[:, :, :, None, None], (SWA_KV_HEADS, 2, g // 2, SWA_BLOCK, LANES))
    return t.reshape(SWA_KV_HEADS, 2, (g // 2) * SWA_BLOCK, LANES)


def kernel(x, c, ctx, c_ctx, mod_w, mod_b, even_w_in, mla_q_norm, mla_kv_norm, mla_w_q_up, mla_w_uk, mla_w_uv,
           na_rel_bias, even_w_out, odd_w_in, swa_sinks, odd_w_out, ffn_w_gate_up, ffn_w_down, final_norm):
    b, s, d = x.shape
    n_ctx = ctx.shape[1]
    n_rows = s // GRID_W
    assert d == D_MODEL and s % (NA_STEP_BLOCKS * NA_RB * GRID_W) == 0 and s % (SWA_GROUP * SWA_BLOCK) == 0
    assert mod_w.shape[0] == 2 and b <= 4

    ctx_grp = 4
    c8 = jnp.zeros((8, d), F32).at[:b].set(c.astype(F32)).at[ctx_grp].set(c_ctx.astype(F32))
    mods = _modulation(c8, mod_w.astype(F32), mod_b.astype(F32))
    mods0 = mods[0].reshape(8, 1, 6 * d)
    mods1 = mods[1].reshape(8, 1, 6 * d)
    lat_grp = lambda bi: bi
    ctx_g = lambda bi: ctx_grp

    tm = min(512, s)
    tf = 256
    wgu = ffn_w_gate_up.astype(BF16)
    wdn = ffn_w_down.astype(BF16)

    win, wq, wuk, wuvt, wo_mla, wo_na = _even_weights(even_w_in[0], mla_w_q_up[0], mla_w_uk[0], mla_w_uv[0],
                                                       even_w_out[0])
    qn = mla_q_norm[0].astype(F32).reshape(1, -1)
    kvn = mla_kv_norm[0].astype(F32).reshape(1, -1)
    cos_m, sin_m = _rope_tables(s, MLA_ROPE, LANES, ROPE_LANE)
    one_c = jnp.ones((n_ctx, LANES), F32)
    zero_c = jnp.zeros((n_ctx, LANES), F32)

    q_l, k_l, naq_l, nak_l, nav_l = _inproj0(x, mods0, lat_grp, win, qn, wq, wuk, kvn, cos_m, sin_m, tm)
    q_c, k_c, naq_c, nak_c, nav_c = _inproj0(ctx, mods0, ctx_g, win, qn, wq, wuk, kvn, one_c, zero_c, n_ctx)

    k_all = jnp.concatenate([k_c, k_l], axis=1)
    vt_all = jnp.swapaxes(k_all[:, :, :MLA_V_ROWS], 1, 2)
    o_mla_l = _mla_t_attention(q_l, k_all, vt_all, wuvt, min(1024, s))
    o_mla_c = _mla_t_attention(q_c, k_c, vt_all[:, :, :n_ctx], wuvt, n_ctx)

    bias = _na_bias_table(na_rel_bias[0], n_rows)
    o_na_l = _na_attention(naq_l, nak_l, nav_l, nak_c, nav_c, bias)
    o_na_c = _pair_ctx_attention(naq_c, nak_c, nav_c)

    h_lat = _ffn_block(x, mods0, lat_grp, [o_mla_l, o_na_l], [wo_mla, wo_na], wgu, wdn, 0, None, tm, tf)
    h_ctx = _ffn_block(ctx, mods0, ctx_g, [o_mla_c, o_na_c], [wo_mla, wo_na], wgu, wdn, 0, None, n_ctx, tf)

    win1 = _odd_weights(odd_w_in[0])
    cos_s, sin_s = _rope_tables(s, HEAD_DIM, HEAD_DIM)
    q1, k1, v1 = _inproj1(h_lat, mods1, lat_grp, win1, cos_s, sin_s, tm)
    _, k1c, v1c = _inproj1(h_ctx, mods1, ctx_g, win1, one_c, zero_c, n_ctx)
    o_swa = _swa_attention(q1, k1, v1, k1c, v1c, _sink_table(swa_sinks[0]))
    fn = final_norm.astype(F32).reshape(1, d)
    return _ffn_block(h_lat, mods1, lat_grp, [o_swa], [odd_w_out[0].astype(BF16)], wgu, wdn, 1, fn, tm, tf)
```

```python
import functools

import numpy as np
import jax
import jax.numpy as jnp
from jax import lax
from jax.experimental import pallas as pl
from jax.experimental.pallas import tpu as pltpu

F32 = jnp.float32
BF16 = jnp.bfloat16

D_MODEL = 1024
GRID_W = 64
HEAD_DIM = 64
ROPE_BASE = 10000.0
EPS = 1e-6
NEG_INF = -1e30
LOG2_E = 1.4426950408889634

MLA_HEADS = 8
MLA_NOPE = 64
MLA_ROPE = 32
MLA_V = 64
MLA_Q_RANK = 256
MLA_KV_RANK = 128
MLA_IN = MLA_Q_RANK + MLA_KV_RANK + MLA_ROPE

NA_HEADS = 8
NA_KR = 8
NA_KC = 16
NA_RB = 4
NA_SPAN = 12
NA_STEP_BLOCKS = 16

SWA_HEADS = 16
SWA_KV_HEADS = 2
SWA_WINDOW = 128
SWA_BLOCK = 128
SWA_GROUP = 16

LANES = 128
MLA_QK_PAD = 256
MLA_ITEM_ROWS = 256
MLA_KEY_CHUNK = 768
ONES_LANE = 0
ROPE_LANE = 8
MLA_V_ROWS = 144
VMEM_LIMIT = 56 * 1024 * 1024


def _cparams(sem):
    return pltpu.CompilerParams(dimension_semantics=sem, vmem_limit_bytes=VMEM_LIMIT)


def _dot(a, b):
    return jnp.dot(a, b, preferred_element_type=F32)


def _dot_nt(a, b):
    return lax.dot_general(a, b, (((1,), (1,)), ((), ())), preferred_element_type=F32)


def _rms(x):
    return x * lax.rsqrt(jnp.mean(x * x, axis=-1, keepdims=True) + EPS)


def _silu(x):
    return x * (1.0 / (1.0 + jnp.exp(-x)))


def _rope_tile(x, cos, sin, quarter, first_lane=0):
    lane = lax.broadcasted_iota(jnp.int32, x.shape, 1)
    first = ((lane + (2 * quarter - first_lane)) % (2 * quarter)) < quarter
    swapped = jnp.where(first, pltpu.roll(x, LANES - quarter, 1), pltpu.roll(x, quarter, 1))
    return x * cos + swapped * sin


def _zero_of(x):
    u = lax.bitcast_convert_type(x, jnp.uint32)
    u = lax.shift_right_logical(lax.shift_right_logical(u, jnp.uint32(16)), jnp.uint32(16))
    return lax.bitcast_convert_type(u, F32)


def _tiles(s):
    return [s[:, j * LANES:(j + 1) * LANES] for j in range(s.shape[1] // LANES)]


def _rowmax128(tiles):
    mx = tiles[0]
    for t in tiles[1:]:
        mx = jnp.maximum(mx, t)
    return jnp.broadcast_to(jnp.max(mx, axis=1, keepdims=True), mx.shape)


def _exp_tiles(tiles, m):
    return [jnp.exp2(t - m).astype(BF16) for t in tiles]


def _pair_indicator(n, hi):
    lane = lax.broadcasted_iota(jnp.int32, (n, LANES), 1)
    return jnp.where((lane >= HEAD_DIM) == hi, 1.0, 0.0).astype(BF16)


def _mod_kernel(c_ref, w_ref, b_ref, o_ref):
    a = _silu(c_ref[...])
    o_ref[0] = jnp.dot(a, w_ref[0], precision=lax.Precision.HIGHEST, preferred_element_type=F32) + b_ref[0]


def _modulation(c8, mod_w, mod_b):
    depth, d, n = mod_w.shape
    tn = 1536
    return pl.pallas_call(
        _mod_kernel,
        out_shape=jax.ShapeDtypeStruct((depth, 8, n), F32),
        grid=(depth, n // tn),
        in_specs=[pl.BlockSpec((8, d), lambda l, j: (0, 0)),
                  pl.BlockSpec((1, d, tn), lambda l, j: (l, 0, j)),
                  pl.BlockSpec((1, 1, tn), lambda l, j: (l, 0, j))],
        out_specs=pl.BlockSpec((1, 8, tn), lambda l, j: (l, 0, j)),
        compiler_params=_cparams(("parallel", "parallel")),
        name="modulation",
    )(c8, mod_w, mod_b.reshape(depth, 1, n))


def _inproj0_kernel(x_ref, mod_ref, win_ref, qn_ref, wq_ref, wuk_ref, kvn_ref, cos_ref, sin_ref,
                    qmla_ref, kmla_ref, naq_ref, nak_ref, nav_ref):
    d = D_MODEL
    x = x_ref[0]
    shift = mod_ref[0, :, 0:d]
    scale = mod_ref[0, :, d:2 * d]
    a = (_rms(x) * (1.0 + scale) + shift).astype(BF16)
    p = _dot(a, win_ref[...])
    cos = cos_ref[...]
    sin = sin_ref[...]
    mla_scale = (MLA_NOPE + MLA_ROPE) ** -0.5 * LOG2_E

    cq = (_rms(p[:, 0:MLA_Q_RANK]) * qn_ref[...]).astype(BF16)
    q = _dot(cq, wq_ref[...])
    n_nope = MLA_HEADS * MLA_NOPE
    q_lat = _dot(q[:, 0:n_nope].astype(BF16), wuk_ref[...])
    for h in range(MLA_HEADS):
        qr = _rope_tile(q[:, n_nope + h * LANES:n_nope + (h + 1) * LANES], cos, sin, MLA_ROPE // 4, ROPE_LANE)
        qmla_ref[0, h, :, 0:LANES] = (q_lat[:, h * LANES:(h + 1) * LANES] * mla_scale).astype(BF16)
        qmla_ref[0, h, :, LANES:2 * LANES] = (qr * mla_scale).astype(BF16)

    ckv = _rms(p[:, MLA_Q_RANK:MLA_Q_RANK + MLA_KV_RANK]) * kvn_ref[...]
    kr = _rope_tile(p[:, 3 * LANES:4 * LANES], cos, sin, MLA_ROPE // 4, ROPE_LANE)
    lane = lax.broadcasted_iota(jnp.int32, kr.shape, 1)
    kr = jnp.where(lane == ONES_LANE, 1.0, kr)
    kmla_ref[0, :, 0:LANES] = ckv.astype(BF16)
    kmla_ref[0, :, LANES:2 * LANES] = kr.astype(BF16)

    w = NA_HEADS * HEAD_DIM
    naq_ref[0] = (p[:, 4 * LANES:4 * LANES + w] * (HEAD_DIM ** -0.5 * LOG2_E)).astype(BF16)
    nak_ref[0] = p[:, 4 * LANES + w:4 * LANES + 2 * w].astype(BF16)
    nav_ref[0] = p[:, 4 * LANES + 2 * w:4 * LANES + 3 * w].astype(BF16)


def _inproj0(x, mods, grp, win, qn, wq, wuk, kvn, cos, sin, tm):
    b, s, d = x.shape
    nt = s // tm
    w = NA_HEADS * HEAD_DIM
    const = lambda bi, i: (0, 0)
    return pl.pallas_call(
        _inproj0_kernel,
        out_shape=(jax.ShapeDtypeStruct((b, MLA_HEADS, s, MLA_QK_PAD), BF16),
                   jax.ShapeDtypeStruct((b, s, MLA_QK_PAD), BF16),
                   jax.ShapeDtypeStruct((b, s, w), BF16),
                   jax.ShapeDtypeStruct((b, s, w), BF16),
                   jax.ShapeDtypeStruct((b, s, w), BF16)),
        grid=(b, nt),
        in_specs=[pl.BlockSpec((1, tm, d), lambda bi, i: (bi, i, 0)),
                  pl.BlockSpec((1, 1, mods.shape[2]), lambda bi, i: (grp(bi), 0, 0)),
                  pl.BlockSpec(win.shape, const),
                  pl.BlockSpec(qn.shape, const),
                  pl.BlockSpec(wq.shape, const),
                  pl.BlockSpec(wuk.shape, const),
                  pl.BlockSpec(kvn.shape, const),
                  pl.BlockSpec((tm, LANES), lambda bi, i: (i, 0)),
                  pl.BlockSpec((tm, LANES), lambda bi, i: (i, 0))],
        out_specs=(pl.BlockSpec((1, MLA_HEADS, tm, MLA_QK_PAD), lambda bi, i: (bi, 0, i, 0)),
                   pl.BlockSpec((1, tm, MLA_QK_PAD), lambda bi, i: (bi, i, 0)),
                   pl.BlockSpec((1, tm, w), lambda bi, i: (bi, i, 0)),
                   pl.BlockSpec((1, tm, w), lambda bi, i: (bi, i, 0)),
                   pl.BlockSpec((1, tm, w), lambda bi, i: (bi, i, 0))),
        compiler_params=_cparams(("parallel", "parallel")),
        name="inproj_even",
    )(x, mods, win, qn, wq, wuk, kvn, cos, sin)


def _mla_t_kernel(q_ref, k_ref, vt_ref, wuvt_ref, o_ref, s0_ref, s1_ref, p0_ref, p1_ref, m0_ref, m1_ref, oh_ref):
    n_heads, tq = q_ref.shape[1], q_ref.shape[2]
    nk, rc = s0_ref.shape
    n_items = (tq // rc) * n_heads
    s_refs, p_refs, m_refs = (s0_ref, s1_ref), (p0_ref, p1_ref), (m0_ref, m1_ref)
    tk = MLA_KEY_CHUNK if nk % MLA_KEY_CHUNK == 0 else 2 * LANES
    rg = 128

    def item(i):
        if isinstance(i, int):
            return i % n_heads, i // n_heads
        return lax.rem(i, n_heads), lax.div(i, n_heads)

    def round_(r, par):
        static = isinstance(r, int)
        do_qk = not static or r < n_items
        do_sm = not static or 1 <= r <= n_items
        do_pv = not static or 2 <= r <= n_items + 1
        if do_qk:
            h_q, c_q = item(r)
            row_q = c_q * rc if static else pl.multiple_of(c_q * rc, rc)
            q = q_ref[0, h_q, pl.ds(row_q, rc), :]
        if do_sm:
            m8 = m_refs[1 - par][...]
        if do_pv:
            acc = None
        for c in range(nk // tk):
            keys = slice(c * tk, (c + 1) * tk)
            if do_qk:
                st = _dot_nt(k_ref[0, keys, :], q)
                s_refs[par][keys, :] = st
                cm = jnp.max(st.reshape(tk // 8, 8, rc), axis=0)
                m_refs[par][...] = cm if c == 0 else jnp.maximum(m_refs[par][...], cm)
            if do_sm:
                for g in range(tk // rg):
                    m = m8 + _zero_of(st[g * rg:g * rg + 8, :]) if do_qk else m8
                    blk = slice(c * tk + g * rg, c * tk + (g + 1) * rg)
                    sb = s_refs[1 - par][blk, :].reshape(rg // 8, 8, rc)
                    p_refs[1 - par][blk, :] = jnp.exp2(sb - m[None]).reshape(rg, rc).astype(BF16)
            if do_pv:
                part = _dot(vt_ref[0, :, keys], p_refs[par][keys, :])
                acc = part if acc is None else acc + part
        if do_qk:
            mx = m_refs[par][...]
            m_refs[par][...] = jnp.broadcast_to(jnp.max(mx, axis=0, keepdims=True), mx.shape)
        if do_pv:
            h_v, c_v = item(r - 2)
            l = acc[MLA_KV_RANK + ONES_LANE:MLA_KV_RANK + ONES_LANE + 1, :]
            oh_ref[c_v, h_v] = (acc[0:MLA_KV_RANK, :] / l).astype(BF16)

    round_(0, 0)
    round_(1, 1)
    n_full = max(n_items - 2, 0)

    def pair(i, carry):
        r = 2 + 2 * i
        round_(r, 0)
        round_(r + 1, 1)
        return carry

    lax.fori_loop(0, n_full // 2, pair, 0)
    for r in range(2 + 2 * (n_full // 2), n_items):
        round_(r, r % 2)
    for r in range(max(n_items, 2), n_items + 2):
        round_(r, r % 2)

    for ci in range(tq // rc):
        out_t = jnp.concatenate([_dot(wuvt_ref[h], oh_ref[ci, h]) for h in range(n_heads)], axis=0)
        o_ref[0, ci * rc:(ci + 1) * rc, :] = out_t.T.astype(o_ref.dtype)


def _mla_t_attention(q, k, vt, wuvt, tq):
    b, h, sq, _ = q.shape
    nk = k.shape[1]
    rc = min(MLA_ITEM_ROWS, tq)
    assert nk % (2 * LANES) == 0 and tq % rc == 0 and sq % tq == 0
    wo = wuvt.shape[0] * wuvt.shape[1]
    one_buffer = pl.Buffered(1)
    return pl.pallas_call(
        _mla_t_kernel,
        out_shape=jax.ShapeDtypeStruct((b, sq, wo), BF16),
        grid=(b, sq // tq),
        in_specs=[pl.BlockSpec((1, h, tq, MLA_QK_PAD), lambda bi, i: (bi, 0, i, 0)),
                  pl.BlockSpec((1, nk, MLA_QK_PAD), lambda bi, i: (bi, 0, 0), pipeline_mode=one_buffer),
                  pl.BlockSpec((1, MLA_V_ROWS, nk), lambda bi, i: (bi, 0, 0), pipeline_mode=one_buffer),
                  _resident(wuvt.shape)],
        out_specs=pl.BlockSpec((1, tq, wo), lambda bi, i: (bi, i, 0)),
        scratch_shapes=[pltpu.VMEM((nk, rc), F32), pltpu.VMEM((nk, rc), F32),
                        pltpu.VMEM((nk, rc), BF16), pltpu.VMEM((nk, rc), BF16),
                        pltpu.VMEM((8, rc), F32), pltpu.VMEM((8, rc), F32),
                        pltpu.VMEM((tq // rc, h, MLA_KV_RANK, rc), BF16)],
        compiler_params=_cparams(("parallel", "arbitrary")),
        name="mla_attention",
    )(q, k, vt, wuvt)


def _split_heads(t):
    lane = lax.broadcasted_iota(jnp.int32, t.shape, 1)
    lo = lane < HEAD_DIM
    zero = jnp.zeros_like(t)
    return jnp.where(lo, t, zero), jnp.where(lo, zero, t)


def _stack_pair(lo_hi_ctx, lo_hi_lat):
    return jnp.concatenate([lo_hi_ctx[0], lo_hi_ctx[1], lo_hi_lat[0], lo_hi_lat[1]], axis=0)


def _na_kernel(q_ref, k_ref, v_ref, kc_ref, vc_ref, bias_ref, o_ref, *, n_rows):
    step = pl.program_id(2)
    nrb = n_rows // NA_RB
    n_keys = NA_SPAN * GRID_W
    n_ctx = kc_ref.shape[1]
    tq = NA_RB * GRID_W
    kcs = _split_heads(kc_ref[0])
    vcs = _split_heads(vc_ref[0])
    ind = _stack_pair((_pair_indicator(n_ctx, False), _pair_indicator(n_ctx, True)),
                      (_pair_indicator(n_keys, False), _pair_indicator(n_keys, True)))
    for i in range(NA_STEP_BLOCKS):
        rb = step * NA_STEP_BLOCKS + i
        case = jnp.minimum(rb, 1) + jnp.maximum(rb - (nrb - 2), 0)
        base = jnp.clip(rb * NA_RB - NA_KR // 2, 0, n_rows - NA_SPAN)
        start = pl.multiple_of(base * GRID_W, 256)
        q = q_ref[0, i * tq:(i + 1) * tq, :]
        k_st = _stack_pair(kcs, _split_heads(k_ref[0, pl.ds(start, n_keys), :]))
        v_st = _stack_pair(vcs, _split_heads(v_ref[0, pl.ds(start, n_keys), :]))
        s = _dot_nt(q, k_st)
        p_ctx, p_lat = [], []
        for par in range(2):
            t_c = _tiles(s[:, par * n_ctx:(par + 1) * n_ctx])
            off = 2 * n_ctx + par * n_keys
            t_l = _tiles(s[:, off:off + n_keys] + bias_ref[case, par])
            m = _rowmax128(t_c + t_l)
            p_ctx += _exp_tiles(t_c, m)
            p_lat += _exp_tiles(t_l, m)
        o = _dot(jnp.concatenate(p_ctx + p_lat, axis=1), jnp.concatenate([v_st, ind], axis=1))
        o_ref[0, i * tq:(i + 1) * tq, :] = (o[:, :LANES] / o[:, LANES:]).astype(o_ref.dtype)


def _na_attention(q, k, v, kc, vc, bias):
    b, s, w = q.shape
    n_rows = s // GRID_W
    nrb = n_rows // NA_RB
    tq = NA_STEP_BLOCKS * NA_RB * GRID_W
    n_ctx = kc.shape[1]
    return pl.pallas_call(
        functools.partial(_na_kernel, n_rows=n_rows),
        out_shape=jax.ShapeDtypeStruct((b, s, w), BF16),
        grid=(w // LANES, b, nrb // NA_STEP_BLOCKS),
        in_specs=[pl.BlockSpec((1, tq, LANES), lambda j, bi, r: (bi, r, j)),
                  pl.BlockSpec((1, s, LANES), lambda j, bi, r: (bi, 0, j)),
                  pl.BlockSpec((1, s, LANES), lambda j, bi, r: (bi, 0, j)),
                  pl.BlockSpec((1, n_ctx, LANES), lambda j, bi, r: (bi, 0, j)),
                  pl.BlockSpec((1, n_ctx, LANES), lambda j, bi, r: (bi, 0, j)),
                  pl.BlockSpec((3, 2, NA_RB * GRID_W, NA_SPAN * GRID_W), lambda j, bi, r: (0, j, 0, 0))],
        out_specs=pl.BlockSpec((1, tq, LANES), lambda j, bi, r: (bi, r, j)),
        compiler_params=_cparams(("parallel", "parallel", "arbitrary")),
        name="na_attention",
    )(q, k, v, kc, vc, bias)


def _pair_ctx_kernel(q_ref, kc_ref, vc_ref, o_ref):
    q = q_ref[0]
    n_ctx = kc_ref.shape[1]
    kcs = _split_heads(kc_ref[0])
    vcs = _split_heads(vc_ref[0])
    s = _dot_nt(q, jnp.concatenate(kcs, axis=0))
    p = []
    for par in range(2):
        t_c = _tiles(s[:, par * n_ctx:(par + 1) * n_ctx])
        p += _exp_tiles(t_c, _rowmax128(t_c))
    ind = jnp.concatenate([_pair_indicator(n_ctx, False), _pair_indicator(n_ctx, True)], axis=0)
    o = _dot(jnp.concatenate(p, axis=1), jnp.concatenate([jnp.concatenate(vcs, axis=0), ind], axis=1))
    o_ref[0] = (o[:, :LANES] / o[:, LANES:]).astype(o_ref.dtype)


def _pair_ctx_attention(q, kc, vc):
    b, n, w = q.shape
    spec = pl.BlockSpec((1, n, LANES), lambda bi, j: (bi, 0, j))
    return pl.pallas_call(
        _pair_ctx_kernel,
        out_shape=jax.ShapeDtypeStruct((b, n, w), BF16),
        grid=(b, w // LANES),
        in_specs=[spec, spec, spec],
        out_specs=spec,
        compiler_params=_cparams(("parallel", "parallel")),
        name="na_ctx_attention",
    )(q, kc, vc)


def _na_bias_table(rel_bias, n_rows):
    qc = np.arange(GRID_W)
    cstart = np.clip(qc - NA_KC // 2, 0, GRID_W - NA_KC)
    kc = np.arange(GRID_W)
    col_ok = (kc[None, :] >= cstart[:, None]) & (kc[None, :] < cstart[:, None] + NA_KC)
    n_dr, n_dc = 2 * NA_KR - 1, 2 * NA_KC - 1
    nh = rel_bias.shape[0]
    dc = kc[None, :] - qc[:, None] + NA_KC - 1
    sel_col = ((dc[None] == np.arange(n_dc)[:, None, None]) & col_ok[None]).astype(np.float32)
    toe = jnp.einsum('hdu,uqk->hdqk', rel_bias.astype(F32), jnp.asarray(sel_col), precision=lax.Precision.HIGHEST)
    toe = jnp.where(jnp.asarray(col_ok)[None, None], toe * LOG2_E, NEG_INF)
    toe = jnp.concatenate([toe, jnp.full((nh, 1, GRID_W, GRID_W), NEG_INF, F32)], axis=1)
    toe = jnp.concatenate([toe, toe], axis=-1)
    idx = np.full((3, NA_RB, NA_SPAN), n_dr, np.int32)
    nrb = n_rows // NA_RB
    for case, rb in enumerate((0, min(1, nrb - 1), nrb - 1)):
        r0 = rb * NA_RB
        base = int(np.clip(r0 - NA_KR // 2, 0, n_rows - NA_SPAN))
        for a in range(NA_RB):
            r = r0 + a
            rs = int(np.clip(r - NA_KR // 2, 0, n_rows - NA_KR))
            for t in range(NA_SPAN):
                kr = base + t
                if rs <= kr < rs + NA_KR:
                    idx[case, a, t] = kr - r + NA_KR - 1

    def assemble(toe_ref, o_ref):
        lo = lax.broadcasted_iota(jnp.int32, (GRID_W, LANES), 1) < GRID_W
        for case in range(3):
            for a in range(NA_RB):
                for tp in range(NA_SPAN // 2):
                    d0, d1 = int(idx[case, a, 2 * tp]), int(idx[case, a, 2 * tp + 1])
                    tile = toe_ref[0, d0] if d0 == d1 else jnp.where(lo, toe_ref[0, d0], toe_ref[0, d1])
                    o_ref[case, 0, a * GRID_W:(a + 1) * GRID_W, tp * LANES:(tp + 1) * LANES] = tile

    tq, tkeys = NA_RB * GRID_W, NA_SPAN * GRID_W
    return pl.pallas_call(
        assemble,
        out_shape=jax.ShapeDtypeStruct((3, nh, tq, tkeys), F32),
        grid=(nh,),
        in_specs=[pl.BlockSpec((1, n_dr + 1, GRID_W, LANES), lambda h: (h, 0, 0, 0))],
        out_specs=pl.BlockSpec((3, 1, tq, tkeys), lambda h: (0, h, 0, 0)),
        compiler_params=_cparams(("parallel",)),
        name="na_bias_table",
    )(toe)


def _ffn_kernel(*refs, n_attn, tf, final):
    h_ref, mod_ref = refs[0], refs[1]
    attn_refs = refs[2:2 + n_attn]
    wout_refs = refs[2 + n_attn:2 + 2 * n_attn]
    wgu_ref, wd_ref = refs[2 + 2 * n_attn:4 + 2 * n_attn]
    pos = 4 + 2 * n_attn
    fn_ref = refs[pos] if final else None
    pos += 1 if final else 0
    o_ref, a2_ref, f_ref = refs[pos:pos + 3]
    d = D_MODEL
    dff = wd_ref.shape[1]

    proj = _dot(attn_refs[0][0], wout_refs[0][...])
    for t in range(1, n_attn):
        proj = proj + _dot(attn_refs[t][0], wout_refs[t][...])
    h1 = h_ref[0] + mod_ref[0, :, 2 * d:3 * d] * proj
    o_ref[0] = h1
    a2_ref[...] = (_rms(h1) * (1.0 + mod_ref[0, :, 4 * d:5 * d]) + mod_ref[0, :, 3 * d:4 * d]).astype(BF16)

    for j in range(dff // tf):
        a2 = a2_ref[...]
        g = _dot(a2, wgu_ref[0, :, j * tf:(j + 1) * tf])
        u = _dot(a2, wgu_ref[0, :, dff + j * tf:dff + (j + 1) * tf])
        f_ref[:, j * tf:(j + 1) * tf] = (_silu(g) * u).astype(BF16)

    out = o_ref[0] + mod_ref[0, :, 5 * d:6 * d] * _dot(f_ref[...], wd_ref[0])
    if final:
        out = _rms(out) * fn_ref[...]
    o_ref[0] = out


def _resident(shape):
    return pl.BlockSpec(shape, lambda *_: (0,) * len(shape), pipeline_mode=pl.Buffered(1))


def _ffn_block(h, mods, grp, attns, wouts, wgu, wd, layer, final_norm, tm, tf):
    b, s, d = h.shape
    dff = wd.shape[1]
    n_attn = len(attns)
    final = final_norm is not None
    in_specs = [pl.BlockSpec((1, tm, d), lambda bi, i: (bi, i, 0)),
                pl.BlockSpec((1, 1, mods.shape[2]), lambda bi, i: (grp(bi), 0, 0))]
    in_specs += [pl.BlockSpec((1, tm, a.shape[2]), lambda bi, i: (bi, i, 0)) for a in attns]
    in_specs += [_resident(w.shape) for w in wouts]
    in_specs += [pl.BlockSpec((1,) + w.shape[1:], lambda *_: (layer, 0, 0), pipeline_mode=pl.Buffered(1))
                 for w in (wgu, wd)]
    args = [h, mods, *attns, *wouts, wgu, wd]
    if final:
        in_specs.append(_resident((1, d)))
        args.append(final_norm)
    return pl.pallas_call(
        functools.partial(_ffn_kernel, n_attn=n_attn, tf=tf, final=final),
        out_shape=jax.ShapeDtypeStruct((b, s, d), F32),
        grid=(b, s // tm),
        in_specs=in_specs,
        out_specs=pl.BlockSpec((1, tm, d), lambda bi, i: (bi, i, 0)),
        scratch_shapes=[pltpu.VMEM((tm, d), BF16), pltpu.VMEM((tm, dff), BF16)],
        compiler_params=_cparams(("parallel", "parallel")),
        name="outproj_ffn",
    )(*args)


def _inproj1_kernel(x_ref, mod_ref, win_ref, cos_ref, sin_ref, q_ref, k_ref, v_ref):
    d = D_MODEL
    x = x_ref[0]
    a = (_rms(x) * (1.0 + mod_ref[0, :, d:2 * d]) + mod_ref[0, :, 0:d]).astype(BF16)
    p = _dot(a, win_ref[...])
    cos = cos_ref[...]
    sin = sin_ref[...]
    nq = SWA_HEADS * HEAD_DIM
    quarter = HEAD_DIM // 4
    for t in range(nq // LANES):
        q = _rope_tile(p[:, t * LANES:(t + 1) * LANES], cos, sin, quarter)
        q_ref[0, :, t * LANES:(t + 1) * LANES] = (q * (HEAD_DIM ** -0.5 * LOG2_E)).astype(BF16)
    k = _rope_tile(p[:, nq:nq + LANES], cos, sin, quarter)
    v = p[:, nq + LANES:nq + 2 * LANES]
    lo = lax.broadcasted_iota(jnp.int32, k.shape, 1) < HEAD_DIM
    for ref, t in ((k_ref, k), (v_ref, v)):
        sw = pltpu.roll(t, HEAD_DIM, 1)
        tiles = (jnp.where(lo, t, 0.0), jnp.where(lo, 0.0, sw), jnp.where(lo, sw, 0.0), jnp.where(lo, 0.0, t))
        for i, tile in enumerate(tiles):
            ref[0, :, i * LANES:(i + 1) * LANES] = tile.astype(BF16)


def _inproj1(x, mods, grp, win, cos, sin, tm):
    b, s, d = x.shape
    nq = SWA_HEADS * HEAD_DIM
    nkv = 2 * SWA_KV_HEADS * LANES
    return pl.pallas_call(
        _inproj1_kernel,
        out_shape=(jax.ShapeDtypeStruct((b, s, nq), BF16),
                   jax.ShapeDtypeStruct((b, s, nkv), BF16),
                   jax.ShapeDtypeStruct((b, s, nkv), BF16)),
        grid=(b, s // tm),
        in_specs=[pl.BlockSpec((1, tm, d), lambda bi, i: (bi, i, 0)),
                  pl.BlockSpec((1, 1, mods.shape[2]), lambda bi, i: (grp(bi), 0, 0)),
                  pl.BlockSpec(win.shape, lambda bi, i: (0, 0)),
                  pl.BlockSpec((tm, LANES), lambda bi, i: (i, 0)),
                  pl.BlockSpec((tm, LANES), lambda bi, i: (i, 0))],
        out_specs=(pl.BlockSpec((1, tm, nq), lambda bi, i: (bi, i, 0)),
                   pl.BlockSpec((1, tm, nkv), lambda bi, i: (bi, i, 0)),
                   pl.BlockSpec((1, tm, nkv), lambda bi, i: (bi, i, 0))),
        compiler_params=_cparams(("parallel", "parallel")),
        name="inproj_odd",
    )(x, mods, win, cos, sin)


def _swa_kernel(q_ref, km_ref, k0_ref, kp_ref, vm_ref, v0_ref, vp_ref, kc_ref, vc_ref, sink_ref, o_ref, *, n_steps):
    step = pl.program_id(2)
    blk = SWA_BLOCK
    npair = q_ref.shape[2] // LANES
    kall = jnp.concatenate([km_ref[0], k0_ref[0], kp_ref[0]], axis=0)
    vall = jnp.concatenate([vm_ref[0], v0_ref[0], vp_ref[0]], axis=0)
    kc = kc_ref[0]
    vc = vc_ref[0]
    n_ctx = kc.shape[0]
    kc_st = jnp.concatenate([kc[:, :LANES], kc[:, LANES:]], axis=0)
    vc_st = jnp.concatenate([vc[:, :LANES], vc[:, LANES:]], axis=0)
    ind = _stack_pair((_pair_indicator(n_ctx, False), _pair_indicator(n_ctx, True)),
                      (_pair_indicator(3 * blk, False), _pair_indicator(3 * blk, True)))
    r = lax.broadcasted_iota(jnp.int32, (blk, 3 * blk), 0)
    c = lax.broadcasted_iota(jnp.int32, (blk, 3 * blk), 1)
    cc = c % blk
    lane = lax.broadcasted_iota(jnp.int32, (npair * blk, LANES), 1)
    lo = lane < HEAD_DIM

    def band(left_off, right_off):
        ok = ((c >= blk) & (c < 2 * blk)) | ((c < blk) & (cc >= r + left_off)) | ((c >= 2 * blk) & (cc <= r - right_off))
        neg = jnp.where(ok, 0.0, NEG_INF).astype(F32)
        return jnp.concatenate([neg] * npair, axis=0)

    for t in range(SWA_GROUP):
        left_off = jnp.where(step >= 1, 0, blk) if t == 0 else 0
        right_off = jnp.where(step <= n_steps - 2, 0, blk) if t == SWA_GROUP - 1 else 0
        neg = band(left_off, right_off)
        qb = q_ref[0, t * blk:(t + 1) * blk, :]
        qst = jnp.concatenate([qb[:, i * LANES:(i + 1) * LANES] for i in range(npair)], axis=0)
        kw = kall[t * blk:(t + 3) * blk]
        vw = vall[t * blk:(t + 3) * blk]
        k_st = jnp.concatenate([kc_st, kw[:, :LANES], kw[:, LANES:]], axis=0)
        v_st = jnp.concatenate([vc_st, vw[:, :LANES], vw[:, LANES:]], axis=0)
        s = _dot_nt(qst, k_st)
        p_ctx, p_lat, p_sink = [], [], []
        for par in range(2):
            sink = sink_ref[0, par]
            t_c = _tiles(s[:, par * n_ctx:(par + 1) * n_ctx])
            off = 2 * n_ctx + par * 3 * blk
            t_l = _tiles(s[:, off:off + 3 * blk] + neg)
            m = jnp.maximum(_rowmax128(t_c + t_l), sink)
            p_ctx += _exp_tiles(t_c, m)
            p_lat += _exp_tiles(t_l, m)
            p_sink.append(jnp.exp2(sink - m))
        o = _dot(jnp.concatenate(p_ctx + p_lat, axis=1), jnp.concatenate([v_st, ind], axis=1))
        o = o[:, :LANES] / (o[:, LANES:] + jnp.where(lo, p_sink[0], p_sink[1]))
        for i in range(npair):
            o_ref[0, t * blk:(t + 1) * blk, i * LANES:(i + 1) * LANES] = o[i * blk:(i + 1) * blk].astype(o_ref.dtype)


def _swa_attention(q, kvar, vvar, kc, vc, sink_tab):
    b, s, nq = q.shape
    blk = SWA_BLOCK
    nb = s // blk
    n_steps = nb // SWA_GROUP
    gw = nq // SWA_KV_HEADS
    n_ctx = kc.shape[1]
    prev = lambda bi, g, j: (bi, jnp.maximum(SWA_GROUP * j - 1, 0), g)
    cur = lambda bi, g, j: (bi, j, g)
    nxt = lambda bi, g, j: (bi, jnp.minimum(SWA_GROUP * (j + 1), nb - 1), g)
    edge_spec = lambda f: pl.BlockSpec((1, blk, 2 * LANES), f)
    main_spec = pl.BlockSpec((1, SWA_GROUP * blk, 2 * LANES), cur)
    ctx_spec = pl.BlockSpec((1, n_ctx, 2 * LANES), lambda bi, g, j: (bi, 0, g))
    return pl.pallas_call(
        functools.partial(_swa_kernel, n_steps=n_steps),
        out_shape=jax.ShapeDtypeStruct((b, s, nq), BF16),
        grid=(b, SWA_KV_HEADS, n_steps),
        in_specs=[pl.BlockSpec((1, SWA_GROUP * blk, gw), cur),
                  edge_spec(prev), main_spec, edge_spec(nxt),
                  edge_spec(prev), main_spec, edge_spec(nxt),
                  ctx_spec, ctx_spec,
                  pl.BlockSpec((1, 2, sink_tab.shape[2], LANES), lambda bi, g, j: (g, 0, 0, 0))],
        out_specs=pl.BlockSpec((1, SWA_GROUP * blk, gw), cur),
        compiler_params=_cparams(("parallel", "parallel", "arbitrary")),
        name="swa_attention",
    )(q, kvar, kvar, kvar, vvar, vvar, vvar, kc, vc, sink_tab)


def _rope_tables(s, dim, pad_to, first_lane=0):
    n_rows = s // GRID_W
    half = dim // 2
    inv = ROPE_BASE ** (-jnp.arange(0, half, 2, dtype=F32) / half)
    ar = jnp.arange(n_rows, dtype=jnp.int32).astype(F32)[:, None] * inv
    ac = jnp.arange(GRID_W, dtype=jnp.int32).astype(F32)[:, None] * inv
    expand_r = lambda a: jnp.repeat(a, GRID_W, axis=0)
    expand_c = lambda a: jnp.tile(a, (n_rows, 1))
    cos = jnp.concatenate([expand_r(jnp.cos(ar))] * 2 + [expand_c(jnp.cos(ac))] * 2, axis=1)
    sin = jnp.concatenate([expand_r(-jnp.sin(ar)), expand_r(jnp.sin(ar)),
                           expand_c(-jnp.sin(ac)), expand_c(jnp.sin(ac))], axis=1)
    if pad_to > dim:
        before, after = first_lane, pad_to - dim - first_lane
        cos = jnp.concatenate([jnp.ones((s, before), F32), cos, jnp.ones((s, after), F32)], axis=1)
        sin = jnp.concatenate([jnp.zeros((s, before), F32), sin, jnp.zeros((s, after), F32)], axis=1)
    reps = LANES // cos.shape[1]
    return jnp.tile(cos, (1, reps)), jnp.tile(sin, (1, reps))


def _even_weights(w_in, w_q_up, w_uk, w_uv, w_out):
    d = w_in.shape[0]
    n_lat = MLA_Q_RANK + MLA_KV_RANK
    win = jnp.concatenate([w_in[:, :n_lat], jnp.zeros((d, ROPE_LANE), w_in.dtype), w_in[:, n_lat:MLA_IN],
                           jnp.zeros((d, LANES - ROPE_LANE - MLA_ROPE), w_in.dtype), w_in[:, MLA_IN:]], axis=1)
    wq3 = w_q_up.reshape(MLA_Q_RANK, MLA_HEADS, MLA_NOPE + MLA_ROPE)
    nope = wq3[:, :, :MLA_NOPE].reshape(MLA_Q_RANK, MLA_HEADS * MLA_NOPE)
    rope = jnp.pad(wq3[:, :, MLA_NOPE:], ((0, 0), (0, 0), (ROPE_LANE, LANES - ROPE_LANE - MLA_ROPE)))
    rope = rope.reshape(MLA_Q_RANK, MLA_HEADS * LANES)
    wq = jnp.concatenate([nope, rope], axis=1)
    eye = jnp.eye(MLA_HEADS, dtype=w_uk.dtype)
    wuk = jnp.einsum('hcn,hg->hngc', w_uk, eye).reshape(MLA_HEADS * MLA_NOPE, MLA_HEADS * MLA_KV_RANK)
    wuvt = jnp.swapaxes(w_uv, 1, 2)
    n_mla = MLA_HEADS * MLA_V
    return (win.astype(BF16), wq.astype(BF16), wuk.astype(BF16), wuvt.astype(BF16),
            w_out[:n_mla].astype(BF16), w_out[n_mla:].astype(BF16))


def _sink_table(sinks):
    g = SWA_HEADS // SWA_KV_HEADS
    t = (sinks.astype(F32) * LOG2_E).reshape(SWA_KV_HEADS, g // 2, 2).transpose(0, 2, 1)
    t = jnp.broadcast_to(t[:, :, :, None, None], (SWA_KV_HEADS, 2, g // 2, SWA_BLOCK, LANES))
    return t.reshape(SWA_KV_HEADS, 2, (g // 2) * SWA_BLOCK, LANES)


def kernel(x, c, ctx, c_ctx, mod_w, mod_b, even_w_in, mla_q_norm, mla_kv_norm, mla_w_q_up, mla_w_uk, mla_w_uv,
           na_rel_bias, even_w_out, odd_w_in, swa_sinks, odd_w_out, ffn_w_gate_up, ffn_w_down, final_norm):
    b, s, d = x.shape
    n_ctx = ctx.shape[1]
    n_rows = s // GRID_W
    assert d == D_MODEL and s % (NA_STEP_BLOCKS * NA_RB * GRID_W) == 0 and s % (SWA_GROUP * SWA_BLOCK) == 0
    assert mod_w.shape[0] == 2 and b <= 4

    ctx_grp = 4
    c8 = jnp.zeros((8, d), F32).at[:b].set(c.astype(F32)).at[ctx_grp].set(c_ctx.astype(F32))
    mods = _modulation(c8, mod_w.astype(F32), mod_b.astype(F32))
    mods0 = mods[0].reshape(8, 1, 6 * d)
    mods1 = mods[1].reshape(8, 1, 6 * d)
    lat_grp = lambda bi: bi
    ctx_g = lambda bi: ctx_grp

    tm = min(512, s)
    tf = 256
    wgu = ffn_w_gate_up.astype(BF16)
    wdn = ffn_w_down.astype(BF16)

    win, wq, wuk, wuvt, wo_mla, wo_na = _even_weights(even_w_in[0], mla_w_q_up[0], mla_w_uk[0], mla_w_uv[0],
                                                       even_w_out[0])
    qn = mla_q_norm[0].astype(F32).reshape(1, -1)
    kvn = mla_kv_norm[0].astype(F32).reshape(1, -1)
    cos_m, sin_m = _rope_tables(s, MLA_ROPE, LANES, ROPE_LANE)
    one_c = jnp.ones((n_ctx, LANES), F32)
    zero_c = jnp.zeros((n_ctx, LANES), F32)

    q_l, k_l, naq_l, nak_l, nav_l = _inproj0(x, mods0, lat_grp, win, qn, wq, wuk, kvn, cos_m, sin_m, tm)
    q_c, k_c, naq_c, nak_c, nav_c = _inproj0(ctx, mods0, ctx_g, win, qn, wq, wuk, kvn, one_c, zero_c, n_ctx)

    k_all = jnp.concatenate([k_c, k_l], axis=1)
    vt_all = jnp.swapaxes(k_all[:, :, :MLA_V_ROWS], 1, 2)
    o_mla_l = _mla_t_attention(q_l, k_all, vt_all, wuvt, min(1024, s))
    o_mla_c = _mla_t_attention(q_c, k_c, vt_all[:, :, :n_ctx], wuvt, n_ctx)

    bias = _na_bias_table(na_rel_bias[0], n_rows)
    o_na_l = _na_attention(naq_l, nak_l, nav_l, nak_c, nav_c, bias)
    o_na_c = _pair_ctx_attention(naq_c, nak_c, nav_c)

    h_lat = _ffn_block(x, mods0, lat_grp, [o_mla_l, o_na_l], [wo_mla, wo_na], wgu, wdn, 0, None, tm, tf)
    h_ctx = _ffn_block(ctx, mods0, ctx_g, [o_mla_c, o_na_c], [wo_mla, wo_na], wgu, wdn, 0, None, n_ctx, tf)

    assert SWA_KV_HEADS * HEAD_DIM == LANES
    win1 = odd_w_in[0].astype(BF16)
    cos_s, sin_s = _rope_tables(s, HEAD_DIM, HEAD_DIM)
    q1, k1, v1 = _inproj1(h_lat, mods1, lat_grp, win1, cos_s, sin_s, tm)
    _, k1c, v1c = _inproj1(h_ctx, mods1, ctx_g, win1, one_c, zero_c, n_ctx)
    o_swa = _swa_attention(q1, k1, v1, k1c, v1c, _sink_table(swa_sinks[0]))
    fn = final_norm.astype(F32).reshape(1, d)
    return _ffn_block(h_lat, mods1, lat_grp, [o_swa], [odd_w_out[0].astype(BF16)], wgu, wdn, 1, fn, tm, tf)
```

```python
import functools

import numpy as np
import jax
import jax.numpy as jnp
from jax import lax
from jax.experimental import pallas as pl
from jax.experimental.pallas import tpu as pltpu

F32 = jnp.float32
BF16 = jnp.bfloat16

D_MODEL = 1024
GRID_W = 64
HEAD_DIM = 64
ROPE_BASE = 10000.0
EPS = 1e-6
NEG_INF = -1e30
LOG2_E = 1.4426950408889634

MLA_HEADS = 8
MLA_NOPE = 64
MLA_ROPE = 32
MLA_V = 64
MLA_Q_RANK = 256
MLA_KV_RANK = 128
MLA_IN = MLA_Q_RANK + MLA_KV_RANK + MLA_ROPE

NA_HEADS = 8
NA_KR = 8
NA_KC = 16
NA_RB = 4
NA_SPAN = 12
NA_STEP_BLOCKS = 16

SWA_HEADS = 16
SWA_KV_HEADS = 2
SWA_WINDOW = 128
SWA_BLOCK = 128
SWA_GROUP = 16

LANES = 128
MLA_QK_PAD = 256
MLA_ITEM_ROWS = 256
MLA_KEY_CHUNK = 768
ONES_LANE = 0
ROPE_LANE = 8
MLA_V_ROWS = 144
VMEM_LIMIT = 56 * 1024 * 1024


def _cparams(sem):
    return pltpu.CompilerParams(dimension_semantics=sem, vmem_limit_bytes=VMEM_LIMIT)


def _dot(a, b):
    return jnp.dot(a, b, preferred_element_type=F32)


def _dot_nt(a, b):
    return lax.dot_general(a, b, (((1,), (1,)), ((), ())), preferred_element_type=F32)


def _rms(x):
    return x * lax.rsqrt(jnp.mean(x * x, axis=-1, keepdims=True) + EPS)


def _silu(x):
    return x * (1.0 / (1.0 + jnp.exp(-x)))


def _rope_tile(x, cos, sin, quarter, first_lane=0):
    lane = lax.broadcasted_iota(jnp.int32, x.shape, 1)
    first = ((lane + (2 * quarter - first_lane)) % (2 * quarter)) < quarter
    swapped = jnp.where(first, pltpu.roll(x, LANES - quarter, 1), pltpu.roll(x, quarter, 1))
    return x * cos + swapped * sin


def _zero_of(x):
    u = lax.bitcast_convert_type(x, jnp.uint32)
    u = lax.shift_right_logical(lax.shift_right_logical(u, jnp.uint32(16)), jnp.uint32(16))
    return lax.bitcast_convert_type(u, F32)


def _tiles(s):
    return [s[:, j * LANES:(j + 1) * LANES] for j in range(s.shape[1] // LANES)]


def _rowmax128(tiles):
    mx = tiles[0]
    for t in tiles[1:]:
        mx = jnp.maximum(mx, t)
    return jnp.broadcast_to(jnp.max(mx, axis=1, keepdims=True), mx.shape)


def _exp_tiles(tiles, m):
    return [jnp.exp2(t - m).astype(BF16) for t in tiles]


def _pair_indicator(n, hi):
    lane = lax.broadcasted_iota(jnp.int32, (n, LANES), 1)
    return jnp.where((lane >= HEAD_DIM) == hi, 1.0, 0.0).astype(BF16)


def _mod_kernel(c_ref, w_ref, b_ref, o_ref):
    a = _silu(c_ref[...])
    o_ref[0] = jnp.dot(a, w_ref[0], precision=lax.Precision.HIGHEST, preferred_element_type=F32) + b_ref[0]


def _modulation(c8, mod_w, mod_b):
    depth, d, n = mod_w.shape
    tn = 1536
    return pl.pallas_call(
        _mod_kernel,
        out_shape=jax.ShapeDtypeStruct((depth, 8, n), F32),
        grid=(depth, n // tn),
        in_specs=[pl.BlockSpec((8, d), lambda l, j: (0, 0)),
                  pl.BlockSpec((1, d, tn), lambda l, j: (l, 0, j)),
                  pl.BlockSpec((1, 1, tn), lambda l, j: (l, 0, j))],
        out_specs=pl.BlockSpec((1, 8, tn), lambda l, j: (l, 0, j)),
        compiler_params=_cparams(("parallel", "parallel")),
        name="modulation",
    )(c8, mod_w, mod_b.reshape(depth, 1, n))


def _inproj0_kernel(x_ref, mod_ref, win_ref, qn_ref, wq_ref, wuk_ref, kvn_ref, cos_ref, sin_ref,
                    qmla_ref, kmla_ref, naq_ref, nak_ref, nav_ref):
    d = D_MODEL
    x = x_ref[0]
    shift = mod_ref[0, :, 0:d]
    scale = mod_ref[0, :, d:2 * d]
    a = (_rms(x) * (1.0 + scale) + shift).astype(BF16)
    p = _dot(a, win_ref[...])
    cos = cos_ref[...]
    sin = sin_ref[...]
    mla_scale = (MLA_NOPE + MLA_ROPE) ** -0.5 * LOG2_E

    cq = (_rms(p[:, 0:MLA_Q_RANK]) * qn_ref[...]).astype(BF16)
    q = _dot(cq, wq_ref[...])
    n_nope = MLA_HEADS * MLA_NOPE
    q_lat = _dot(q[:, 0:n_nope].astype(BF16), wuk_ref[...])
    for h in range(MLA_HEADS):
        qr = _rope_tile(q[:, n_nope + h * LANES:n_nope + (h + 1) * LANES], cos, sin, MLA_ROPE // 4, ROPE_LANE)
        qmla_ref[0, h, :, 0:LANES] = (q_lat[:, h * LANES:(h + 1) * LANES] * mla_scale).astype(BF16)
        qmla_ref[0, h, :, LANES:2 * LANES] = (qr * mla_scale).astype(BF16)

    ckv = _rms(p[:, MLA_Q_RANK:MLA_Q_RANK + MLA_KV_RANK]) * kvn_ref[...]
    kr = _rope_tile(p[:, 3 * LANES:4 * LANES], cos, sin, MLA_ROPE // 4, ROPE_LANE)
    lane = lax.broadcasted_iota(jnp.int32, kr.shape, 1)
    kr = jnp.where(lane == ONES_LANE, 1.0, kr)
    kmla_ref[0, :, 0:LANES] = ckv.astype(BF16)
    kmla_ref[0, :, LANES:2 * LANES] = kr.astype(BF16)

    w = NA_HEADS * HEAD_DIM
    naq_ref[0] = (p[:, 4 * LANES:4 * LANES + w] * (HEAD_DIM ** -0.5 * LOG2_E)).astype(BF16)
    nak_ref[0] = p[:, 4 * LANES + w:4 * LANES + 2 * w].astype(BF16)
    nav_ref[0] = p[:, 4 * LANES + 2 * w:4 * LANES + 3 * w].astype(BF16)


def _inproj0(x, mods, grp, win, qn, wq, wuk, kvn, cos, sin, tm):
    b, s, d = x.shape
    nt = s // tm
    w = NA_HEADS * HEAD_DIM
    const = lambda bi, i: (0, 0)
    return pl.pallas_call(
        _inproj0_kernel,
        out_shape=(jax.ShapeDtypeStruct((b, MLA_HEADS, s, MLA_QK_PAD), BF16),
                   jax.ShapeDtypeStruct((b, s, MLA_QK_PAD), BF16),
                   jax.ShapeDtypeStruct((b, s, w), BF16),
                   jax.ShapeDtypeStruct((b, s, w), BF16),
                   jax.ShapeDtypeStruct((b, s, w), BF16)),
        grid=(b, nt),
        in_specs=[pl.BlockSpec((1, tm, d), lambda bi, i: (bi, i, 0)),
                  pl.BlockSpec((1, 1, mods.shape[2]), lambda bi, i: (grp(bi), 0, 0)),
                  pl.BlockSpec(win.shape, const),
                  pl.BlockSpec(qn.shape, const),
                  pl.BlockSpec(wq.shape, const),
                  pl.BlockSpec(wuk.shape, const),
                  pl.BlockSpec(kvn.shape, const),
                  pl.BlockSpec((tm, LANES), lambda bi, i: (i, 0)),
                  pl.BlockSpec((tm, LANES), lambda bi, i: (i, 0))],
        out_specs=(pl.BlockSpec((1, MLA_HEADS, tm, MLA_QK_PAD), lambda bi, i: (bi, 0, i, 0)),
                   pl.BlockSpec((1, tm, MLA_QK_PAD), lambda bi, i: (bi, i, 0)),
                   pl.BlockSpec((1, tm, w), lambda bi, i: (bi, i, 0)),
                   pl.BlockSpec((1, tm, w), lambda bi, i: (bi, i, 0)),
                   pl.BlockSpec((1, tm, w), lambda bi, i: (bi, i, 0))),
        compiler_params=_cparams(("parallel", "parallel")),
        name="inproj_even",
    )(x, mods, win, qn, wq, wuk, kvn, cos, sin)


def _mla_t_kernel(q_ref, k_ref, vt_ref, wuvt_ref, o_ref, s0_ref, s1_ref, p0_ref, p1_ref, m0_ref, m1_ref, oh_ref):
    n_heads, tq = q_ref.shape[1], q_ref.shape[2]
    nk, rc = s0_ref.shape
    n_items = (tq // rc) * n_heads
    s_refs, p_refs, m_refs = (s0_ref, s1_ref), (p0_ref, p1_ref), (m0_ref, m1_ref)
    tk = MLA_KEY_CHUNK if nk % MLA_KEY_CHUNK == 0 else 2 * LANES
    rg = 128

    def item(i):
        if isinstance(i, int):
            return i % n_heads, i // n_heads
        return lax.rem(i, n_heads), lax.div(i, n_heads)

    def round_(r, par):
        static = isinstance(r, int)
        do_qk = not static or r < n_items
        do_sm = not static or 1 <= r <= n_items
        do_pv = not static or 2 <= r <= n_items + 1
        if do_qk:
            h_q, c_q = item(r)
            row_q = c_q * rc if static else pl.multiple_of(c_q * rc, rc)
            q = q_ref[0, h_q, pl.ds(row_q, rc), :]
        if do_sm:
            m8 = m_refs[1 - par][...]
        if do_pv:
            acc = None
        for c in range(nk // tk):
            keys = slice(c * tk, (c + 1) * tk)
            if do_qk:
                st = _dot_nt(k_ref[0, keys, :], q)
                s_refs[par][keys, :] = st
                cm = jnp.max(st.reshape(tk // 8, 8, rc), axis=0)
                m_refs[par][...] = cm if c == 0 else jnp.maximum(m_refs[par][...], cm)
            if do_sm:
                for g in range(tk // rg):
                    m = m8 + _zero_of(st[g * rg:g * rg + 8, :]) if do_qk else m8
                    blk = slice(c * tk + g * rg, c * tk + (g + 1) * rg)
                    sb = s_refs[1 - par][blk, :].reshape(rg // 8, 8, rc)
                    p_refs[1 - par][blk, :] = jnp.exp2(sb - m[None]).reshape(rg, rc).astype(BF16)
            if do_pv:
                part = _dot(vt_ref[0, :, keys], p_refs[par][keys, :])
                acc = part if acc is None else acc + part
        if do_qk:
            mx = m_refs[par][...]
            m_refs[par][...] = jnp.broadcast_to(jnp.max(mx, axis=0, keepdims=True), mx.shape)
        if do_pv:
            h_v, c_v = item(r - 2)
            l = acc[MLA_KV_RANK + ONES_LANE:MLA_KV_RANK + ONES_LANE + 1, :]
            oh_ref[c_v, h_v] = (acc[0:MLA_KV_RANK, :] / l).astype(BF16)

    round_(0, 0)
    round_(1, 1)
    n_full = max(n_items - 2, 0)

    def pair(i, carry):
        r = 2 + 2 * i
        round_(r, 0)
        round_(r + 1, 1)
        return carry

    lax.fori_loop(0, n_full // 2, pair, 0)
    for r in range(2 + 2 * (n_full // 2), n_items):
        round_(r, r % 2)
    for r in range(max(n_items, 2), n_items + 2):
        round_(r, r % 2)

    for ci in range(tq // rc):
        out_t = jnp.concatenate([_dot(wuvt_ref[h], oh_ref[ci, h]) for h in range(n_heads)], axis=0)
        o_ref[0, ci * rc:(ci + 1) * rc, :] = out_t.T.astype(o_ref.dtype)


def _mla_t_attention(q, k, vt, wuvt, tq):
    b, h, sq, _ = q.shape
    nk = k.shape[1]
    rc = min(MLA_ITEM_ROWS, tq)
    assert nk % (2 * LANES) == 0 and tq % rc == 0 and sq % tq == 0
    wo = wuvt.shape[0] * wuvt.shape[1]
    one_buffer = pl.Buffered(1)
    return pl.pallas_call(
        _mla_t_kernel,
        out_shape=jax.ShapeDtypeStruct((b, sq, wo), BF16),
        grid=(b, sq // tq),
        in_specs=[pl.BlockSpec((1, h, tq, MLA_QK_PAD), lambda bi, i: (bi, 0, i, 0)),
                  pl.BlockSpec((1, nk, MLA_QK_PAD), lambda bi, i: (bi, 0, 0), pipeline_mode=one_buffer),
                  pl.BlockSpec((1, MLA_V_ROWS, nk), lambda bi, i: (bi, 0, 0), pipeline_mode=one_buffer),
                  _resident(wuvt.shape)],
        out_specs=pl.BlockSpec((1, tq, wo), lambda bi, i: (bi, i, 0)),
        scratch_shapes=[pltpu.VMEM((nk, rc), F32), pltpu.VMEM((nk, rc), F32),
                        pltpu.VMEM((nk, rc), BF16), pltpu.VMEM((nk, rc), BF16),
                        pltpu.VMEM((8, rc), F32), pltpu.VMEM((8, rc), F32),
                        pltpu.VMEM((tq // rc, h, MLA_KV_RANK, rc), BF16)],
        compiler_params=_cparams(("parallel", "arbitrary")),
        name="mla_attention",
    )(q, k, vt, wuvt)


def _split_heads(t):
    lane = lax.broadcasted_iota(jnp.int32, t.shape, 1)
    lo = lane < HEAD_DIM
    zero = jnp.zeros_like(t)
    return jnp.where(lo, t, zero), jnp.where(lo, zero, t)


def _stack_pair(lo_hi_ctx, lo_hi_lat):
    return jnp.concatenate([lo_hi_ctx[0], lo_hi_ctx[1], lo_hi_lat[0], lo_hi_lat[1]], axis=0)


def _na_kernel(q_ref, k_ref, v_ref, kc_ref, vc_ref, bias_ref, o_ref, *, n_rows):
    step = pl.program_id(2)
    nrb = n_rows // NA_RB
    n_keys = NA_SPAN * GRID_W
    n_ctx = kc_ref.shape[1]
    tq = NA_RB * GRID_W
    kcs = _split_heads(kc_ref[0])
    vcs = _split_heads(vc_ref[0])
    ind = _stack_pair((_pair_indicator(n_ctx, False), _pair_indicator(n_ctx, True)),
                      (_pair_indicator(n_keys, False), _pair_indicator(n_keys, True)))
    for i in range(NA_STEP_BLOCKS):
        rb = step * NA_STEP_BLOCKS + i
        case = jnp.minimum(rb, 1) + jnp.maximum(rb - (nrb - 2), 0)
        base = jnp.clip(rb * NA_RB - NA_KR // 2, 0, n_rows - NA_SPAN)
        start = pl.multiple_of(base * GRID_W, 256)
        q = q_ref[0, i * tq:(i + 1) * tq, :]
        k_st = _stack_pair(kcs, _split_heads(k_ref[0, pl.ds(start, n_keys), :]))
        v_st = _stack_pair(vcs, _split_heads(v_ref[0, pl.ds(start, n_keys), :]))
        s = _dot_nt(q, k_st)
        p_ctx, p_lat = [], []
        for par in range(2):
            t_c = _tiles(s[:, par * n_ctx:(par + 1) * n_ctx])
            off = 2 * n_ctx + par * n_keys
            t_l = _tiles(s[:, off:off + n_keys] + bias_ref[case, par])
            m = _rowmax128(t_c + t_l)
            p_ctx += _exp_tiles(t_c, m)
            p_lat += _exp_tiles(t_l, m)
        o = _dot(jnp.concatenate(p_ctx + p_lat, axis=1), jnp.concatenate([v_st, ind], axis=1))
        o_ref[0, i * tq:(i + 1) * tq, :] = (o[:, :LANES] / o[:, LANES:]).astype(o_ref.dtype)


def _na_attention(q, k, v, kc, vc, bias):
    b, s, w = q.shape
    n_rows = s // GRID_W
    nrb = n_rows // NA_RB
    tq = NA_STEP_BLOCKS * NA_RB * GRID_W
    n_ctx = kc.shape[1]
    return pl.pallas_call(
        functools.partial(_na_kernel, n_rows=n_rows),
        out_shape=jax.ShapeDtypeStruct((b, s, w), BF16),
        grid=(w // LANES, b, nrb // NA_STEP_BLOCKS),
        in_specs=[pl.BlockSpec((1, tq, LANES), lambda j, bi, r: (bi, r, j)),
                  pl.BlockSpec((1, s, LANES), lambda j, bi, r: (bi, 0, j)),
                  pl.BlockSpec((1, s, LANES), lambda j, bi, r: (bi, 0, j)),
                  pl.BlockSpec((1, n_ctx, LANES), lambda j, bi, r: (bi, 0, j)),
                  pl.BlockSpec((1, n_ctx, LANES), lambda j, bi, r: (bi, 0, j)),
                  pl.BlockSpec((3, 2, NA_RB * GRID_W, NA_SPAN * GRID_W), lambda j, bi, r: (0, j, 0, 0))],
        out_specs=pl.BlockSpec((1, tq, LANES), lambda j, bi, r: (bi, r, j)),
        compiler_params=_cparams(("parallel", "parallel", "arbitrary")),
        name="na_attention",
    )(q, k, v, kc, vc, bias)


def _pair_ctx_kernel(q_ref, kc_ref, vc_ref, o_ref):
    q = q_ref[0]
    n_ctx = kc_ref.shape[1]
    kcs = _split_heads(kc_ref[0])
    vcs = _split_heads(vc_ref[0])
    s = _dot_nt(q, jnp.concatenate(kcs, axis=0))
    p = []
    for par in range(2):
        t_c = _tiles(s[:, par * n_ctx:(par + 1) * n_ctx])
        p += _exp_tiles(t_c, _rowmax128(t_c))
    ind = jnp.concatenate([_pair_indicator(n_ctx, False), _pair_indicator(n_ctx, True)], axis=0)
    o = _dot(jnp.concatenate(p, axis=1), jnp.concatenate([jnp.concatenate(vcs, axis=0), ind], axis=1))
    o_ref[0] = (o[:, :LANES] / o[:, LANES:]).astype(o_ref.dtype)


def _pair_ctx_attention(q, kc, vc):
    b, n, w = q.shape
    spec = pl.BlockSpec((1, n, LANES), lambda bi, j: (bi, 0, j))
    return pl.pallas_call(
        _pair_ctx_kernel,
        out_shape=jax.ShapeDtypeStruct((b, n, w), BF16),
        grid=(b, w // LANES),
        in_specs=[spec, spec, spec],
        out_specs=spec,
        compiler_params=_cparams(("parallel", "parallel")),
        name="na_ctx_attention",
    )(q, kc, vc)


def _na_bias_table(rel_bias, n_rows):
    qc = np.arange(GRID_W)
    cstart = np.clip(qc - NA_KC // 2, 0, GRID_W - NA_KC)
    kc = np.arange(GRID_W)
    col_ok = (kc[None, :] >= cstart[:, None]) & (kc[None, :] < cstart[:, None] + NA_KC)
    n_dr, n_dc = 2 * NA_KR - 1, 2 * NA_KC - 1
    nh = rel_bias.shape[0]
    dc = kc[None, :] - qc[:, None] + NA_KC - 1
    sel_col = ((dc[None] == np.arange(n_dc)[:, None, None]) & col_ok[None]).astype(np.float32)
    toe = jnp.einsum('hdu,uqk->hdqk', rel_bias.astype(F32), jnp.asarray(sel_col), precision=lax.Precision.HIGHEST)
    toe = jnp.where(jnp.asarray(col_ok)[None, None], toe * LOG2_E, NEG_INF)
    toe = jnp.concatenate([toe, jnp.full((nh, 1, GRID_W, GRID_W), NEG_INF, F32)], axis=1)
    toe = jnp.concatenate([toe, toe], axis=-1)
    idx = np.full((3, NA_RB, NA_SPAN), n_dr, np.int32)
    nrb = n_rows // NA_RB
    for case, rb in enumerate((0, min(1, nrb - 1), nrb - 1)):
        r0 = rb * NA_RB
        base = int(np.clip(r0 - NA_KR // 2, 0, n_rows - NA_SPAN))
        for a in range(NA_RB):
            r = r0 + a
            rs = int(np.clip(r - NA_KR // 2, 0, n_rows - NA_KR))
            for t in range(NA_SPAN):
                kr = base + t
                if rs <= kr < rs + NA_KR:
                    idx[case, a, t] = kr - r + NA_KR - 1

    def assemble(toe_ref, o_ref):
        lo = lax.broadcasted_iota(jnp.int32, (GRID_W, LANES), 1) < GRID_W
        for case in range(3):
            for a in range(NA_RB):
                for tp in range(NA_SPAN // 2):
                    d0, d1 = int(idx[case, a, 2 * tp]), int(idx[case, a, 2 * tp + 1])
                    tile = toe_ref[0, d0] if d0 == d1 else jnp.where(lo, toe_ref[0, d0], toe_ref[0, d1])
                    o_ref[case, 0, a * GRID_W:(a + 1) * GRID_W, tp * LANES:(tp + 1) * LANES] = tile

    tq, tkeys = NA_RB * GRID_W, NA_SPAN * GRID_W
    return pl.pallas_call(
        assemble,
        out_shape=jax.ShapeDtypeStruct((3, nh, tq, tkeys), F32),
        grid=(nh,),
        in_specs=[pl.BlockSpec((1, n_dr + 1, GRID_W, LANES), lambda h: (h, 0, 0, 0))],
        out_specs=pl.BlockSpec((3, 1, tq, tkeys), lambda h: (0, h, 0, 0)),
        compiler_params=_cparams(("parallel",)),
        name="na_bias_table",
    )(toe)


def _ffn_kernel(*refs, n_attn, tf, final):
    h_ref, mod_ref = refs[0], refs[1]
    attn_refs = refs[2:2 + n_attn]
    wout_refs = refs[2 + n_attn:2 + 2 * n_attn]
    wgu_ref, wd_ref = refs[2 + 2 * n_attn:4 + 2 * n_attn]
    pos = 4 + 2 * n_attn
    fn_ref = refs[pos] if final else None
    pos += 1 if final else 0
    o_ref, a2_ref, f_ref = refs[pos:pos + 3]
    d = D_MODEL
    dff = wd_ref.shape[1]

    proj = _dot(attn_refs[0][0], wout_refs[0][...])
    for t in range(1, n_attn):
        proj = proj + _dot(attn_refs[t][0], wout_refs[t][...])
    h1 = h_ref[0] + mod_ref[0, :, 2 * d:3 * d] * proj
    o_ref[0] = h1
    a2_ref[...] = (_rms(h1) * (1.0 + mod_ref[0, :, 4 * d:5 * d]) + mod_ref[0, :, 3 * d:4 * d]).astype(BF16)

    for j in range(dff // tf):
        a2 = a2_ref[...]
        g = _dot(a2, wgu_ref[0, :, j * tf:(j + 1) * tf])
        u = _dot(a2, wgu_ref[0, :, dff + j * tf:dff + (j + 1) * tf])
        f_ref[:, j * tf:(j + 1) * tf] = (_silu(g) * u).astype(BF16)

    out = o_ref[0] + mod_ref[0, :, 5 * d:6 * d] * _dot(f_ref[...], wd_ref[0])
    if final:
        out = _rms(out) * fn_ref[...]
    o_ref[0] = out


def _resident(shape):
    return pl.BlockSpec(shape, lambda *_: (0,) * len(shape), pipeline_mode=pl.Buffered(1))


def _ffn_block(h, mods, grp, attns, wouts, wgu, wd, layer, final_norm, tm, tf):
    b, s, d = h.shape
    dff = wd.shape[1]
    n_attn = len(attns)
    final = final_norm is not None
    in_specs = [pl.BlockSpec((1, tm, d), lambda bi, i: (bi, i, 0)),
                pl.BlockSpec((1, 1, mods.shape[2]), lambda bi, i: (grp(bi), 0, 0))]
    in_specs += [pl.BlockSpec((1, tm, a.shape[2]), lambda bi, i: (bi, i, 0)) for a in attns]
    in_specs += [_resident(w.shape) for w in wouts]
    in_specs += [pl.BlockSpec((1,) + w.shape[1:], lambda *_: (layer, 0, 0), pipeline_mode=pl.Buffered(1))
                 for w in (wgu, wd)]
    args = [h, mods, *attns, *wouts, wgu, wd]
    if final:
        in_specs.append(_resident((1, d)))
        args.append(final_norm)
    return pl.pallas_call(
        functools.partial(_ffn_kernel, n_attn=n_attn, tf=tf, final=final),
        out_shape=jax.ShapeDtypeStruct((b, s, d), F32),
        grid=(b, s // tm),
        in_specs=in_specs,
        out_specs=pl.BlockSpec((1, tm, d), lambda bi, i: (bi, i, 0)),
        scratch_shapes=[pltpu.VMEM((tm, d), BF16), pltpu.VMEM((tm, dff), BF16)],
        compiler_params=_cparams(("parallel", "parallel")),
        name="outproj_ffn",
    )(*args)


def _inproj1_kernel(x_ref, mod_ref, win_ref, cos_ref, sin_ref, q_ref, k_ref, v_ref):
    d = D_MODEL
    x = x_ref[0]
    a = (_rms(x) * (1.0 + mod_ref[0, :, d:2 * d]) + mod_ref[0, :, 0:d]).astype(BF16)
    p = _dot(a, win_ref[...])
    cos = cos_ref[...]
    sin = sin_ref[...]
    nq = SWA_HEADS * HEAD_DIM
    quarter = HEAD_DIM // 4
    for t in range(nq // LANES):
        q = _rope_tile(p[:, t * LANES:(t + 1) * LANES], cos, sin, quarter)
        q_ref[0, :, t * LANES:(t + 1) * LANES] = (q * (HEAD_DIM ** -0.5 * LOG2_E)).astype(BF16)
    k = _rope_tile(p[:, nq:nq + LANES], cos, sin, quarter)
    v = p[:, nq + LANES:nq + 2 * LANES]
    lo = lax.broadcasted_iota(jnp.int32, k.shape, 1) < HEAD_DIM
    for ref, t in ((k_ref, k), (v_ref, v)):
        sw = pltpu.roll(t, HEAD_DIM, 1)
        tiles = (jnp.where(lo, t, 0.0), jnp.where(lo, 0.0, sw), jnp.where(lo, sw, 0.0), jnp.where(lo, 0.0, t))
        for i, tile in enumerate(tiles):
            ref[0, :, i * LANES:(i + 1) * LANES] = tile.astype(BF16)


def _inproj1(x, mods, grp, win, cos, sin, tm):
    b, s, d = x.shape
    nq = SWA_HEADS * HEAD_DIM
    nkv = 2 * SWA_KV_HEADS * LANES
    return pl.pallas_call(
        _inproj1_kernel,
        out_shape=(jax.ShapeDtypeStruct((b, s, nq), BF16),
                   jax.ShapeDtypeStruct((b, s, nkv), BF16),
                   jax.ShapeDtypeStruct((b, s, nkv), BF16)),
        grid=(b, s // tm),
        in_specs=[pl.BlockSpec((1, tm, d), lambda bi, i: (bi, i, 0)),
                  pl.BlockSpec((1, 1, mods.shape[2]), lambda bi, i: (grp(bi), 0, 0)),
                  pl.BlockSpec(win.shape, lambda bi, i: (0, 0)),
                  pl.BlockSpec((tm, LANES), lambda bi, i: (i, 0)),
                  pl.BlockSpec((tm, LANES), lambda bi, i: (i, 0))],
        out_specs=(pl.BlockSpec((1, tm, nq), lambda bi, i: (bi, i, 0)),
                   pl.BlockSpec((1, tm, nkv), lambda bi, i: (bi, i, 0)),
                   pl.BlockSpec((1, tm, nkv), lambda bi, i: (bi, i, 0))),
        compiler_params=_cparams(("parallel", "parallel")),
        name="inproj_odd",
    )(x, mods, win, cos, sin)


def _swa_kernel(q_ref, km_ref, k0_ref, kp_ref, vm_ref, v0_ref, vp_ref, kc_ref, vc_ref, sink_ref, o_ref, *, n_steps):
    step = pl.program_id(2)
    blk = SWA_BLOCK
    npair = q_ref.shape[2] // LANES
    kall = jnp.concatenate([km_ref[0], k0_ref[0], kp_ref[0]], axis=0)
    vall = jnp.concatenate([vm_ref[0], v0_ref[0], vp_ref[0]], axis=0)
    kc = kc_ref[0]
    vc = vc_ref[0]
    n_ctx = kc.shape[0]
    kc_st = jnp.concatenate([kc[:, :LANES], kc[:, LANES:]], axis=0)
    vc_st = jnp.concatenate([vc[:, :LANES], vc[:, LANES:]], axis=0)
    ind = _stack_pair((_pair_indicator(n_ctx, False), _pair_indicator(n_ctx, True)),
                      (_pair_indicator(3 * blk, False), _pair_indicator(3 * blk, True)))
    r = lax.broadcasted_iota(jnp.int32, (blk, 3 * blk), 0)
    c = lax.broadcasted_iota(jnp.int32, (blk, 3 * blk), 1)
    cc = c % blk
    lane = lax.broadcasted_iota(jnp.int32, (npair * blk, LANES), 1)
    lo = lane < HEAD_DIM

    def band(left_off, right_off):
        ok = ((c >= blk) & (c < 2 * blk)) | ((c < blk) & (cc >= r + left_off)) | ((c >= 2 * blk) & (cc <= r - right_off))
        neg = jnp.where(ok, 0.0, NEG_INF).astype(F32)
        return jnp.concatenate([neg] * npair, axis=0)

    for t in range(SWA_GROUP):
        left_off = jnp.where(step >= 1, 0, blk) if t == 0 else 0
        right_off = jnp.where(step <= n_steps - 2, 0, blk) if t == SWA_GROUP - 1 else 0
        neg = band(left_off, right_off)
        qb = q_ref[0, t * blk:(t + 1) * blk, :]
        qst = jnp.concatenate([qb[:, i * LANES:(i + 1) * LANES] for i in range(npair)], axis=0)
        kw = kall[t * blk:(t + 3) * blk]
        vw = vall[t * blk:(t + 3) * blk]
        k_st = jnp.concatenate([kc_st, kw[:, :LANES], kw[:, LANES:]], axis=0)
        v_st = jnp.concatenate([vc_st, vw[:, :LANES], vw[:, LANES:]], axis=0)
        s = _dot_nt(qst, k_st)
        p_ctx, p_lat, p_sink = [], [], []
        for par in range(2):
            sink = sink_ref[0, par]
            t_c = _tiles(s[:, par * n_ctx:(par + 1) * n_ctx])
            off = 2 * n_ctx + par * 3 * blk
            t_l = _tiles(s[:, off:off + 3 * blk] + neg)
            m = jnp.maximum(_rowmax128(t_c + t_l), sink)
            p_ctx += _exp_tiles(t_c, m)
            p_lat += _exp_tiles(t_l, m)
            p_sink.append(jnp.exp2(sink - m))
        o = _dot(jnp.concatenate(p_ctx + p_lat, axis=1), jnp.concatenate([v_st, ind], axis=1))
        o = o[:, :LANES] / (o[:, LANES:] + jnp.where(lo, p_sink[0], p_sink[1]))
        for i in range(npair):
            o_ref[0, t * blk:(t + 1) * blk, i * LANES:(i + 1) * LANES] = o[i * blk:(i + 1) * blk].astype(o_ref.dtype)


def _swa_attention(q, kvar, vvar, kc, vc, sink_tab):
    b, s, nq = q.shape
    blk = SWA_BLOCK
    nb = s // blk
    n_steps = nb // SWA_GROUP
    gw = nq // SWA_KV_HEADS
    n_ctx = kc.shape[1]
    prev = lambda bi, g, j: (bi, jnp.maximum(SWA_GROUP * j - 1, 0), g)
    cur = lambda bi, g, j: (bi, j, g)
    nxt = lambda bi, g, j: (bi, jnp.minimum(SWA_GROUP * (j + 1), nb - 1), g)
    edge_spec = lambda f: pl.BlockSpec((1, blk, 2 * LANES), f)
    main_spec = pl.BlockSpec((1, SWA_GROUP * blk, 2 * LANES), cur)
    ctx_spec = pl.BlockSpec((1, n_ctx, 2 * LANES), lambda bi, g, j: (bi, 0, g))
    return pl.pallas_call(
        functools.partial(_swa_kernel, n_steps=n_steps),
        out_shape=jax.ShapeDtypeStruct((b, s, nq), BF16),
        grid=(b, SWA_KV_HEADS, n_steps),
        in_specs=[pl.BlockSpec((1, SWA_GROUP * blk, gw), cur),
                  edge_spec(prev), main_spec, edge_spec(nxt),
                  edge_spec(prev), main_spec, edge_spec(nxt),
                  ctx_spec, ctx_spec,
                  pl.BlockSpec((1, 2, sink_tab.shape[2], LANES), lambda bi, g, j: (g, 0, 0, 0))],
        out_specs=pl.BlockSpec((1, SWA_GROUP * blk, gw), cur),
        compiler_params=_cparams(("parallel", "parallel", "arbitrary")),
        name="swa_attention",
    )(q, kvar, kvar, kvar, vvar, vvar, vvar, kc, vc, sink_tab)


def _rope_tables(s, dim, pad_to, first_lane=0):
    n_rows = s // GRID_W
    half = dim // 2
    inv = ROPE_BASE ** (-jnp.arange(0, half, 2, dtype=F32) / half)
    ar = jnp.arange(n_rows, dtype=jnp.int32).astype(F32)[:, None] * inv
    ac = jnp.arange(GRID_W, dtype=jnp.int32).astype(F32)[:, None] * inv
    expand_r = lambda a: jnp.repeat(a, GRID_W, axis=0)
    expand_c = lambda a: jnp.tile(a, (n_rows, 1))
    cos = jnp.concatenate([expand_r(jnp.cos(ar))] * 2 + [expand_c(jnp.cos(ac))] * 2, axis=1)
    sin = jnp.concatenate([expand_r(-jnp.sin(ar)), expand_r(jnp.sin(ar)),
                           expand_c(-jnp.sin(ac)), expand_c(jnp.sin(ac))], axis=1)
    if pad_to > dim:
        before, after = first_lane, pad_to - dim - first_lane
        cos = jnp.concatenate([jnp.ones((s, before), F32), cos, jnp.ones((s, after), F32)], axis=1)
        sin = jnp.concatenate([jnp.zeros((s, before), F32), sin, jnp.zeros((s, after), F32)], axis=1)
    reps = LANES // cos.shape[1]
    return jnp.tile(cos, (1, reps)), jnp.tile(sin, (1, reps))


def _even_weights(w_in, w_q_up, w_uk, w_uv, w_out):
    d = w_in.shape[0]
    n_lat = MLA_Q_RANK + MLA_KV_RANK
    win = jnp.concatenate([w_in[:, :n_lat], jnp.zeros((d, ROPE_LANE), w_in.dtype), w_in[:, n_lat:MLA_IN],
                           jnp.zeros((d, LANES - ROPE_LANE - MLA_ROPE), w_in.dtype), w_in[:, MLA_IN:]], axis=1)
    wq3 = w_q_up.reshape(MLA_Q_RANK, MLA_HEADS, MLA_NOPE + MLA_ROPE)
    nope = wq3[:, :, :MLA_NOPE].reshape(MLA_Q_RANK, MLA_HEADS * MLA_NOPE)
    rope = jnp.pad(wq3[:, :, MLA_NOPE:], ((0, 0), (0, 0), (ROPE_LANE, LANES - ROPE_LANE - MLA_ROPE)))
    rope = rope.reshape(MLA_Q_RANK, MLA_HEADS * LANES)
    wq = jnp.concatenate([nope, rope], axis=1)
    eye = jnp.eye(MLA_HEADS, dtype=w_uk.dtype)
    wuk = jnp.einsum('hcn,hg->hngc', w_uk, eye).reshape(MLA_HEADS * MLA_NOPE, MLA_HEADS * MLA_KV_RANK)
    wuvt = jnp.swapaxes(w_uv, 1, 2)
    n_mla = MLA_HEADS * MLA_V
    return (win.astype(BF16), wq.astype(BF16), wuk.astype(BF16), wuvt.astype(BF16),
            w_out[:n_mla].astype(BF16), w_out[n_mla:].astype(BF16))


def _sink_table(sinks):
    g = SWA_HEADS // SWA_KV_HEADS
    t = (sinks.astype(F32) * LOG2_E).reshape(SWA_KV_HEADS, g // 2, 2).transpose(0, 2, 1)
    t = jnp.broadcast_to(t[:, :, :, None, None], (SWA_KV_HEADS, 2, g // 2, SWA_BLOCK, LANES))
    return t.reshape(SWA_KV_HEADS, 2, (g // 2) * SWA_BLOCK, LANES)


def kernel(x, c, ctx, c_ctx, mod_w, mod_b, even_w_in, mla_q_norm, mla_kv_norm, mla_w_q_up, mla_w_uk, mla_w_uv,
           na_rel_bias, even_w_out, odd_w_in, swa_sinks, odd_w_out, ffn_w_gate_up, ffn_w_down, final_norm):
    b, s, d = x.shape
    n_ctx = ctx.shape[1]
    n_rows = s // GRID_W
    assert d == D_MODEL and s % (NA_STEP_BLOCKS * NA_RB * GRID_W) == 0 and s % (SWA_GROUP * SWA_BLOCK) == 0
    assert mod_w.shape[0] == 2 and b <= 4

    ctx_grp = 4
    c8 = jnp.zeros((8, d), F32).at[:b].set(c.astype(F32)).at[ctx_grp].set(c_ctx.astype(F32))
    mods = _modulation(c8, mod_w.astype(F32), mod_b.astype(F32))
    mods0 = mods[0].reshape(8, 1, 6 * d)
    mods1 = mods[1].reshape(8, 1, 6 * d)
    lat_grp = lambda bi: bi
    ctx_g = lambda bi: ctx_grp

    tm = min(512, s)
    tf = 256
    wgu = ffn_w_gate_up.astype(BF16)
    wdn = ffn_w_down.astype(BF16)

    win, wq, wuk, wuvt, wo_mla, wo_na = _even_weights(even_w_in[0], mla_w_q_up[0], mla_w_uk[0], mla_w_uv[0],
                                                       even_w_out[0])
    qn = mla_q_norm[0].astype(F32).reshape(1, -1)
    kvn = mla_kv_norm[0].astype(F32).reshape(1, -1)
    cos_m, sin_m = _rope_tables(s, MLA_ROPE, LANES, ROPE_LANE)
    one_c = jnp.ones((n_ctx, LANES), F32)
    zero_c = jnp.zeros((n_ctx, LANES), F32)

    q_l, k_l, naq_l, nak_l, nav_l = _inproj0(x, mods0, lat_grp, win, qn, wq, wuk, kvn, cos_m, sin_m, tm)
    q_c, k_c, naq_c, nak_c, nav_c = _inproj0(ctx, mods0, ctx_g, win, qn, wq, wuk, kvn, one_c, zero_c, n_ctx)

    k_all = jnp.concatenate([k_c, k_l], axis=1)
    vt_all = jnp.swapaxes(k_all[:, :, :MLA_V_ROWS], 1, 2)
    o_mla_l = _mla_t_attention(q_l, k_all, vt_all, wuvt, min(1024, s))
    o_mla_c = _mla_t_attention(q_c, k_c, vt_all[:, :, :n_ctx], wuvt, n_ctx)

    bias = _na_bias_table(na_rel_bias[0], n_rows)
    o_na_l = _na_attention(naq_l, nak_l, nav_l, nak_c, nav_c, bias)
    o_na_c = _pair_ctx_attention(naq_c, nak_c, nav_c)

    h_lat = _ffn_block(x, mods0, lat_grp, [o_mla_l, o_na_l], [wo_mla, wo_na], wgu, wdn, 0, None, tm, tf)
    h_ctx = _ffn_block(ctx, mods0, ctx_g, [o_mla_c, o_na_c], [wo_mla, wo_na], wgu, wdn, 0, None, n_ctx, tf)

    assert SWA_KV_HEADS * HEAD_DIM == LANES
    assert SWA_WINDOW == SWA_BLOCK
    win1 = odd_w_in[0].astype(BF16)
    cos_s, sin_s = _rope_tables(s, HEAD_DIM, HEAD_DIM)
    q1, k1, v1 = _inproj1(h_lat, mods1, lat_grp, win1, cos_s, sin_s, tm)
    _, k1c, v1c = _inproj1(h_ctx, mods1, ctx_g, win1, one_c, zero_c, n_ctx)
    o_swa = _swa_attention(q1, k1, v1, k1c, v1c, _sink_table(swa_sinks[0]))
    fn = final_norm.astype(F32).reshape(1, d)
    return _ffn_block(h_lat, mods1, lat_grp, [o_swa], [odd_w_out[0].astype(BF16)], wgu, wdn, 1, fn, tm, tf)
```

```python
import functools

import numpy as np
import jax
import jax.numpy as jnp
from jax import lax
from jax.experimental import pallas as pl
from jax.experimental.pallas import tpu as pltpu

F32 = jnp.float32
BF16 = jnp.bfloat16

D_MODEL = 1024
GRID_W = 64
HEAD_DIM = 64
ROPE_BASE = 10000.0
EPS = 1e-6
NEG_INF = -1e30
LOG2_E = 1.4426950408889634

MLA_HEADS = 8
MLA_NOPE = 64
MLA_ROPE = 32
MLA_V = 64
MLA_Q_RANK = 256
MLA_KV_RANK = 128
MLA_IN = MLA_Q_RANK + MLA_KV_RANK + MLA_ROPE

NA_HEADS = 8
NA_KR = 8
NA_KC = 16
NA_RB = 4
NA_SPAN = 12
NA_STEP_BLOCKS = 16

SWA_HEADS = 16
SWA_KV_HEADS = 2
SWA_WINDOW = 128
SWA_BLOCK = 128
SWA_GROUP = 16

LANES = 128
MLA_QK_PAD = 256
MLA_ITEM_ROWS = 256
MLA_KEY_CHUNK = 768
ONES_LANE = 0
ROPE_LANE = 8
MLA_V_ROWS = 144
VMEM_LIMIT = 56 * 1024 * 1024


def _cparams(sem):
    return pltpu.CompilerParams(dimension_semantics=sem, vmem_limit_bytes=VMEM_LIMIT)


def _dot(a, b):
    return jnp.dot(a, b, preferred_element_type=F32)


def _dot_nt(a, b):
    return lax.dot_general(a, b, (((1,), (1,)), ((), ())), preferred_element_type=F32)


def _rms(x):
    return x * lax.rsqrt(jnp.mean(x * x, axis=-1, keepdims=True) + EPS)


def _silu(x):
    return x * (1.0 / (1.0 + jnp.exp(-x)))


def _rope_tile(x, cos, sin, quarter, first_lane=0):
    lane = lax.broadcasted_iota(jnp.int32, x.shape, 1)
    first = ((lane + (2 * quarter - first_lane)) % (2 * quarter)) < quarter
    swapped = jnp.where(first, pltpu.roll(x, LANES - quarter, 1), pltpu.roll(x, quarter, 1))
    return x * cos + swapped * sin


def _zero_of(x):
    u = lax.bitcast_convert_type(x, jnp.uint32)
    u = lax.shift_right_logical(lax.shift_right_logical(u, jnp.uint32(16)), jnp.uint32(16))
    return lax.bitcast_convert_type(u, F32)


def _tiles(s):
    return [s[:, j * LANES:(j + 1) * LANES] for j in range(s.shape[1] // LANES)]


def _rowmax128(tiles):
    mx = tiles[0]
    for t in tiles[1:]:
        mx = jnp.maximum(mx, t)
    return jnp.broadcast_to(jnp.max(mx, axis=1, keepdims=True), mx.shape)


def _exp_tiles(tiles, m):
    return [jnp.exp2(t - m).astype(BF16) for t in tiles]


def _pair_indicator(n, hi):
    lane = lax.broadcasted_iota(jnp.int32, (n, LANES), 1)
    return jnp.where((lane >= HEAD_DIM) == hi, 1.0, 0.0).astype(BF16)


def _mod_kernel(c_ref, w_ref, b_ref, o_ref):
    a = _silu(c_ref[...])
    o_ref[0] = jnp.dot(a, w_ref[0], precision=lax.Precision.HIGHEST, preferred_element_type=F32) + b_ref[0]


def _modulation(c8, mod_w, mod_b):
    depth, d, n = mod_w.shape
    tn = 1536
    return pl.pallas_call(
        _mod_kernel,
        out_shape=jax.ShapeDtypeStruct((depth, 8, n), F32),
        grid=(depth, n // tn),
        in_specs=[pl.BlockSpec((8, d), lambda l, j: (0, 0)),
                  pl.BlockSpec((1, d, tn), lambda l, j: (l, 0, j)),
                  pl.BlockSpec((1, 1, tn), lambda l, j: (l, 0, j))],
        out_specs=pl.BlockSpec((1, 8, tn), lambda l, j: (l, 0, j)),
        compiler_params=_cparams(("parallel", "parallel")),
        name="modulation",
    )(c8, mod_w, mod_b.reshape(depth, 1, n))


def _inproj0_kernel(x_ref, mod_ref, win_ref, qn_ref, wq_ref, wuk_ref, kvn_ref, cos_ref, sin_ref,
                    qmla_ref, kmla_ref, naq_ref, nak_ref, nav_ref):
    d = D_MODEL
    x = x_ref[0]
    shift = mod_ref[0, :, 0:d]
    scale = mod_ref[0, :, d:2 * d]
    a = (_rms(x) * (1.0 + scale) + shift).astype(BF16)
    p = _dot(a, win_ref[...])
    cos = cos_ref[...]
    sin = sin_ref[...]
    mla_scale = (MLA_NOPE + MLA_ROPE) ** -0.5 * LOG2_E

    cq = (_rms(p[:, 0:MLA_Q_RANK]) * qn_ref[...]).astype(BF16)
    q = _dot(cq, wq_ref[...])
    n_nope = MLA_HEADS * MLA_NOPE
    q_lat = _dot(q[:, 0:n_nope].astype(BF16), wuk_ref[...])
    for h in range(MLA_HEADS):
        qr = _rope_tile(q[:, n_nope + h * LANES:n_nope + (h + 1) * LANES], cos, sin, MLA_ROPE // 4, ROPE_LANE)
        qmla_ref[0, h, :, 0:LANES] = (q_lat[:, h * LANES:(h + 1) * LANES] * mla_scale).astype(BF16)
        qmla_ref[0, h, :, LANES:2 * LANES] = (qr * mla_scale).astype(BF16)

    ckv = _rms(p[:, MLA_Q_RANK:MLA_Q_RANK + MLA_KV_RANK]) * kvn_ref[...]
    kr = _rope_tile(p[:, 3 * LANES:4 * LANES], cos, sin, MLA_ROPE // 4, ROPE_LANE)
    lane = lax.broadcasted_iota(jnp.int32, kr.shape, 1)
    kr = jnp.where(lane == ONES_LANE, 1.0, kr)
    kmla_ref[0, :, 0:LANES] = ckv.astype(BF16)
    kmla_ref[0, :, LANES:2 * LANES] = kr.astype(BF16)

    w = NA_HEADS * HEAD_DIM
    naq_ref[0] = (p[:, 4 * LANES:4 * LANES + w] * (HEAD_DIM ** -0.5 * LOG2_E)).astype(BF16)
    nak_ref[0] = p[:, 4 * LANES + w:4 * LANES + 2 * w].astype(BF16)
    nav_ref[0] = p[:, 4 * LANES + 2 * w:4 * LANES + 3 * w].astype(BF16)


def _inproj0(x, mods, grp, win, qn, wq, wuk, kvn, cos, sin, tm):
    b, s, d = x.shape
    nt = s // tm
    w = NA_HEADS * HEAD_DIM
    const = lambda bi, i: (0, 0)
    return pl.pallas_call(
        _inproj0_kernel,
        out_shape=(jax.ShapeDtypeStruct((b, MLA_HEADS, s, MLA_QK_PAD), BF16),
                   jax.ShapeDtypeStruct((b, s, MLA_QK_PAD), BF16),
                   jax.ShapeDtypeStruct((b, s, w), BF16),
                   jax.ShapeDtypeStruct((b, s, w), BF16),
                   jax.ShapeDtypeStruct((b, s, w), BF16)),
        grid=(b, nt),
        in_specs=[pl.BlockSpec((1, tm, d), lambda bi, i: (bi, i, 0)),
                  pl.BlockSpec((1, 1, mods.shape[2]), lambda bi, i: (grp(bi), 0, 0)),
                  pl.BlockSpec(win.shape, const),
                  pl.BlockSpec(qn.shape, const),
                  pl.BlockSpec(wq.shape, const),
                  pl.BlockSpec(wuk.shape, const),
                  pl.BlockSpec(kvn.shape, const),
                  pl.BlockSpec((tm, LANES), lambda bi, i: (i, 0)),
                  pl.BlockSpec((tm, LANES), lambda bi, i: (i, 0))],
        out_specs=(pl.BlockSpec((1, MLA_HEADS, tm, MLA_QK_PAD), lambda bi, i: (bi, 0, i, 0)),
                   pl.BlockSpec((1, tm, MLA_QK_PAD), lambda bi, i: (bi, i, 0)),
                   pl.BlockSpec((1, tm, w), lambda bi, i: (bi, i, 0)),
                   pl.BlockSpec((1, tm, w), lambda bi, i: (bi, i, 0)),
                   pl.BlockSpec((1, tm, w), lambda bi, i: (bi, i, 0))),
        compiler_params=_cparams(("parallel", "parallel")),
        name="inproj_even",
    )(x, mods, win, qn, wq, wuk, kvn, cos, sin)


def _mla_t_kernel(q_ref, k_ref, vt_ref, wuvt_ref, o_ref, s0_ref, s1_ref, p0_ref, p1_ref, m0_ref, m1_ref, oh_ref):
    n_heads, tq = q_ref.shape[1], q_ref.shape[2]
    nk, rc = s0_ref.shape
    n_items = (tq // rc) * n_heads
    s_refs, p_refs, m_refs = (s0_ref, s1_ref), (p0_ref, p1_ref), (m0_ref, m1_ref)
    tk = MLA_KEY_CHUNK if nk % MLA_KEY_CHUNK == 0 else 2 * LANES
    rg = 128

    def item(i):
        if isinstance(i, int):
            return i % n_heads, i // n_heads
        return lax.rem(i, n_heads), lax.div(i, n_heads)

    def round_(r, par):
        static = isinstance(r, int)
        do_qk = not static or r < n_items
        do_sm = not static or 1 <= r <= n_items
        do_pv = not static or 2 <= r <= n_items + 1
        if do_qk:
            h_q, c_q = item(r)
            row_q = c_q * rc if static else pl.multiple_of(c_q * rc, rc)
            q = q_ref[0, h_q, pl.ds(row_q, rc), :]
        if do_sm:
            m8 = m_refs[1 - par][...]
        if do_pv:
            acc = None
        for c in range(nk // tk):
            keys = slice(c * tk, (c + 1) * tk)
            if do_qk:
                st = _dot_nt(k_ref[0, keys, :], q)
                s_refs[par][keys, :] = st
                cm = jnp.max(st.reshape(tk // 8, 8, rc), axis=0)
                m_refs[par][...] = cm if c == 0 else jnp.maximum(m_refs[par][...], cm)
            if do_sm:
                for g in range(tk // rg):
                    m = m8 + _zero_of(st[g * rg:g * rg + 8, :]) if do_qk else m8
                    blk = slice(c * tk + g * rg, c * tk + (g + 1) * rg)
                    sb = s_refs[1 - par][blk, :].reshape(rg // 8, 8, rc)
                    p_refs[1 - par][blk, :] = jnp.exp2(sb - m[None]).reshape(rg, rc).astype(BF16)
            if do_pv:
                part = _dot(vt_ref[0, :, keys], p_refs[par][keys, :])
                acc = part if acc is None else acc + part
        if do_qk:
            mx = m_refs[par][...]
            m_refs[par][...] = jnp.broadcast_to(jnp.max(mx, axis=0, keepdims=True), mx.shape)
        if do_pv:
            h_v, c_v = item(r - 2)
            l = acc[MLA_KV_RANK + ONES_LANE:MLA_KV_RANK + ONES_LANE + 1, :]
            oh_ref[c_v, h_v] = (acc[0:MLA_KV_RANK, :] / l).astype(BF16)

    round_(0, 0)
    round_(1, 1)
    n_full = max(n_items - 2, 0)

    def pair(i, carry):
        r = 2 + 2 * i
        round_(r, 0)
        round_(r + 1, 1)
        return carry

    lax.fori_loop(0, n_full // 2, pair, 0)
    for r in range(2 + 2 * (n_full // 2), n_items):
        round_(r, r % 2)
    for r in range(max(n_items, 2), n_items + 2):
        round_(r, r % 2)

    for ci in range(tq // rc):
        out_t = jnp.concatenate([_dot(wuvt_ref[h], oh_ref[ci, h]) for h in range(n_heads)], axis=0)
        o_ref[0, ci * rc:(ci + 1) * rc, :] = out_t.T.astype(o_ref.dtype)


def _mla_t_attention(q, k, vt, wuvt, tq):
    b, h, sq, _ = q.shape
    nk = k.shape[1]
    rc = min(MLA_ITEM_ROWS, tq)
    assert nk % (2 * LANES) == 0 and tq % rc == 0 and sq % tq == 0
    wo = wuvt.shape[0] * wuvt.shape[1]
    one_buffer = pl.Buffered(1)
    return pl.pallas_call(
        _mla_t_kernel,
        out_shape=jax.ShapeDtypeStruct((b, sq, wo), BF16),
        grid=(b, sq // tq),
        in_specs=[pl.BlockSpec((1, h, tq, MLA_QK_PAD), lambda bi, i: (bi, 0, i, 0)),
                  pl.BlockSpec((1, nk, MLA_QK_PAD), lambda bi, i: (bi, 0, 0), pipeline_mode=one_buffer),
                  pl.BlockSpec((1, MLA_V_ROWS, nk), lambda bi, i: (bi, 0, 0), pipeline_mode=one_buffer),
                  _resident(wuvt.shape)],
        out_specs=pl.BlockSpec((1, tq, wo), lambda bi, i: (bi, i, 0)),
        scratch_shapes=[pltpu.VMEM((nk, rc), F32), pltpu.VMEM((nk, rc), F32),
                        pltpu.VMEM((nk, rc), BF16), pltpu.VMEM((nk, rc), BF16),
                        pltpu.VMEM((8, rc), F32), pltpu.VMEM((8, rc), F32),
                        pltpu.VMEM((tq // rc, h, MLA_KV_RANK, rc), BF16)],
        compiler_params=_cparams(("parallel", "arbitrary")),
        name="mla_attention",
    )(q, k, vt, wuvt)


def _split_heads(t):
    lane = lax.broadcasted_iota(jnp.int32, t.shape, 1)
    lo = lane < HEAD_DIM
    zero = jnp.zeros_like(t)
    return jnp.where(lo, t, zero), jnp.where(lo, zero, t)


def _stack_pair(lo_hi_ctx, lo_hi_lat):
    return jnp.concatenate([lo_hi_ctx[0], lo_hi_ctx[1], lo_hi_lat[0], lo_hi_lat[1]], axis=0)


def _na_kernel(q_ref, k_ref, v_ref, kc_ref, vc_ref, bias_ref, o_ref, s0_ref, s1_ref, p0_ref, p1_ref, *, n_rows):
    step = pl.program_id(2)
    rg = 32
    nrb = n_rows // NA_RB
    n_keys = NA_SPAN * GRID_W
    n_ctx = kc_ref.shape[1]
    tq = NA_RB * GRID_W
    kcs = _split_heads(kc_ref[0])
    vcs = _split_heads(vc_ref[0])
    ind = _stack_pair((_pair_indicator(n_ctx, False), _pair_indicator(n_ctx, True)),
                      (_pair_indicator(n_keys, False), _pair_indicator(n_keys, True)))
    for i in range(NA_STEP_BLOCKS):
        rb = step * NA_STEP_BLOCKS + i
        case = jnp.minimum(rb, 1) + jnp.maximum(rb - (nrb - 2), 0)
        base = jnp.clip(rb * NA_RB - NA_KR // 2, 0, n_rows - NA_SPAN)
        start = pl.multiple_of(base * GRID_W, 256)
        q = q_ref[0, i * tq:(i + 1) * tq, :]
        k_st = _stack_pair(kcs, _split_heads(k_ref[0, pl.ds(start, n_keys), :]))
        v_st = _stack_pair(vcs, _split_heads(v_ref[0, pl.ds(start, n_keys), :]))
        s_ref, p_ref = (s0_ref, p0_ref) if i % 2 == 0 else (s1_ref, p1_ref)
        s_ref[...] = _dot_nt(q, k_st)
        for g in range(tq // rg):
            rows = slice(g * rg, (g + 1) * rg)
            for par in range(2):
                offs_c = [par * n_ctx + j * LANES for j in range(n_ctx // LANES)]
                offs_l = [2 * n_ctx + par * n_keys + j * LANES for j in range(n_keys // LANES)]
                t_c = [s_ref[rows, o_:o_ + LANES] for o_ in offs_c]
                t_l = [s_ref[rows, o_:o_ + LANES] + bias_ref[case, par, rows, j * LANES:(j + 1) * LANES]
                       for j, o_ in enumerate(offs_l)]
                m = _rowmax128(t_c + t_l)
                for o_, pt in zip(offs_c + offs_l, _exp_tiles(t_c + t_l, m)):
                    p_ref[rows, o_:o_ + LANES] = pt
        o = _dot(p_ref[...], jnp.concatenate([v_st, ind], axis=1))
        o_ref[0, i * tq:(i + 1) * tq, :] = (o[:, :LANES] / o[:, LANES:]).astype(o_ref.dtype)


def _na_attention(q, k, v, kc, vc, bias):
    b, s, w = q.shape
    n_rows = s // GRID_W
    nrb = n_rows // NA_RB
    tq = NA_STEP_BLOCKS * NA_RB * GRID_W
    n_ctx = kc.shape[1]
    n_cols = 2 * n_ctx + 2 * NA_SPAN * GRID_W
    return pl.pallas_call(
        functools.partial(_na_kernel, n_rows=n_rows),
        out_shape=jax.ShapeDtypeStruct((b, s, w), BF16),
        grid=(w // LANES, b, nrb // NA_STEP_BLOCKS),
        in_specs=[pl.BlockSpec((1, tq, LANES), lambda j, bi, r: (bi, r, j)),
                  pl.BlockSpec((1, s, LANES), lambda j, bi, r: (bi, 0, j)),
                  pl.BlockSpec((1, s, LANES), lambda j, bi, r: (bi, 0, j)),
                  pl.BlockSpec((1, n_ctx, LANES), lambda j, bi, r: (bi, 0, j)),
                  pl.BlockSpec((1, n_ctx, LANES), lambda j, bi, r: (bi, 0, j)),
                  pl.BlockSpec((3, 2, NA_RB * GRID_W, NA_SPAN * GRID_W), lambda j, bi, r: (0, j, 0, 0))],
        out_specs=pl.BlockSpec((1, tq, LANES), lambda j, bi, r: (bi, r, j)),
        scratch_shapes=[pltpu.VMEM((NA_RB * GRID_W, n_cols), F32), pltpu.VMEM((NA_RB * GRID_W, n_cols), F32),
                        pltpu.VMEM((NA_RB * GRID_W, n_cols), BF16), pltpu.VMEM((NA_RB * GRID_W, n_cols), BF16)],
        compiler_params=_cparams(("parallel", "parallel", "arbitrary")),
        name="na_attention",
    )(q, k, v, kc, vc, bias)


def _pair_ctx_kernel(q_ref, kc_ref, vc_ref, o_ref):
    q = q_ref[0]
    n_ctx = kc_ref.shape[1]
    kcs = _split_heads(kc_ref[0])
    vcs = _split_heads(vc_ref[0])
    s = _dot_nt(q, jnp.concatenate(kcs, axis=0))
    p = []
    for par in range(2):
        t_c = _tiles(s[:, par * n_ctx:(par + 1) * n_ctx])
        p += _exp_tiles(t_c, _rowmax128(t_c))
    ind = jnp.concatenate([_pair_indicator(n_ctx, False), _pair_indicator(n_ctx, True)], axis=0)
    o = _dot(jnp.concatenate(p, axis=1), jnp.concatenate([jnp.concatenate(vcs, axis=0), ind], axis=1))
    o_ref[0] = (o[:, :LANES] / o[:, LANES:]).astype(o_ref.dtype)


def _pair_ctx_attention(q, kc, vc):
    b, n, w = q.shape
    spec = pl.BlockSpec((1, n, LANES), lambda bi, j: (bi, 0, j))
    return pl.pallas_call(
        _pair_ctx_kernel,
        out_shape=jax.ShapeDtypeStruct((b, n, w), BF16),
        grid=(b, w // LANES),
        in_specs=[spec, spec, spec],
        out_specs=spec,
        compiler_params=_cparams(("parallel", "parallel")),
        name="na_ctx_attention",
    )(q, kc, vc)


def _na_bias_table(rel_bias, n_rows):
    qc = np.arange(GRID_W)
    cstart = np.clip(qc - NA_KC // 2, 0, GRID_W - NA_KC)
    kc = np.arange(GRID_W)
    col_ok = (kc[None, :] >= cstart[:, None]) & (kc[None, :] < cstart[:, None] + NA_KC)
    n_dr, n_dc = 2 * NA_KR - 1, 2 * NA_KC - 1
    nh = rel_bias.shape[0]
    dc = kc[None, :] - qc[:, None] + NA_KC - 1
    sel_col = ((dc[None] == np.arange(n_dc)[:, None, None]) & col_ok[None]).astype(np.float32)
    toe = jnp.einsum('hdu,uqk->hdqk', rel_bias.astype(F32), jnp.asarray(sel_col), precision=lax.Precision.HIGHEST)
    toe = jnp.where(jnp.asarray(col_ok)[None, None], toe * LOG2_E, NEG_INF)
    toe = jnp.concatenate([toe, jnp.full((nh, 1, GRID_W, GRID_W), NEG_INF, F32)], axis=1)
    toe = jnp.concatenate([toe, toe], axis=-1)
    idx = np.full((3, NA_RB, NA_SPAN), n_dr, np.int32)
    nrb = n_rows // NA_RB
    for case, rb in enumerate((0, min(1, nrb - 1), nrb - 1)):
        r0 = rb * NA_RB
        base = int(np.clip(r0 - NA_KR // 2, 0, n_rows - NA_SPAN))
        for a in range(NA_RB):
            r = r0 + a
            rs = int(np.clip(r - NA_KR // 2, 0, n_rows - NA_KR))
            for t in range(NA_SPAN):
                kr = base + t
                if rs <= kr < rs + NA_KR:
                    idx[case, a, t] = kr - r + NA_KR - 1

    def assemble(toe_ref, o_ref):
        lo = lax.broadcasted_iota(jnp.int32, (GRID_W, LANES), 1) < GRID_W
        for case in range(3):
            for a in range(NA_RB):
                for tp in range(NA_SPAN // 2):
                    d0, d1 = int(idx[case, a, 2 * tp]), int(idx[case, a, 2 * tp + 1])
                    tile = toe_ref[0, d0] if d0 == d1 else jnp.where(lo, toe_ref[0, d0], toe_ref[0, d1])
                    o_ref[case, 0, a * GRID_W:(a + 1) * GRID_W, tp * LANES:(tp + 1) * LANES] = tile

    tq, tkeys = NA_RB * GRID_W, NA_SPAN * GRID_W
    return pl.pallas_call(
        assemble,
        out_shape=jax.ShapeDtypeStruct((3, nh, tq, tkeys), F32),
        grid=(nh,),
        in_specs=[pl.BlockSpec((1, n_dr + 1, GRID_W, LANES), lambda h: (h, 0, 0, 0))],
        out_specs=pl.BlockSpec((3, 1, tq, tkeys), lambda h: (0, h, 0, 0)),
        compiler_params=_cparams(("parallel",)),
        name="na_bias_table",
    )(toe)


def _ffn_kernel(*refs, n_attn, tf, final):
    h_ref, mod_ref = refs[0], refs[1]
    attn_refs = refs[2:2 + n_attn]
    wout_refs = refs[2 + n_attn:2 + 2 * n_attn]
    wgu_ref, wd_ref = refs[2 + 2 * n_attn:4 + 2 * n_attn]
    pos = 4 + 2 * n_attn
    fn_ref = refs[pos] if final else None
    pos += 1 if final else 0
    o_ref, a2_ref, f_ref = refs[pos:pos + 3]
    d = D_MODEL
    dff = wd_ref.shape[1]

    proj = _dot(attn_refs[0][0], wout_refs[0][...])
    for t in range(1, n_attn):
        proj = proj + _dot(attn_refs[t][0], wout_refs[t][...])
    h1 = h_ref[0] + mod_ref[0, :, 2 * d:3 * d] * proj
    o_ref[0] = h1
    a2_ref[...] = (_rms(h1) * (1.0 + mod_ref[0, :, 4 * d:5 * d]) + mod_ref[0, :, 3 * d:4 * d]).astype(BF16)

    for j in range(dff // tf):
        a2 = a2_ref[...]
        g = _dot(a2, wgu_ref[0, :, j * tf:(j + 1) * tf])
        u = _dot(a2, wgu_ref[0, :, dff + j * tf:dff + (j + 1) * tf])
        f_ref[:, j * tf:(j + 1) * tf] = (_silu(g) * u).astype(BF16)

    out = o_ref[0] + mod_ref[0, :, 5 * d:6 * d] * _dot(f_ref[...], wd_ref[0])
    if final:
        out = _rms(out) * fn_ref[...]
    o_ref[0] = out


def _resident(shape):
    return pl.BlockSpec(shape, lambda *_: (0,) * len(shape), pipeline_mode=pl.Buffered(1))


def _ffn_block(h, mods, grp, attns, wouts, wgu, wd, layer, final_norm, tm, tf):
    b, s, d = h.shape
    dff = wd.shape[1]
    n_attn = len(attns)
    final = final_norm is not None
    in_specs = [pl.BlockSpec((1, tm, d), lambda bi, i: (bi, i, 0)),
                pl.BlockSpec((1, 1, mods.shape[2]), lambda bi, i: (grp(bi), 0, 0))]
    in_specs += [pl.BlockSpec((1, tm, a.shape[2]), lambda bi, i: (bi, i, 0)) for a in attns]
    in_specs += [_resident(w.shape) for w in wouts]
    in_specs += [pl.BlockSpec((1,) + w.shape[1:], lambda *_: (layer, 0, 0), pipeline_mode=pl.Buffered(1))
                 for w in (wgu, wd)]
    args = [h, mods, *attns, *wouts, wgu, wd]
    if final:
        in_specs.append(_resident((1, d)))
        args.append(final_norm)
    return pl.pallas_call(
        functools.partial(_ffn_kernel, n_attn=n_attn, tf=tf, final=final),
        out_shape=jax.ShapeDtypeStruct((b, s, d), F32),
        grid=(b, s // tm),
        in_specs=in_specs,
        out_specs=pl.BlockSpec((1, tm, d), lambda bi, i: (bi, i, 0)),
        scratch_shapes=[pltpu.VMEM((tm, d), BF16), pltpu.VMEM((tm, dff), BF16)],
        compiler_params=_cparams(("parallel", "parallel")),
        name="outproj_ffn",
    )(*args)


def _inproj1_kernel(x_ref, mod_ref, win_ref, cos_ref, sin_ref, q_ref, k_ref, v_ref):
    d = D_MODEL
    x = x_ref[0]
    a = (_rms(x) * (1.0 + mod_ref[0, :, d:2 * d]) + mod_ref[0, :, 0:d]).astype(BF16)
    p = _dot(a, win_ref[...])
    cos = cos_ref[...]
    sin = sin_ref[...]
    nq = SWA_HEADS * HEAD_DIM
    quarter = HEAD_DIM // 4
    for t in range(nq // LANES):
        q = _rope_tile(p[:, t * LANES:(t + 1) * LANES], cos, sin, quarter)
        q_ref[0, :, t * LANES:(t + 1) * LANES] = (q * (HEAD_DIM ** -0.5 * LOG2_E)).astype(BF16)
    k = _rope_tile(p[:, nq:nq + LANES], cos, sin, quarter)
    v = p[:, nq + LANES:nq + 2 * LANES]
    lo = lax.broadcasted_iota(jnp.int32, k.shape, 1) < HEAD_DIM
    for ref, t in ((k_ref, k), (v_ref, v)):
        sw = pltpu.roll(t, HEAD_DIM, 1)
        tiles = (jnp.where(lo, t, 0.0), jnp.where(lo, 0.0, sw), jnp.where(lo, sw, 0.0), jnp.where(lo, 0.0, t))
        for i, tile in enumerate(tiles):
            ref[0, :, i * LANES:(i + 1) * LANES] = tile.astype(BF16)


def _inproj1(x, mods, grp, win, cos, sin, tm):
    b, s, d = x.shape
    nq = SWA_HEADS * HEAD_DIM
    nkv = 2 * SWA_KV_HEADS * LANES
    return pl.pallas_call(
        _inproj1_kernel,
        out_shape=(jax.ShapeDtypeStruct((b, s, nq), BF16),
                   jax.ShapeDtypeStruct((b, s, nkv), BF16),
                   jax.ShapeDtypeStruct((b, s, nkv), BF16)),
        grid=(b, s // tm),
        in_specs=[pl.BlockSpec((1, tm, d), lambda bi, i: (bi, i, 0)),
                  pl.BlockSpec((1, 1, mods.shape[2]), lambda bi, i: (grp(bi), 0, 0)),
                  pl.BlockSpec(win.shape, lambda bi, i: (0, 0)),
                  pl.BlockSpec((tm, LANES), lambda bi, i: (i, 0)),
                  pl.BlockSpec((tm, LANES), lambda bi, i: (i, 0))],
        out_specs=(pl.BlockSpec((1, tm, nq), lambda bi, i: (bi, i, 0)),
                   pl.BlockSpec((1, tm, nkv), lambda bi, i: (bi, i, 0)),
                   pl.BlockSpec((1, tm, nkv), lambda bi, i: (bi, i, 0))),
        compiler_params=_cparams(("parallel", "parallel")),
        name="inproj_odd",
    )(x, mods, win, cos, sin)


def _swa_kernel(q_ref, km_ref, k0_ref, kp_ref, vm_ref, v0_ref, vp_ref, kc_ref, vc_ref, sink_ref, o_ref, *, n_steps):
    step = pl.program_id(2)
    blk = SWA_BLOCK
    npair = q_ref.shape[2] // LANES
    kall = jnp.concatenate([km_ref[0], k0_ref[0], kp_ref[0]], axis=0)
    vall = jnp.concatenate([vm_ref[0], v0_ref[0], vp_ref[0]], axis=0)
    kc = kc_ref[0]
    vc = vc_ref[0]
    n_ctx = kc.shape[0]
    kc_st = jnp.concatenate([kc[:, :LANES], kc[:, LANES:]], axis=0)
    vc_st = jnp.concatenate([vc[:, :LANES], vc[:, LANES:]], axis=0)
    ind = _stack_pair((_pair_indicator(n_ctx, False), _pair_indicator(n_ctx, True)),
                      (_pair_indicator(3 * blk, False), _pair_indicator(3 * blk, True)))
    r = lax.broadcasted_iota(jnp.int32, (blk, 3 * blk), 0)
    c = lax.broadcasted_iota(jnp.int32, (blk, 3 * blk), 1)
    cc = c % blk
    lane = lax.broadcasted_iota(jnp.int32, (npair * blk, LANES), 1)
    lo = lane < HEAD_DIM

    def band(left_off, right_off):
        ok = ((c >= blk) & (c < 2 * blk)) | ((c < blk) & (cc >= r + left_off)) | ((c >= 2 * blk) & (cc <= r - right_off))
        neg = jnp.where(ok, 0.0, NEG_INF).astype(F32)
        return jnp.concatenate([neg] * npair, axis=0)

    for t in range(SWA_GROUP):
        left_off = jnp.where(step >= 1, 0, blk) if t == 0 else 0
        right_off = jnp.where(step <= n_steps - 2, 0, blk) if t == SWA_GROUP - 1 else 0
        neg = band(left_off, right_off)
        qb = q_ref[0, t * blk:(t + 1) * blk, :]
        qst = jnp.concatenate([qb[:, i * LANES:(i + 1) * LANES] for i in range(npair)], axis=0)
        kw = kall[t * blk:(t + 3) * blk]
        vw = vall[t * blk:(t + 3) * blk]
        k_st = jnp.concatenate([kc_st, kw[:, :LANES], kw[:, LANES:]], axis=0)
        v_st = jnp.concatenate([vc_st, vw[:, :LANES], vw[:, LANES:]], axis=0)
        s = _dot_nt(qst, k_st)
        p_ctx, p_lat, p_sink = [], [], []
        for par in range(2):
            sink = sink_ref[0, par]
            t_c = _tiles(s[:, par * n_ctx:(par + 1) * n_ctx])
            off = 2 * n_ctx + par * 3 * blk
            t_l = _tiles(s[:, off:off + 3 * blk] + neg)
            m = jnp.maximum(_rowmax128(t_c + t_l), sink)
            p_ctx += _exp_tiles(t_c, m)
            p_lat += _exp_tiles(t_l, m)
            p_sink.append(jnp.exp2(sink - m))
        o = _dot(jnp.concatenate(p_ctx + p_lat, axis=1), jnp.concatenate([v_st, ind], axis=1))
        o = o[:, :LANES] / (o[:, LANES:] + jnp.where(lo, p_sink[0], p_sink[1]))
        for i in range(npair):
            o_ref[0, t * blk:(t + 1) * blk, i * LANES:(i + 1) * LANES] = o[i * blk:(i + 1) * blk].astype(o_ref.dtype)


def _swa_attention(q, kvar, vvar, kc, vc, sink_tab):
    b, s, nq = q.shape
    blk = SWA_BLOCK
    nb = s // blk
    n_steps = nb // SWA_GROUP
    gw = nq // SWA_KV_HEADS
    n_ctx = kc.shape[1]
    prev = lambda bi, g, j: (bi, jnp.maximum(SWA_GROUP * j - 1, 0), g)
    cur = lambda bi, g, j: (bi, j, g)
    nxt = lambda bi, g, j: (bi, jnp.minimum(SWA_GROUP * (j + 1), nb - 1), g)
    edge_spec = lambda f: pl.BlockSpec((1, blk, 2 * LANES), f)
    main_spec = pl.BlockSpec((1, SWA_GROUP * blk, 2 * LANES), cur)
    ctx_spec = pl.BlockSpec((1, n_ctx, 2 * LANES), lambda bi, g, j: (bi, 0, g))
    return pl.pallas_call(
        functools.partial(_swa_kernel, n_steps=n_steps),
        out_shape=jax.ShapeDtypeStruct((b, s, nq), BF16),
        grid=(b, SWA_KV_HEADS, n_steps),
        in_specs=[pl.BlockSpec((1, SWA_GROUP * blk, gw), cur),
                  edge_spec(prev), main_spec, edge_spec(nxt),
                  edge_spec(prev), main_spec, edge_spec(nxt),
                  ctx_spec, ctx_spec,
                  pl.BlockSpec((1, 2, sink_tab.shape[2], LANES), lambda bi, g, j: (g, 0, 0, 0))],
        out_specs=pl.BlockSpec((1, SWA_GROUP * blk, gw), cur),
        compiler_params=_cparams(("parallel", "parallel", "arbitrary")),
        name="swa_attention",
    )(q, kvar, kvar, kvar, vvar, vvar, vvar, kc, vc, sink_tab)


def _rope_tables(s, dim, pad_to, first_lane=0):
    n_rows = s // GRID_W
    half = dim // 2
    inv = ROPE_BASE ** (-jnp.arange(0, half, 2, dtype=F32) / half)
    ar = jnp.arange(n_rows, dtype=jnp.int32).astype(F32)[:, None] * inv
    ac = jnp.arange(GRID_W, dtype=jnp.int32).astype(F32)[:, None] * inv
    expand_r = lambda a: jnp.repeat(a, GRID_W, axis=0)
    expand_c = lambda a: jnp.tile(a, (n_rows, 1))
    cos = jnp.concatenate([expand_r(jnp.cos(ar))] * 2 + [expand_c(jnp.cos(ac))] * 2, axis=1)
    sin = jnp.concatenate([expand_r(-jnp.sin(ar)), expand_r(jnp.sin(ar)),
                           expand_c(-jnp.sin(ac)), expand_c(jnp.sin(ac))], axis=1)
    if pad_to > dim:
        before, after = first_lane, pad_to - dim - first_lane
        cos = jnp.concatenate([jnp.ones((s, before), F32), cos, jnp.ones((s, after), F32)], axis=1)
        sin = jnp.concatenate([jnp.zeros((s, before), F32), sin, jnp.zeros((s, after), F32)], axis=1)
    reps = LANES // cos.shape[1]
    return jnp.tile(cos, (1, reps)), jnp.tile(sin, (1, reps))


def _even_weights(w_in, w_q_up, w_uk, w_uv, w_out):
    d = w_in.shape[0]
    n_lat = MLA_Q_RANK + MLA_KV_RANK
    win = jnp.concatenate([w_in[:, :n_lat], jnp.zeros((d, ROPE_LANE), w_in.dtype), w_in[:, n_lat:MLA_IN],
                           jnp.zeros((d, LANES - ROPE_LANE - MLA_ROPE), w_in.dtype), w_in[:, MLA_IN:]], axis=1)
    wq3 = w_q_up.reshape(MLA_Q_RANK, MLA_HEADS, MLA_NOPE + MLA_ROPE)
    nope = wq3[:, :, :MLA_NOPE].reshape(MLA_Q_RANK, MLA_HEADS * MLA_NOPE)
    rope = jnp.pad(wq3[:, :, MLA_NOPE:], ((0, 0), (0, 0), (ROPE_LANE, LANES - ROPE_LANE - MLA_ROPE)))
    rope = rope.reshape(MLA_Q_RANK, MLA_HEADS * LANES)
    wq = jnp.concatenate([nope, rope], axis=1)
    eye = jnp.eye(MLA_HEADS, dtype=w_uk.dtype)
    wuk = jnp.einsum('hcn,hg->hngc', w_uk, eye).reshape(MLA_HEADS * MLA_NOPE, MLA_HEADS * MLA_KV_RANK)
    wuvt = jnp.swapaxes(w_uv, 1, 2)
    n_mla = MLA_HEADS * MLA_V
    return (win.astype(BF16), wq.astype(BF16), wuk.astype(BF16), wuvt.astype(BF16),
            w_out[:n_mla].astype(BF16), w_out[n_mla:].astype(BF16))


def _sink_table(sinks):
    g = SWA_HEADS // SWA_KV_HEADS
    t = (sinks.astype(F32) * LOG2_E).reshape(SWA_KV_HEADS, g // 2, 2).transpose(0, 2, 1)
    t = jnp.broadcast_to(t[:, :, :, None, None], (SWA_KV_HEADS, 2, g // 2, SWA_BLOCK, LANES))
    return t.reshape(SWA_KV_HEADS, 2, (g // 2) * SWA_BLOCK, LANES)


def kernel(x, c, ctx, c_ctx, mod_w, mod_b, even_w_in, mla_q_norm, mla_kv_norm, mla_w_q_up, mla_w_uk, mla_w_uv,
           na_rel_bias, even_w_out, odd_w_in, swa_sinks, odd_w_out, ffn_w_gate_up, ffn_w_down, final_norm):
    b, s, d = x.shape
    n_ctx = ctx.shape[1]
    n_rows = s // GRID_W
    assert d == D_MODEL and s % (NA_STEP_BLOCKS * NA_RB * GRID_W) == 0 and s % (SWA_GROUP * SWA_BLOCK) == 0
    assert mod_w.shape[0] == 2 and b <= 4

    ctx_grp = 4
    c8 = jnp.zeros((8, d), F32).at[:b].set(c.astype(F32)).at[ctx_grp].set(c_ctx.astype(F32))
    mods = _modulation(c8, mod_w.astype(F32), mod_b.astype(F32))
    mods0 = mods[0].reshape(8, 1, 6 * d)
    mods1 = mods[1].reshape(8, 1, 6 * d)
    lat_grp = lambda bi: bi
    ctx_g = lambda bi: ctx_grp

    tm = min(512, s)
    tf = 256
    wgu = ffn_w_gate_up.astype(BF16)
    wdn = ffn_w_down.astype(BF16)

    win, wq, wuk, wuvt, wo_mla, wo_na = _even_weights(even_w_in[0], mla_w_q_up[0], mla_w_uk[0], mla_w_uv[0],
                                                       even_w_out[0])
    qn = mla_q_norm[0].astype(F32).reshape(1, -1)
    kvn = mla_kv_norm[0].astype(F32).reshape(1, -1)
    cos_m, sin_m = _rope_tables(s, MLA_ROPE, LANES, ROPE_LANE)
    one_c = jnp.ones((n_ctx, LANES), F32)
    zero_c = jnp.zeros((n_ctx, LANES), F32)

    q_l, k_l, naq_l, nak_l, nav_l = _inproj0(x, mods0, lat_grp, win, qn, wq, wuk, kvn, cos_m, sin_m, tm)
    q_c, k_c, naq_c, nak_c, nav_c = _inproj0(ctx, mods0, ctx_g, win, qn, wq, wuk, kvn, one_c, zero_c, n_ctx)

    k_all = jnp.concatenate([k_c, k_l], axis=1)
    vt_all = jnp.swapaxes(k_all[:, :, :MLA_V_ROWS], 1, 2)
    o_mla_l = _mla_t_attention(q_l, k_all, vt_all, wuvt, min(1024, s))
    o_mla_c = _mla_t_attention(q_c, k_c, vt_all[:, :, :n_ctx], wuvt, n_ctx)

    bias = _na_bias_table(na_rel_bias[0], n_rows)
    o_na_l = _na_attention(naq_l, nak_l, nav_l, nak_c, nav_c, bias)
    o_na_c = _pair_ctx_attention(naq_c, nak_c, nav_c)

    h_lat = _ffn_block(x, mods0, lat_grp, [o_mla_l, o_na_l], [wo_mla, wo_na], wgu, wdn, 0, None, tm, tf)
    h_ctx = _ffn_block(ctx, mods0, ctx_g, [o_mla_c, o_na_c], [wo_mla, wo_na], wgu, wdn, 0, None, n_ctx, tf)

    assert SWA_KV_HEADS * HEAD_DIM == LANES
    assert SWA_WINDOW == SWA_BLOCK
    win1 = odd_w_in[0].astype(BF16)
    cos_s, sin_s = _rope_tables(s, HEAD_DIM, HEAD_DIM)
    q1, k1, v1 = _inproj1(h_lat, mods1, lat_grp, win1, cos_s, sin_s, tm)
    _, k1c, v1c = _inproj1(h_ctx, mods1, ctx_g, win1, one_c, zero_c, n_ctx)
    o_swa = _swa_attention(q1, k1, v1, k1c, v1c, _sink_table(swa_sinks[0]))
    fn = final_norm.astype(F32).reshape(1, d)
    return _ffn_block(h_lat, mods1, lat_grp, [o_swa], [odd_w_out[0].astype(BF16)], wgu, wdn, 1, fn, tm, tf)
```
